```python
import jax, jax.numpy as jnp
from jax import lax
import numpy as np

D_MODEL = 4096
BATCH = 4
SEQ = 2048
DEPTH = 2
DEC_BATCH = 8
DEC_SEQ = 8
PAST_LEN = 16384
PAGE_SIZE = 128

MIX_W = D_MODEL
A_W = MIX_W // 4
B_W = MIX_W // 4
C_W = MIX_W // 4
D_W = MIX_W - A_W - B_W - C_W
D_FF = ((8 * D_MODEL // 3 + 255) // 256) * 256
RMS_EPS = 1e-6

HEAD_DIM_A = 128
N_HEADS_A = A_W // HEAD_DIM_A
DILATED_BRANCHES = ((128, 1), (512, 4), (2048, 16))
WIN_MAX = 2048
Q_BLOCK = 128
ROPE_THETA = 10000.0

HEAD_DIM_B = 64
N_HEADS_B = B_W // HEAD_DIM_B
LORA_W = 64
LORA_A = 64
LORA_G = 160
GN_EPS = 64e-5
B_FEAT = 3 * B_W + LORA_W + LORA_A + LORA_G

CHUNK = 128
N_GROUPS_C = 8
GROUP_C = C_W // N_GROUPS_C

POOL_WINDOWS = (2, 4, 8, 16)
POOL_PREV = max(POOL_WINDOWS) - 1
POOL_GROUP = D_W // len(POOL_WINDOWS)

OFF_QA = 0
OFF_KA = A_W
OFF_VA = 2 * A_W
OFF_B = 3 * A_W
OFF_C = OFF_B + B_FEAT
OFF_D = OFF_C + 2 * C_W
PROJ_W = OFF_D + D_W

kernel_name = 'hybrid_dilated_rwkv7_gmlp_pool_decode_step'

F32 = jnp.float32


def _rmsnorm(x, g):
    xf = x.astype(F32)
    y = xf * lax.rsqrt(jnp.mean(xf * xf, axis=-1, keepdims=True) + RMS_EPS)
    return (y * g.astype(F32)).astype(x.dtype)


def _swiglu(x, wg, wu, wd):
    return (jax.nn.silu(x @ wg) * (x @ wu)) @ wd


def _rope(x, pos):
    half = x.shape[-1] // 2
    inv = ROPE_THETA ** (-jnp.arange(half, dtype=F32) / half)
    ang = pos.astype(F32)[:, None] * inv[None, :]
    cos = jnp.cos(ang)[None, :, None, :]
    sin = jnp.sin(ang)[None, :, None, :]
    x1 = x[..., :half].astype(F32)
    x2 = x[..., half:].astype(F32)
    return jnp.concatenate([x1 * cos - x2 * sin, x1 * sin + x2 * cos], axis=-1).astype(x.dtype)


def _dilated_branch(q, k_all, v_all, rows, window, dilation):
    offs = jnp.arange(window // dilation + 1) * dilation
    idx = rows[:, None] - offs[None, :]
    valid = idx >= 0
    idx = jnp.clip(idx, 0, k_all.shape[1] - 1)
    kg = k_all[:, idx]
    vg = v_all[:, idx]
    s = jnp.einsum('bqhd,bqnhd->bhqn', q, kg, preferred_element_type=F32) * (HEAD_DIM_A ** -0.5)
    s = jnp.where(valid[None, None], s, -jnp.inf)
    lse = jax.nn.logsumexp(s, axis=-1)
    p = jnp.exp(s - lse[..., None])
    o = jnp.einsum('bhqn,bqnhd->bqhd', p.astype(vg.dtype), vg, preferred_element_type=F32)
    return o, lse


def _dilated_attention(q, k_all, v_all, rows):
    B, T, H, hd = q.shape
    qb = min(Q_BLOCK, T)
    nb = -(-T // qb)
    pad = nb * qb - T
    qp = jnp.pad(q, ((0, 0), (0, pad), (0, 0), (0, 0))).reshape(B, nb, qb, H, hd).transpose(1, 0, 2, 3, 4)
    rp = jnp.pad(rows, (0, pad), mode='edge').reshape(nb, qb)

    def block(args):
        qblk, rblk = args
        outs, lses = [], []
        for window, dilation in DILATED_BRANCHES:
            o, lse = _dilated_branch(qblk, k_all, v_all, rblk, window, dilation)
            outs.append(o)
            lses.append(lse)
        lam = jax.nn.softmax(jnp.stack(lses), axis=0)
        lam = jnp.transpose(lam, (0, 1, 3, 2))[..., None]
        return jnp.sum(lam * jnp.stack(outs), axis=0).astype(q.dtype)

    out = lax.map(block, (qp, rp))
    return out.transpose(1, 0, 2, 3, 4).reshape(B, nb * qb, H, hd)[:, :T]


def _rwkv7_recurrence(r, w, k, v, kk, a, s0):
    def step(S, inp):
        r_t, w_t, k_t, v_t, kk_t, a_t = inp
        sa = jnp.einsum('bhvk,bhk->bhv', S, -kk_t)
        S = (S * w_t[:, :, None, :] + sa[..., None] * (kk_t * a_t)[:, :, None, :]
             + v_t[..., None] * k_t[:, :, None, :])
        return S, jnp.einsum('bhvk,bhk->bhv', S, r_t)

    xs = tuple(jnp.swapaxes(t.astype(F32), 0, 1) for t in (r, w, k, v, kk, a))
    S, o = lax.scan(step, s0.astype(F32), xs)
    return jnp.swapaxes(o, 0, 1), S


def _rwkv7_mix(fb, shift_prev, s0, mu, w0, w_up, a0, a_up, g_up, k_k, k_a, r_k, ln_w, ln_b):
    B, T, _ = fb.shape
    prev = jnp.concatenate([shift_prev[:, None, :].astype(fb.dtype), fb[:, :-1]], axis=1)
    fs = fb + mu * (prev - fb)
    r = fs[..., :B_W]
    k = fs[..., B_W:2 * B_W]
    v = fs[..., 2 * B_W:3 * B_W]
    zw = fs[..., 3 * B_W:3 * B_W + LORA_W]
    za = fs[..., 3 * B_W + LORA_W:3 * B_W + LORA_W + LORA_A]
    zg = fs[..., 3 * B_W + LORA_W + LORA_A:]
    w_log = -jax.nn.softplus(-(w0 + jnp.tanh(zw) @ w_up).astype(F32)) - 0.5
    decay = jnp.exp(-jnp.exp(w_log))
    a = jax.nn.sigmoid((a0 + za @ a_up).astype(F32))
    g = (jax.nn.sigmoid(zg) @ g_up).astype(F32)
    hs = (B, T, N_HEADS_B, HEAD_DIM_B)
    kf = k.astype(F32)
    kk = (kf * k_k).reshape(hs)
    kk = kk / jnp.maximum(jnp.sqrt(jnp.sum(kk * kk, axis=-1, keepdims=True)), 1e-12)
    kmod = (kf * (1.0 + (a - 1.0) * k_a)).reshape(hs)
    rf = r.astype(F32).reshape(hs)
    vf = v.astype(F32).reshape(hs)
    o, S = _rwkv7_recurrence(rf, decay.reshape(hs), kmod, vf, kk, a.reshape(hs), s0)
    mean = jnp.mean(o, axis=-1, keepdims=True)
    var = jnp.mean(jnp.square(o - mean), axis=-1, keepdims=True)
    o = ((o - mean) * lax.rsqrt(var + GN_EPS)).reshape(B, T, B_W) * ln_w + ln_b
    bonus = jnp.sum(rf * kmod * r_k, axis=-1, keepdims=True) * vf
    o = (o + bonus.reshape(B, T, B_W)) * g
    return o.astype(fb.dtype), S, fb[:, -1]


def _chunk_gmlp(u, v, w_s, b_s):
    B, T, _ = v.shape
    nc = -(-T // CHUNK)
    pad = nc * CHUNK - T
    vp = jnp.pad(v, ((0, 0), (0, pad), (0, 0))).reshape(B, nc, CHUNK, N_GROUPS_C, GROUP_C)
    ws = jnp.where(jnp.tril(jnp.ones((CHUNK, CHUNK), dtype=bool))[None], w_s, 0.0)
    s = jnp.einsum('gij,bcjgd->bcigd', ws, vp) + jnp.transpose(b_s)[None, None, :, :, None]
    s = s.reshape(B, nc * CHUNK, C_W)[:, :T]
    return u * s


def _pool_mix(p, prefix, pos, w_pool, scale):
    T = p.shape[1]
    ext = jnp.concatenate([prefix.astype(p.dtype), p], axis=1)
    cs = jnp.pad(jnp.cumsum(ext.astype(F32), axis=1), ((0, 0), (1, 0), (0, 0)))
    pf = p.astype(F32)
    outs = []
    for g, win in enumerate(POOL_WINDOWS):
        sl = slice(g * POOL_GROUP, (g + 1) * POOL_GROUP)
        hi = cs[:, POOL_PREV + 1:POOL_PREV + 1 + T, sl]
        lo = cs[:, POOL_PREV + 1 - win:POOL_PREV + 1 - win + T, sl]
        cnt = jnp.minimum(win, pos + 1).astype(F32)[None, :, None]
        pooled = (hi - lo) / cnt - pf[..., sl]
        outs.append(jnp.einsum('btc,cd->btd', pooled.astype(p.dtype), w_pool[g]))
    return jnp.concatenate(outs, axis=-1) * scale, ext[:, -POOL_PREV:]


def _layer(x, l, W, pos0, kv_prefix, shift_prev, wkv0, pool_prefix):
    B, T, _ = x.shape
    pos = pos0 + jnp.arange(T)
    h = x + 0.5 * _swiglu(_rmsnorm(x, W['ln_ffn1'][l]), W['w1_gate'][l], W['w1_up'][l], W['w1_down'][l])
    z = _rmsnorm(h, W['ln_mix'][l]) @ W['w_in'][l]
    hs = (B, T, N_HEADS_A, HEAD_DIM_A)
    qa = _rope(z[..., OFF_QA:OFF_KA].reshape(hs), pos)
    ka = _rope(z[..., OFF_KA:OFF_VA].reshape(hs), pos)
    va = z[..., OFF_VA:OFF_B].reshape(hs)
    if kv_prefix is None:
        k_all, v_all, n_prev = ka, va, 0
    else:
        k_prev, v_prev = kv_prefix
        k_all = jnp.concatenate([k_prev.astype(ka.dtype), ka], axis=1)
        v_all = jnp.concatenate([v_prev.astype(va.dtype), va], axis=1)
        n_prev = k_prev.shape[1]
    out_a = _dilated_attention(qa, k_all, v_all, n_prev + jnp.arange(T)).reshape(B, T, A_W)
    out_b, wkv_new, shift_new = _rwkv7_mix(
        z[..., OFF_B:OFF_C], shift_prev, wkv0, W['mu_b'][l], W['w0'][l], W['w_up'][l], W['a0'][l],
        W['a_up'][l], W['g_up'][l], W['k_k'][l], W['k_a'][l], W['r_k'][l], W['ln_x_w'][l], W['ln_x_b'][l])
    vc = z[..., OFF_C + C_W:OFF_D]
    out_c = _chunk_gmlp(z[..., OFF_C:OFF_C + C_W], vc, W['w_s'][l], W['b_s'][l])
    out_d, pool_new = _pool_mix(z[..., OFF_D:PROJ_W], pool_prefix, pos, W['w_pool'][l], W['pool_scale'][l])
    mix = jnp.concatenate([_rmsnorm(out_a, W['g_out_a'][l]), out_b,
                           _rmsnorm(out_c, W['g_out_c'][l]), _rmsnorm(out_d, W['g_out_d'][l])], axis=-1)
    h = h + mix @ W['w_out'][l]
    h = h + 0.5 * _swiglu(_rmsnorm(h, W['ln_ffn2'][l]), W['w2_gate'][l], W['w2_up'][l], W['w2_down'][l])
    return h, (ka, va, wkv_new, shift_new, pool_new, vc)


def setup_inputs(seed: int = 0) -> dict:
    key = jax.random.key(seed)
    ks = jax.random.split(key, 40)

    def nrm(i, shape, scale):
        return scale * jax.random.normal(ks[i], shape, F32)

    swa_buf = min(WIN_MAX, PAST_LEN)
    return {
        'x_prompt': nrm(0, (BATCH, SEQ, D_MODEL), 1.0),
        'x_sample': nrm(1, (DEC_BATCH, DEC_SEQ, D_MODEL), 1.0),
        'cache_k_swa': nrm(2, (DEPTH, DEC_BATCH, swa_buf, N_HEADS_A, HEAD_DIM_A), 1.0),
        'cache_v_swa': nrm(3, (DEPTH, DEC_BATCH, swa_buf, N_HEADS_A, HEAD_DIM_A), 1.0),
        'state_rwkv_wkv': nrm(4, (DEPTH, DEC_BATCH, N_HEADS_B, HEAD_DIM_B, HEAD_DIM_B), 0.3),
        'state_rwkv_shift': nrm(5, (DEPTH, DEC_BATCH, B_FEAT), 1.0),
        'state_pool': nrm(6, (DEPTH, DEC_BATCH, POOL_PREV, D_W), 1.0),
        'ln_ffn1': 1.0 + nrm(7, (DEPTH, D_MODEL), 0.02),
        'w1_gate': nrm(8, (DEPTH, D_MODEL, D_FF), D_MODEL ** -0.5),
        'w1_up': nrm(9, (DEPTH, D_MODEL, D_FF), D_MODEL ** -0.5),
        'w1_down': nrm(10, (DEPTH, D_FF, D_MODEL), D_FF ** -0.5),
        'ln_mix': 1.0 + nrm(11, (DEPTH, D_MODEL), 0.02),
        'w_in': nrm(12, (DEPTH, D_MODEL, PROJ_W), D_MODEL ** -0.5),
        'g_out_a': 1.0 + nrm(13, (DEPTH, A_W), 0.02),
        'mu_b': jax.random.uniform(ks[14], (DEPTH, B_FEAT), F32),
        'w0': jax.random.uniform(ks[15], (DEPTH, B_W), F32, -4.0, 1.0),
        'w_up': nrm(16, (DEPTH, LORA_W, B_W), LORA_W ** -0.5),
        'a0': nrm(17, (DEPTH, B_W), 0.5),
        'a_up': nrm(18, (DEPTH, LORA_A, B_W), LORA_A ** -0.5),
        'g_up': nrm(19, (DEPTH, LORA_G, B_W), LORA_G ** -0.5),
        'k_k': 0.85 + nrm(20, (DEPTH, B_W), 0.05),
        'k_a': 1.0 + nrm(21, (DEPTH, B_W), 0.05),
        'r_k': nrm(22, (DEPTH, N_HEADS_B, HEAD_DIM_B), 0.1),
        'ln_x_w': 1.0 + nrm(23, (DEPTH, B_W), 0.02),
        'ln_x_b': nrm(24, (DEPTH, B_W), 0.02),
        'w_s': nrm(25, (DEPTH, N_GROUPS_C, CHUNK, CHUNK), CHUNK ** -0.5),
        'b_s': 1.0 + nrm(26, (DEPTH, N_GROUPS_C, CHUNK), 0.02),
        'g_out_c': 1.0 + nrm(27, (DEPTH, C_W), 0.02),
        'w_pool': nrm(28, (DEPTH, len(POOL_WINDOWS), POOL_GROUP, POOL_GROUP), POOL_GROUP ** -0.5),
        'pool_scale': 1.0 + nrm(29, (DEPTH, D_W), 0.1),
        'g_out_d': 1.0 + nrm(30, (DEPTH, D_W), 0.02),
        'w_out': nrm(31, (DEPTH, MIX_W, D_MODEL), MIX_W ** -0.5),
        'ln_ffn2': 1.0 + nrm(32, (DEPTH, D_MODEL), 0.02),
        'w2_gate': nrm(33, (DEPTH, D_MODEL, D_FF), D_MODEL ** -0.5),
        'w2_up': nrm(34, (DEPTH, D_MODEL, D_FF), D_MODEL ** -0.5),
        'w2_down': nrm(35, (DEPTH, D_FF, D_MODEL), D_FF ** -0.5),
        'ln_final': 1.0 + nrm(36, (D_MODEL,), 0.02),
    }


def reference(x_prompt, x_sample, cache_k_swa, cache_v_swa, state_rwkv_wkv, state_rwkv_shift, state_pool,
              ln_ffn1, w1_gate, w1_up, w1_down, ln_mix, w_in, g_out_a, mu_b, w0, w_up, a0, a_up, g_up,
              k_k, k_a, r_k, ln_x_w, ln_x_b, w_s, b_s, g_out_c, w_pool, pool_scale, g_out_d, w_out,
              ln_ffn2, w2_gate, w2_up, w2_down, ln_final):
    W = dict(ln_ffn1=ln_ffn1, w1_gate=w1_gate, w1_up=w1_up, w1_down=w1_down, ln_mix=ln_mix, w_in=w_in,
             g_out_a=g_out_a, mu_b=mu_b, w0=w0, w_up=w_up, a0=a0, a_up=a_up, g_up=g_up, k_k=k_k, k_a=k_a,
             r_k=r_k, ln_x_w=ln_x_w, ln_x_b=ln_x_b, w_s=w_s, b_s=b_s, g_out_c=g_out_c, w_pool=w_pool,
             pool_scale=pool_scale, g_out_d=g_out_d, w_out=w_out, ln_ffn2=ln_ffn2, w2_gate=w2_gate,
             w2_up=w2_up, w2_down=w2_down)
    nbp = x_prompt.shape[0]
    yp, ys = x_prompt, x_sample
    p_states, s_states = [], []
    for l in range(DEPTH):
        yp, sp = _layer(yp, l, W, 0, None,
                        jnp.zeros((nbp, B_FEAT), yp.dtype),
                        jnp.zeros((nbp, N_HEADS_B, HEAD_DIM_B, HEAD_DIM_B), F32),
                        jnp.zeros((nbp, POOL_PREV, D_W), yp.dtype))
        ys, ss = _layer(ys, l, W, PAST_LEN, (cache_k_swa[l], cache_v_swa[l]),
                        state_rwkv_shift[l], state_rwkv_wkv[l], state_pool[l])
        p_states.append(sp)
        s_states.append(ss)
    keep = min(WIN_MAX, x_prompt.shape[1])
    new_k_swa_prompt = jnp.stack([s[0][:, -keep:] for s in p_states])
    new_v_swa_prompt = jnp.stack([s[1][:, -keep:] for s in p_states])
    new_wkv_prompt = jnp.stack([s[2] for s in p_states])
    new_shift_prompt = jnp.stack([s[3] for s in p_states])
    new_pool_prompt = jnp.stack([s[4] for s in p_states])
    new_k_swa_sample = jnp.stack([s[0] for s in s_states])
    new_v_swa_sample = jnp.stack([s[1] for s in s_states])
    new_wkv_sample = jnp.stack([s[2] for s in s_states])
    new_shift_sample = jnp.stack([s[3] for s in s_states])
    new_pool_sample = jnp.stack([s[4] for s in s_states])
    new_gmlp_v_sample = jnp.stack([s[5] for s in s_states])
    y_prompt = _rmsnorm(yp, ln_final)
    y_sample = _rmsnorm(ys, ln_final)
    return (y_prompt, y_sample, new_k_swa_prompt, new_v_swa_prompt, new_wkv_prompt, new_shift_prompt,
            new_pool_prompt, new_k_swa_sample, new_v_swa_sample, new_wkv_sample, new_shift_sample,
            new_pool_sample, new_gmlp_v_sample)
```

```python
import functools

import jax
import jax.numpy as jnp
from jax import lax
from jax.experimental import pallas as pl
from jax.experimental.pallas import tpu as pltpu

F32 = jnp.float32
BF16 = jnp.bfloat16

D_MODEL = 4096
DEPTH = 2
PAST_LEN = 16384
A_W = B_W = C_W = D_W = D_MODEL // 4
RMS_EPS = 1e-6
HEAD_DIM_A = 128
N_HEADS_A = A_W // HEAD_DIM_A
DILATED_BRANCHES = ((128, 1), (512, 4), (2048, 16))
WIN_MAX = 2048
ROPE_THETA = 10000.0
HEAD_DIM_B = 64
N_HEADS_B = B_W // HEAD_DIM_B
LORA_W, LORA_A, LORA_G = 64, 64, 160
LORA_ALL = LORA_W + LORA_A + LORA_G
GN_EPS = 64e-5
B_FEAT = 3 * B_W + LORA_ALL
CHUNK = 128
N_GROUPS_C = 8
POOL_WINDOWS = (2, 4, 8, 16)
POOL_PREV = max(POOL_WINDOWS) - 1
POOL_GROUP = D_W // len(POOL_WINDOWS)
OFF_B = 3 * A_W
OFF_C = OFF_B + B_FEAT
OFF_D = OFF_C + 2 * C_W

LANES = 128
SUBLANES = 8
VMEM_BYTES_V7X = 64 * 2**20
VMEM_INTERNAL_RESERVE = 12 * 2**20

Z_QA, Z_KA, Z_VA = 0, A_W, 2 * A_W
Z_RKV = 3 * A_W
Z_CU = Z_RKV + 3 * B_W
Z_CV = Z_CU + C_W
Z_D = Z_CV + C_W
Z_LORA = Z_D + D_W
LORA_PAD = 3 * LANES
Z_W = Z_LORA + LORA_PAD

NEG = -1e30


def _vmem_limit(block_bytes, scratch_bytes=0):
    need = 2 * block_bytes + scratch_bytes + VMEM_INTERNAL_RESERVE
    return int(min(max(need, 16 * 2**20), VMEM_BYTES_V7X - 4 * 2**20))


def _params(sem, block_bytes, scratch_bytes=0):
    return pltpu.CompilerParams(dimension_semantics=sem,
                                vmem_limit_bytes=_vmem_limit(block_bytes, scratch_bytes))


def _nbytes(shape, dtype):
    n = 1
    for s in shape:
        n *= s
    return n * jnp.dtype(dtype).itemsize


def _rmsnorm_body(x_ref, g_ref, o_ref):
    x = x_ref[...]
    ms = jnp.mean(x * x, axis=-1, keepdims=True)
    o_ref[...] = (x * lax.rsqrt(ms + RMS_EPS) * g_ref[...]).astype(o_ref.dtype)


def _rmsnorm(x, g, out_dtype):
    m, d = x.shape
    tm = min(m, 256)
    blk = _nbytes((tm, d), F32) + _nbytes((tm, d), out_dtype)
    return pl.pallas_call(
        _rmsnorm_body, grid=(m // tm,),
        in_specs=[pl.BlockSpec((tm, d), lambda i: (i, 0)), pl.BlockSpec((1, d), lambda i: (0, 0))],
        out_specs=pl.BlockSpec((tm, d), lambda i: (i, 0)),
        out_shape=jax.ShapeDtypeStruct((m, d), out_dtype),
        compiler_params=_params(("parallel",), blk), name="rmsnorm")(x, g.reshape(1, d))


def _ffn_up_body(x_ref, wg_ref, wu_ref, o_ref):
    x = x_ref[...]
    g = jnp.dot(x, wg_ref[...], preferred_element_type=F32)
    u = jnp.dot(x, wu_ref[...], preferred_element_type=F32)
    o_ref[...] = (g * jax.nn.sigmoid(g) * u).astype(o_ref.dtype)


def _ffn_up(x, wg, wu):
    m, k = x.shape
    n = wg.shape[1]
    tm = min(m, 1024)
    tn = 256
    blk = _nbytes((tm, k), BF16) + 2 * _nbytes((k, tn), BF16) + _nbytes((tm, tn), BF16)
    return pl.pallas_call(
        _ffn_up_body, grid=(m // tm, n // tn),
        in_specs=[pl.BlockSpec((tm, k), lambda i, j: (i, 0)),
                  pl.BlockSpec((k, tn), lambda i, j: (0, j)),
                  pl.BlockSpec((k, tn), lambda i, j: (0, j))],
        out_specs=pl.BlockSpec((tm, tn), lambda i, j: (i, j)),
        out_shape=jax.ShapeDtypeStruct((m, n), BF16),
        compiler_params=_params(("parallel", "arbitrary"), blk), name="ffn_up")(x, wg, wu)


def _mm_body(a_ref, b_ref, o_ref):
    o_ref[...] = jnp.dot(a_ref[...], b_ref[...], preferred_element_type=F32)


def _mm_res_body(a_ref, b_ref, r_ref, o_ref, *, scale):
    acc = jnp.dot(a_ref[...], b_ref[...], preferred_element_type=F32)
    o_ref[...] = r_ref[...] + scale * acc


def _matmul(a, b, res=None, scale=1.0, tm=1024, tn=512):
    m, k = a.shape
    n = b.shape[1]
    tm = min(m, tm)
    assert m % tm == 0 and n % tn == 0
    blk = _nbytes((tm, k), BF16) + _nbytes((k, tn), BF16) + _nbytes((tm, tn), F32)
    in_specs = [pl.BlockSpec((tm, k), lambda i, j: (i, 0)), pl.BlockSpec((k, tn), lambda i, j: (0, j))]
    args = [a, b]
    if res is None:
        body = _mm_body
    else:
        body = functools.partial(_mm_res_body, scale=scale)
        in_specs.append(pl.BlockSpec((tm, tn), lambda i, j: (i, j)))
        args.append(res)
        blk += _nbytes((tm, tn), F32)
    return pl.pallas_call(
        body, grid=(m // tm, n // tn), in_specs=in_specs,
        out_specs=pl.BlockSpec((tm, tn), lambda i, j: (i, j)),
        out_shape=jax.ShapeDtypeStruct((m, n), F32),
        compiler_params=_params(("parallel", "arbitrary"), blk), name="matmul")(*args)


def _rope(x, cos, sin_signed):
    return x * cos + pltpu.roll(x, HEAD_DIM_A // 2, 1) * sin_signed


def _branch_multiplicity(delta):
    c = jnp.zeros(delta.shape, F32)
    for window, dilation in DILATED_BRANCHES:
        assert dilation & (dilation - 1) == 0
        hit = jnp.where(delta <= window, 1.0, 0.0)
        if dilation > 1:
            hit = jnp.where((delta & (dilation - 1)) == 0, hit, 0.0)
        c = c + hit
    return jnp.where(delta >= 0, c, 0.0)


_TRANS_B = (((1,), (1,)), ((), ()))


def _attn_prompt_body(q_ref, k_ref, v_ref, cos_ref, sin_ref, o_ref, kout_ref, qs, ks, vs, *, t_len, tq):
    cos = cos_ref[...]
    sin = sin_ref[...]
    k = _rope(k_ref[0], cos, sin)
    kout_ref[0] = k
    ks[...] = k.astype(BF16)
    qs[...] = (_rope(q_ref[0], cos, sin) * (HEAD_DIM_A ** -0.5)).astype(BF16)
    vs[...] = v_ref[0].astype(BF16)
    rel = (lax.broadcasted_iota(jnp.int32, (tq, tq), 0) - lax.broadcasted_iota(jnp.int32, (tq, tq), 1))
    for i in range(t_len // tq):
        q = qs[i * tq:(i + 1) * tq, :]

        def body(j, carry, i=i, q=q):
            m, l, acc = carry
            off = pl.multiple_of(j * tq, tq)
            kj = ks[pl.ds(off, tq), :]
            vj = vs[pl.ds(off, tq), :]
            s = lax.dot_general(q, kj, _TRANS_B, preferred_element_type=F32)
            c = _branch_multiplicity(rel + (i * tq - off))
            sm = jnp.where(c > 0.0, s, NEG)
            m_new = jnp.maximum(m, jnp.max(sm, axis=-1, keepdims=True))
            alpha = jnp.exp(m - m_new)
            p = jnp.exp(sm - m_new) * c
            l = alpha * l + jnp.sum(p, axis=-1, keepdims=True)
            acc = alpha * acc + jnp.dot(p.astype(BF16), vj, preferred_element_type=F32)
            return m_new, l, acc

        init = (jnp.full((tq, 1), NEG, F32), jnp.zeros((tq, 1), F32), jnp.zeros((tq, HEAD_DIM_A), F32))
        _, l, acc = lax.fori_loop(0, i + 1, body, init)
        o_ref[0, i * tq:(i + 1) * tq, :] = acc / l


def _attn_sample_body(q_ref, k_ref, v_ref, kp_ref, vp_ref, cos_ref, sin_ref, o_ref, kout_ref, *, t_len, n_prev):
    cos = cos_ref[...]
    sin = sin_ref[...]
    k = _rope(k_ref[0], cos, sin)
    kout_ref[0] = k
    q = (_rope(q_ref[0], cos, sin) * (HEAD_DIM_A ** -0.5)).astype(BF16)
    pad = jnp.zeros((LANES - t_len, HEAD_DIM_A), F32)
    kn = jnp.concatenate([k, pad], axis=0).astype(BF16)
    vn = jnp.concatenate([v_ref[0], pad], axis=0).astype(BF16)
    s1 = lax.dot_general(q, kp_ref[0].astype(BF16), _TRANS_B, preferred_element_type=F32)
    d1 = (n_prev + lax.broadcasted_iota(jnp.int32, (t_len, n_prev), 0)
          - lax.broadcasted_iota(jnp.int32, (t_len, n_prev), 1))
    c1 = _branch_multiplicity(d1)
    s2 = lax.dot_general(q, kn, _TRANS_B, preferred_element_type=F32)
    d2 = (lax.broadcasted_iota(jnp.int32, (t_len, LANES), 0) - lax.broadcasted_iota(jnp.int32, (t_len, LANES), 1))
    c2 = _branch_multiplicity(d2)
    sm1 = jnp.where(c1 > 0.0, s1, NEG)
    sm2 = jnp.where(c2 > 0.0, s2, NEG)
    m = jnp.maximum(jnp.max(sm1, axis=-1, keepdims=True), jnp.max(sm2, axis=-1, keepdims=True))
    p1 = jnp.exp(sm1 - m) * c1
    p2 = jnp.exp(sm2 - m) * c2
    l = jnp.sum(p1, axis=-1, keepdims=True) + jnp.sum(p2, axis=-1, keepdims=True)
    acc = (jnp.dot(p1.astype(BF16), vp_ref[0].astype(BF16), preferred_element_type=F32)
           + jnp.dot(p2.astype(BF16), vn, preferred_element_type=F32))
    o_ref[0] = acc / l


def _rope_tables(pos0, t_len):
    half = HEAD_DIM_A // 2
    inv = ROPE_THETA ** (-jnp.arange(half, dtype=F32) / half)
    ang = (pos0 + jnp.arange(t_len)).astype(F32)[:, None] * inv[None, :]
    cos = jnp.cos(ang)
    sin = jnp.sin(ang)
    return jnp.concatenate([cos, cos], axis=-1), jnp.concatenate([-sin, sin], axis=-1)


def _attention(z3, pos0, kv_prefix):
    b, t, _ = z3.shape
    cos, sin = _rope_tables(pos0, t)
    hd = HEAD_DIM_A
    col = lambda base: (lambda bi, hi: (bi, 0, base // hd + hi))
    tab = pl.BlockSpec((t, hd), lambda bi, hi: (0, 0))
    zspecs = [pl.BlockSpec((1, t, hd), col(Z_QA)), pl.BlockSpec((1, t, hd), col(Z_KA)),
              pl.BlockSpec((1, t, hd), col(Z_VA))]
    out_specs = [pl.BlockSpec((1, t, hd), col(0)), pl.BlockSpec((1, t, hd), col(0))]
    out_shape = [jax.ShapeDtypeStruct((b, t, A_W), F32), jax.ShapeDtypeStruct((b, t, A_W), F32)]
    blk = 7 * _nbytes((t, hd), F32)
    if kv_prefix is None:
        tq = min(t, 256)
        body = functools.partial(_attn_prompt_body, t_len=t, tq=tq)
        return pl.pallas_call(
            body, grid=(b, N_HEADS_A), in_specs=zspecs + [tab, tab], out_specs=out_specs, out_shape=out_shape,
            scratch_shapes=[pltpu.VMEM((t, hd), BF16)] * 3,
            compiler_params=_params(("parallel", "parallel"), blk, 3 * _nbytes((t, hd), BF16)),
            name="attn_prompt")(z3, z3, z3, cos, sin)
    k_prev, v_prev = kv_prefix
    n_prev = k_prev.shape[1]
    assert t <= LANES and n_prev % LANES == 0
    k_prev = k_prev.reshape(b, n_prev, A_W)
    v_prev = v_prev.reshape(b, n_prev, A_W)
    body = functools.partial(_attn_sample_body, t_len=t, n_prev=n_prev)
    pspec = pl.BlockSpec((1, n_prev, hd), col(0))
    blk += 2 * _nbytes((n_prev, hd), F32)
    return pl.pallas_call(
        body, grid=(b, N_HEADS_A), in_specs=zspecs + [pspec, pspec, tab, tab], out_specs=out_specs,
        out_shape=out_shape, compiler_params=_params(("parallel", "parallel"), blk),
        name="attn_sample")(z3, z3, z3, k_prev, v_prev, cos, sin)


def _group_sum(x, ones_blockdiag):
    outs = []
    for j in range(x.shape[-1] // LANES):
        outs.append(jnp.dot(x[:, j * LANES:(j + 1) * LANES], ones_blockdiag, preferred_element_type=F32,
                            precision=lax.Precision.HIGHEST))
    return jnp.concatenate(outs, axis=-1)


def _softplus(y):
    return jnp.maximum(y, 0.0) + jnp.log1p(jnp.exp(-jnp.abs(y)))


def _rwkv_prep_body(x_ref, lo_ref, sx_ref, slo_ref, mux_ref, mulo_ref, w0_ref, a0_ref, kkp_ref, kap_ref, rk_ref,
                    wup_ref, aup_ref, gup_ref, ones_ref,
                    r_o, w_o, k_o, v_o, kk_o, b_o, g_o, bonus_o, last_x, last_lo, *, tt):
    @pl.when(pl.program_id(1) == 0)
    def _():
        last_x[0:1, :] = sx_ref[0]
        last_lo[0:1, :] = slo_ref[0]

    x = x_ref[0]
    lo = lo_ref[0]
    first = lax.broadcasted_iota(jnp.int32, (tt, 1), 0) == 0
    px = jnp.where(first, last_x[0:1, :], pltpu.roll(x, 1, 0))
    plo = jnp.where(first, last_lo[0:1, :], pltpu.roll(lo, 1, 0))
    last_x[0:1, :] = x[tt - 1:tt, :]
    last_lo[0:1, :] = lo[tt - 1:tt, :]
    fx = x + mux_ref[...] * (px - x)
    flo = lo + mulo_ref[...] * (plo - lo)
    r = fx[:, :B_W]
    k = fx[:, B_W:2 * B_W]
    v = fx[:, 2 * B_W:]
    zwa = flo[:, :LANES]
    zg = flo[:, LANES:]
    ones_bd = ones_ref[...]
    wl = w0_ref[...] + jnp.dot(jnp.tanh(zwa).astype(BF16), wup_ref[...], preferred_element_type=F32)
    w_log = -_softplus(-wl) - 0.5
    decay = jnp.exp(-jnp.exp(w_log))
    a = jax.nn.sigmoid(a0_ref[...] + jnp.dot(zwa.astype(BF16), aup_ref[...], preferred_element_type=F32))
    g = jnp.dot(jax.nn.sigmoid(zg).astype(BF16), gup_ref[...], preferred_element_type=F32)
    kk = k * kkp_ref[...]
    kk = kk / jnp.maximum(jnp.sqrt(_group_sum(kk * kk, ones_bd)), 1e-12)
    kmod = k * (1.0 + (a - 1.0) * kap_ref[...])
    r_o[0] = r
    w_o[0] = decay
    k_o[0] = kmod
    v_o[0] = v
    kk_o[0] = kk
    b_o[0] = kk * a
    g_o[0] = g
    bonus_o[0] = _group_sum(r * kmod * rk_ref[...], ones_bd) * v


def _rwkv_prep(z3, shift_prev, p):
    b, t, _ = z3.shape
    tt = min(t, 256)
    sx = shift_prev[:, None, :3 * B_W]
    slo = jnp.pad(shift_prev[:, None, 3 * B_W:], ((0, 0), (0, 0), (0, LORA_PAD - LORA_ALL)))
    row = lambda w: pl.BlockSpec((1, w), lambda bi, ti: (0, 0))
    full = lambda a: pl.BlockSpec(a.shape, lambda bi, ti: (0,) * a.ndim)
    in_specs = [pl.BlockSpec((1, tt, 3 * B_W), lambda bi, ti: (bi, ti, Z_RKV // (3 * B_W))),
                pl.BlockSpec((1, tt, LORA_PAD), lambda bi, ti: (bi, ti, Z_LORA // LORA_PAD)),
                pl.BlockSpec((1, 1, 3 * B_W), lambda bi, ti: (bi, 0, 0)),
                pl.BlockSpec((1, 1, LORA_PAD), lambda bi, ti: (bi, 0, 0)),
                row(3 * B_W), row(LORA_PAD), row(B_W), row(B_W), row(B_W), row(B_W), row(B_W),
                full(p['w_up']), full(p['a_up']), full(p['g_up']), full(p['ones_bd'])]
    ospec = pl.BlockSpec((1, tt, B_W), lambda bi, ti: (bi, ti, 0))
    blk = _nbytes((tt, 3 * B_W + LORA_PAD), F32) + 8 * _nbytes((tt, B_W), F32) + 2 * _nbytes((512, B_W), F32)
    return pl.pallas_call(
        functools.partial(_rwkv_prep_body, tt=tt), grid=(b, t // tt), in_specs=in_specs,
        out_specs=[ospec] * 8, out_shape=[jax.ShapeDtypeStruct((b, t, B_W), F32)] * 8,
        scratch_shapes=[pltpu.VMEM((SUBLANES, 3 * B_W), F32), pltpu.VMEM((SUBLANES, LORA_PAD), F32)],
        compiler_params=_params(("parallel", "arbitrary"), blk), name="rwkv_prep")(
            z3, z3, sx, slo, p['mu_x'], p['mu_lo'], p['w0'], p['a0'], p['k_k'], p['k_a'], p['r_k'],
            p['w_up'], p['a_up'], p['g_up'], p['ones_bd'])


N_PARTIAL = 4


def _rwkv_scan_body(r_ref, w_ref, k_ref, kk_ref, b_ref, v_ref, s0_ref, o_ref, sout_ref, state, *, tc, vp):
    @pl.when(pl.program_id(0) == 0)
    def _():
        state[...] = s0_ref[...]

    def step(t, carry):
        parts = [jnp.zeros((vp, LANES), F32)] * N_PARTIAL
        for k in range(HEAD_DIM_B):
            parts[k % N_PARTIAL] = parts[k % N_PARTIAL] + state[k] * kk_ref[t, pl.ds(k, 1), :]
        sa = -((parts[0] + parts[1]) + (parts[2] + parts[3]))
        vt = v_ref[t]
        parts = [jnp.zeros((vp, LANES), F32)] * N_PARTIAL
        for k in range(HEAD_DIM_B):
            s = (state[k] * w_ref[t, pl.ds(k, 1), :] + sa * b_ref[t, pl.ds(k, 1), :]
                 + vt * k_ref[t, pl.ds(k, 1), :])
            state[k] = s
            parts[k % N_PARTIAL] = parts[k % N_PARTIAL] + s * r_ref[t, pl.ds(k, 1), :]
        o_ref[t] = (parts[0] + parts[1]) + (parts[2] + parts[3])
        return carry

    lax.fori_loop(0, tc, step, 0)

    @pl.when(pl.program_id(0) == pl.num_programs(0) - 1)
    def _():
        sout_ref[...] = state[...]


def _rwkv_scan(r, w, k, kk, bv, v, s0):
    t, vp, _ = v.shape
    tc = min(t, 64)
    kspec = pl.BlockSpec((tc, HEAD_DIM_B, LANES), lambda i: (i, 0, 0))
    vspec = pl.BlockSpec((tc, vp, LANES), lambda i: (i, 0, 0))
    sspec = pl.BlockSpec((HEAD_DIM_B, vp, LANES), lambda i: (0, 0, 0))
    blk = 5 * _nbytes((tc, HEAD_DIM_B, LANES), F32) + 2 * _nbytes((tc, vp, LANES), F32) \
        + 2 * _nbytes((HEAD_DIM_B, vp, LANES), F32)
    return pl.pallas_call(
        functools.partial(_rwkv_scan_body, tc=tc, vp=vp), grid=(t // tc,),
        in_specs=[kspec] * 5 + [vspec, sspec], out_specs=[vspec, sspec],
        out_shape=[jax.ShapeDtypeStruct((t, vp, LANES), F32), jax.ShapeDtypeStruct((HEAD_DIM_B, vp, LANES), F32)],
        scratch_shapes=[pltpu.VMEM((HEAD_DIM_B, vp, LANES), F32)],
        compiler_params=_params(("arbitrary",), blk, _nbytes((HEAD_DIM_B, vp, LANES), F32)),
        name="rwkv_scan")(r, w, k, kk, bv, v, s0)


def _rwkv_mix(z3, shift_prev, wkv0, p):
    b, t, _ = z3.shape
    nh, hd = N_HEADS_B, HEAD_DIM_B
    r, w, kmod, v, kk, bv, g, bonus = _rwkv_prep(z3, shift_prev, p)
    dup = LANES // (b * nh)
    assert dup * b * nh == LANES and hd % dup == 0
    vp = hd // dup

    def key_layout(x):
        y = x.reshape(b, t, nh, hd).transpose(1, 3, 0, 2).reshape(t, hd, b * nh)
        return jnp.concatenate([y] * dup, axis=-1)

    def val_layout(x):
        return x.reshape(b, t, nh, dup, vp).transpose(1, 4, 3, 0, 2).reshape(t, vp, LANES)

    s0 = wkv0.reshape(b, nh, dup, vp, hd).transpose(4, 3, 2, 0, 1).reshape(hd, vp, LANES)
    o, s = _rwkv_scan(key_layout(r), key_layout(w), key_layout(kmod), key_layout(kk), key_layout(bv),
                      val_layout(v), s0)
    o = o.reshape(t, vp, dup, b, nh).transpose(3, 0, 4, 2, 1).reshape(b, t, B_W)
    s = s.reshape(hd, vp, dup, b, nh).transpose(3, 4, 2, 1, 0).reshape(b, nh, hd, hd)
    return o, bonus, g, s


def _gmlp_body(u_ref, v_ref, ws_ref, b_ref, o_ref, *, tc):
    keep = (lax.broadcasted_iota(jnp.int32, (CHUNK, CHUNK), 1) <= lax.broadcasted_iota(jnp.int32, (CHUNK, CHUNK), 0))
    for g in range(N_GROUPS_C):
        sl = slice(g * CHUNK, (g + 1) * CHUNK)
        w = jnp.where(keep, ws_ref[g], 0.0).astype(BF16)
        v = v_ref[0, :, sl]
        if tc < CHUNK:
            v = jnp.concatenate([v, jnp.zeros((CHUNK - tc, CHUNK), F32)], axis=0)
        s = jnp.dot(w, v.astype(BF16), preferred_element_type=F32) + b_ref[g]
        o_ref[0, :, sl] = u_ref[0, :, sl] * s[:tc]


def _gmlp(z3, w_s, b_s):
    b, t, _ = z3.shape
    tc = min(t, CHUNK)
    assert t % tc == 0
    blk = 3 * _nbytes((tc, C_W), F32) + 2 * _nbytes((N_GROUPS_C, CHUNK, CHUNK), F32)
    return pl.pallas_call(
        functools.partial(_gmlp_body, tc=tc), grid=(b, t // tc),
        in_specs=[pl.BlockSpec((1, tc, C_W), lambda bi, ci: (bi, ci, Z_CU // C_W)),
                  pl.BlockSpec((1, tc, C_W), lambda bi, ci: (bi, ci, Z_CV // C_W)),
                  pl.BlockSpec((N_GROUPS_C, CHUNK, CHUNK), lambda bi, ci: (0, 0, 0)),
                  pl.BlockSpec((N_GROUPS_C, CHUNK, 1), lambda bi, ci: (0, 0, 0))],
        out_specs=pl.BlockSpec((1, tc, C_W), lambda bi, ci: (bi, ci, 0)),
        out_shape=jax.ShapeDtypeStruct((b, t, C_W), F32),
        compiler_params=_params(("parallel", "parallel"), blk), name="gmlp")(z3, z3, w_s, b_s[:, :, None])


POOL_HALO = 16


def _pool_body(p_ref, pre_ref, w_ref, sc_ref, o_ref, ext, *, t_len, pos0, tc):
    ext[0:POOL_HALO, :] = pre_ref[0]
    ext[POOL_HALO:POOL_HALO + t_len, :] = p_ref[0]
    for c0 in range(0, t_len, tc):
        pos = pos0 + c0 + lax.broadcasted_iota(jnp.int32, (tc, 1), 0)
        for g, win in enumerate(POOL_WINDOWS):
            sl = slice(g * POOL_GROUP, (g + 1) * POOL_GROUP)
            base = POOL_HALO + c0
            acc = ext[base:base + tc, sl]
            for i in range(1, win):
                acc = acc + ext[base - i:base - i + tc, sl]
            cnt = jnp.minimum(win, pos + 1).astype(F32)
            pooled = acc / cnt - p_ref[0, c0:c0 + tc, sl]
            y = jnp.dot(pooled.astype(BF16), w_ref[g], preferred_element_type=F32)
            o_ref[0, c0:c0 + tc, sl] = y * sc_ref[:, sl]


def _pool(z3, prefix, pos0, w_pool, scale):
    b, t, _ = z3.shape
    assert POOL_PREV < POOL_HALO
    pre = jnp.pad(prefix, ((0, 0), (POOL_HALO - POOL_PREV, 0), (0, 0)))
    tc = min(t, 256)
    blk = 2 * _nbytes((t, D_W), F32) + _nbytes((POOL_HALO, D_W), F32) + _nbytes(w_pool.shape, BF16)
    scr = _nbytes((t + POOL_HALO, D_W), F32)
    return pl.pallas_call(
        functools.partial(_pool_body, t_len=t, pos0=pos0, tc=tc), grid=(b,),
        in_specs=[pl.BlockSpec((1, t, D_W), lambda bi: (bi, 0, Z_D // D_W)),
                  pl.BlockSpec((1, POOL_HALO, D_W), lambda bi: (bi, 0, 0)),
                  pl.BlockSpec(w_pool.shape, lambda bi: (0, 0, 0)),
                  pl.BlockSpec((1, D_W), lambda bi: (0, 0))],
        out_specs=pl.BlockSpec((1, t, D_W), lambda bi: (bi, 0, 0)),
        out_shape=jax.ShapeDtypeStruct((b, t, D_W), F32),
        scratch_shapes=[pltpu.VMEM((t + POOL_HALO, D_W), F32)],
        compiler_params=_params(("parallel",), blk, scr), name="pool")(z3, pre, w_pool, scale.reshape(1, D_W))


def _mix_body(oa_ref, ob_ref, bonus_ref, g_ref, oc_ref, od_ref, ga_ref, lnw_ref, lnb_ref, gc_ref, gd_ref, ones_ref,
              o_ref):
    def rms(x, gain):
        ms = jnp.mean(x * x, axis=-1, keepdims=True)
        return x * lax.rsqrt(ms + RMS_EPS) * gain

    ones_bd = ones_ref[...]
    o = ob_ref[...]
    mean = _group_sum(o, ones_bd) * (1.0 / HEAD_DIM_B)
    d = o - mean
    var = _group_sum(d * d, ones_bd) * (1.0 / HEAD_DIM_B)
    ob = d * lax.rsqrt(var + GN_EPS) * lnw_ref[...] + lnb_ref[...]
    ob = (ob + bonus_ref[...]) * g_ref[...]
    o_ref[:, 0:A_W] = rms(oa_ref[...], ga_ref[...]).astype(BF16)
    o_ref[:, A_W:A_W + B_W] = ob.astype(BF16)
    o_ref[:, A_W + B_W:A_W + B_W + C_W] = rms(oc_ref[...], gc_ref[...]).astype(BF16)
    o_ref[:, A_W + B_W + C_W:] = rms(od_ref[...], gd_ref[...]).astype(BF16)


def _mix(oa, ob, bonus, g, oc, od, ga, lnw, lnb, gc, gd, ones_bd):
    m = oa.shape[0]
    tm = min(m, 256)
    act = pl.BlockSpec((tm, A_W), lambda i: (i, 0))
    row = pl.BlockSpec((1, A_W), lambda i: (0, 0))
    blk = 6 * _nbytes((tm, A_W), F32) + _nbytes((tm, D_MODEL), BF16) + _nbytes((512, A_W), F32)
    r1 = lambda a: a.reshape(1, -1)
    return pl.pallas_call(
        _mix_body, grid=(m // tm,),
        in_specs=[act] * 6 + [row] * 5 + [pl.BlockSpec((LANES, LANES), lambda i: (0, 0))],
        out_specs=pl.BlockSpec((tm, D_MODEL), lambda i: (i, 0)),
        out_shape=jax.ShapeDtypeStruct((m, D_MODEL), BF16),
        compiler_params=_params(("parallel",), blk), name="mix")(
            oa, ob, bonus, g, oc, od, r1(ga), r1(lnw), r1(lnb), r1(gc), r1(gd), ones_bd)


def _prep_layer_weights(l, w):
    w_in = w['w_in'][l]
    zpad = jnp.zeros((D_MODEL, LORA_PAD - LORA_ALL), w_in.dtype)
    w_in_z = jnp.concatenate(
        [w_in[:, :OFF_B + 3 * B_W], w_in[:, OFF_C:], w_in[:, OFF_B + 3 * B_W:OFF_C], zpad], axis=1).astype(BF16)
    mu = w['mu_b'][l]
    pad_rows = lambda a, before, total: jnp.pad(a, ((before, total - before - a.shape[0]), (0, 0))).astype(BF16)
    ones_bd = jnp.kron(jnp.eye(LANES // HEAD_DIM_B, dtype=F32), jnp.ones((HEAD_DIM_B, HEAD_DIM_B), F32))
    rw = dict(
        mu_x=mu[None, :3 * B_W], mu_lo=jnp.pad(mu[None, 3 * B_W:], ((0, 0), (0, LORA_PAD - LORA_ALL))),
        w0=w['w0'][l][None], a0=w['a0'][l][None], k_k=w['k_k'][l][None], k_a=w['k_a'][l][None],
        r_k=w['r_k'][l].reshape(1, B_W),
        w_up=pad_rows(w['w_up'][l], 0, LANES), a_up=pad_rows(w['a_up'][l], LORA_W, LANES),
        g_up=pad_rows(w['g_up'][l], 0, LORA_PAD - LANES), ones_bd=ones_bd)
    return dict(
        w1_gate=w['w1_gate'][l].astype(BF16), w1_up=w['w1_up'][l].astype(BF16), w1_down=w['w1_down'][l].astype(BF16),
        w2_gate=w['w2_gate'][l].astype(BF16), w2_up=w['w2_up'][l].astype(BF16), w2_down=w['w2_down'][l].astype(BF16),
        w_in=w_in_z, w_out=w['w_out'][l].astype(BF16), w_pool=w['w_pool'][l].astype(BF16), rwkv=rw)


def _layer(x, l, w, wl, pos0, kv_prefix, shift_prev, wkv0, pool_prefix):
    b, t, _ = x.shape
    m = b * t
    x2 = x.reshape(m, D_MODEL)
    hid = _ffn_up(_rmsnorm(x2, w['ln_ffn1'][l], BF16), wl['w1_gate'], wl['w1_up'])
    h = _matmul(hid, wl['w1_down'], res=x2, scale=0.5, tm=512, tn=512)
    z = _matmul(_rmsnorm(h, w['ln_mix'][l], BF16), wl['w_in'], tm=1024, tn=640)
    z3 = z.reshape(b, t, Z_W)
    out_a, ka = _attention(z3, pos0, kv_prefix)
    o_b, bonus, gate, wkv_new = _rwkv_mix(z3, shift_prev, wkv0, wl['rwkv'])
    out_c = _gmlp(z3, w['w_s'][l], w['b_s'][l])
    out_d = _pool(z3, pool_prefix, pos0, wl['w_pool'], w['pool_scale'][l])
    r2 = lambda a: a.reshape(m, -1)
    mix = _mix(r2(out_a), r2(o_b), r2(bonus), r2(gate), r2(out_c), r2(out_d), w['g_out_a'][l], w['ln_x_w'][l],
               w['ln_x_b'][l], w['g_out_c'][l], w['g_out_d'][l], wl['rwkv']['ones_bd'])
    h = _matmul(mix, wl['w_out'], res=h, scale=1.0, tm=1024, tn=512)
    hid = _ffn_up(_rmsnorm(h, w['ln_ffn2'][l], BF16), wl['w2_gate'], wl['w2_up'])
    y = _matmul(hid, wl['w2_down'], res=h, scale=0.5, tm=512, tn=512)
    hs = (b, t, N_HEADS_A, HEAD_DIM_A)
    va = z3[..., Z_VA:Z_VA + A_W]
    shift_new = jnp.concatenate([z3[:, -1, Z_RKV:Z_RKV + 3 * B_W], z3[:, -1, Z_LORA:Z_LORA + LORA_ALL]], axis=-1)
    p = z3[..., Z_D:Z_D + D_W]
    pool_new = jnp.concatenate([pool_prefix, p], axis=1)[:, -POOL_PREV:]
    vc = z3[..., Z_CV:Z_CV + C_W]
    return y.reshape(b, t, D_MODEL), (ka.reshape(hs), va.reshape(hs), wkv_new, shift_new, pool_new, vc)


def kernel(x_prompt, x_sample, cache_k_swa, cache_v_swa, state_rwkv_wkv, state_rwkv_shift, state_pool, ln_ffn1, w1_gate, w1_up, w1_down, ln_mix, w_in, g_out_a, mu_b, w0, w_up, a0, a_up, g_up, k_k, k_a, r_k, ln_x_w, ln_x_b, w_s, b_s, g_out_c, w_pool, pool_scale, g_out_d, w_out, ln_ffn2, w2_gate, w2_up, w2_down, ln_final):
    w = dict(ln_ffn1=ln_ffn1, w1_gate=w1_gate, w1_up=w1_up, w1_down=w1_down, ln_mix=ln_mix, w_in=w_in,
             g_out_a=g_out_a, mu_b=mu_b, w0=w0, w_up=w_up, a0=a0, a_up=a_up, g_up=g_up, k_k=k_k, k_a=k_a,
             r_k=r_k, ln_x_w=ln_x_w, ln_x_b=ln_x_b, w_s=w_s, b_s=b_s, g_out_c=g_out_c, w_pool=w_pool,
             pool_scale=pool_scale, g_out_d=g_out_d, w_out=w_out, ln_ffn2=ln_ffn2, w2_gate=w2_gate,
             w2_up=w2_up, w2_down=w2_down)
    nbp, t_prompt, _ = x_prompt.shape
    yp, ys = x_prompt, x_sample
    p_states, s_states = [], []
    for l in range(DEPTH):
        wl = _prep_layer_weights(l, w)
        yp, sp = _layer(yp, l, w, wl, 0, None,
                        jnp.zeros((nbp, B_FEAT), F32),
                        jnp.zeros((nbp, N_HEADS_B, HEAD_DIM_B, HEAD_DIM_B), F32),
                        jnp.zeros((nbp, POOL_PREV, D_W), F32))
        ys, ss = _layer(ys, l, w, wl, PAST_LEN, (cache_k_swa[l], cache_v_swa[l]),
                        state_rwkv_shift[l], state_rwkv_wkv[l], state_pool[l])
        p_states.append(sp)
        s_states.append(ss)
    keep = min(WIN_MAX, t_prompt)
    stack = lambda states, i: jnp.stack([s[i] for s in states])
    y_prompt = _rmsnorm(yp.reshape(-1, D_MODEL), ln_final, F32).reshape(yp.shape)
    y_sample = _rmsnorm(ys.reshape(-1, D_MODEL), ln_final, F32).reshape(ys.shape)
    return (y_prompt, y_sample,
            jnp.stack([s[0][:, -keep:] for s in p_states]), jnp.stack([s[1][:, -keep:] for s in p_states]),
            stack(p_states, 2), stack(p_states, 3), stack(p_states, 4),
            stack(s_states, 0), stack(s_states, 1), stack(s_states, 2), stack(s_states, 3), stack(s_states, 4),
            stack(s_states, 5))
```

```python
import functools

import jax
import jax.numpy as jnp
from jax import lax
from jax.experimental import pallas as pl
from jax.experimental.pallas import tpu as pltpu

F32 = jnp.float32
BF16 = jnp.bfloat16

D_MODEL = 4096
DEPTH = 2
PAST_LEN = 16384
A_W = B_W = C_W = D_W = D_MODEL // 4
RMS_EPS = 1e-6
HEAD_DIM_A = 128
N_HEADS_A = A_W // HEAD_DIM_A
DILATED_BRANCHES = ((128, 1), (512, 4), (2048, 16))
WIN_MAX = 2048
ROPE_THETA = 10000.0
HEAD_DIM_B = 64
N_HEADS_B = B_W // HEAD_DIM_B
LORA_W, LORA_A, LORA_G = 64, 64, 160
LORA_ALL = LORA_W + LORA_A + LORA_G
GN_EPS = 64e-5
B_FEAT = 3 * B_W + LORA_ALL
CHUNK = 128
N_GROUPS_C = 8
POOL_WINDOWS = (2, 4, 8, 16)
POOL_PREV = max(POOL_WINDOWS) - 1
POOL_GROUP = D_W // len(POOL_WINDOWS)
OFF_B = 3 * A_W
OFF_C = OFF_B + B_FEAT
OFF_D = OFF_C + 2 * C_W

LANES = 128
SUBLANES = 8
VMEM_BYTES_V7X = 64 * 2**20
VMEM_INTERNAL_RESERVE = 12 * 2**20

Z_QA, Z_KA, Z_VA = 0, A_W, 2 * A_W
Z_RKV = 3 * A_W
Z_CU = Z_RKV + 3 * B_W
Z_CV = Z_CU + C_W
Z_D = Z_CV + C_W
Z_LORA = Z_D + D_W
LORA_PAD = 3 * LANES
MXU_COLS_V7X = 256
Z_W = -(-(Z_LORA + LORA_PAD) // (2 * MXU_COLS_V7X)) * (2 * MXU_COLS_V7X)

NEG = -1e30


def _vmem_limit(block_bytes, scratch_bytes=0):
    need = 2 * block_bytes + scratch_bytes + VMEM_INTERNAL_RESERVE
    return int(min(max(need, 16 * 2**20), VMEM_BYTES_V7X - 4 * 2**20))


def _params(sem, block_bytes, scratch_bytes=0):
    return pltpu.CompilerParams(dimension_semantics=sem,
                                vmem_limit_bytes=_vmem_limit(block_bytes, scratch_bytes))


def _nbytes(shape, dtype):
    n = 1
    for s in shape:
        n *= s
    return n * jnp.dtype(dtype).itemsize


def _rmsnorm_body(x_ref, g_ref, o_ref):
    x = x_ref[...]
    ms = jnp.mean(x * x, axis=-1, keepdims=True)
    o_ref[...] = (x * lax.rsqrt(ms + RMS_EPS) * g_ref[...]).astype(o_ref.dtype)


def _rmsnorm(x, g, out_dtype):
    m, d = x.shape
    tm = min(m, 256)
    blk = _nbytes((tm, d), F32) + _nbytes((tm, d), out_dtype)
    return pl.pallas_call(
        _rmsnorm_body, grid=(m // tm,),
        in_specs=[pl.BlockSpec((tm, d), lambda i: (i, 0)), pl.BlockSpec((1, d), lambda i: (0, 0))],
        out_specs=pl.BlockSpec((tm, d), lambda i: (i, 0)),
        out_shape=jax.ShapeDtypeStruct((m, d), out_dtype),
        compiler_params=_params(("parallel",), blk), name="rmsnorm")(x, g.reshape(1, d))


def _ffn_up_body(x_ref, wg_ref, wu_ref, o_ref):
    x = x_ref[...]
    g = jnp.dot(x, wg_ref[...], preferred_element_type=F32)
    u = jnp.dot(x, wu_ref[...], preferred_element_type=F32)
    o_ref[...] = (g * jax.nn.sigmoid(g) * u).astype(o_ref.dtype)


def _ffn_up(x, wg, wu, l):
    m, k = x.shape
    n = wg.shape[2]
    tm = min(m, 1024)
    tn = MXU_COLS_V7X
    blk = _nbytes((tm, k), BF16) + 2 * _nbytes((k, tn), BF16) + _nbytes((tm, tn), BF16)
    return pl.pallas_call(
        _ffn_up_body, grid=(m // tm, n // tn),
        in_specs=[pl.BlockSpec((tm, k), lambda i, j: (i, 0)),
                  pl.BlockSpec((None, k, tn), lambda i, j: (l, 0, j)),
                  pl.BlockSpec((None, k, tn), lambda i, j: (l, 0, j))],
        out_specs=pl.BlockSpec((tm, tn), lambda i, j: (i, j)),
        out_shape=jax.ShapeDtypeStruct((m, n), BF16),
        compiler_params=_params(("parallel", "arbitrary"), blk), name="ffn_up")(x, wg, wu)


def _mm_body(a_ref, b_ref, o_ref):
    o_ref[...] = jnp.dot(a_ref[...], b_ref[...], preferred_element_type=F32)


def _mm_res_body(a_ref, b_ref, r_ref, o_ref, *, scale):
    acc = jnp.dot(a_ref[...], b_ref[...], preferred_element_type=F32)
    o_ref[...] = r_ref[...] + scale * acc


def _matmul(a, b, l, res=None, scale=1.0, tm=1024, tn=512):
    m, k = a.shape
    n = b.shape[2]
    tm = min(m, tm)
    assert m % tm == 0 and n % tn == 0
    blk = _nbytes((tm, k), BF16) + _nbytes((k, tn), BF16) + _nbytes((tm, tn), F32)
    in_specs = [pl.BlockSpec((tm, k), lambda i, j: (i, 0)), pl.BlockSpec((None, k, tn), lambda i, j: (l, 0, j))]
    args = [a, b]
    if res is None:
        body = _mm_body
    else:
        body = functools.partial(_mm_res_body, scale=scale)
        in_specs.append(pl.BlockSpec((tm, tn), lambda i, j: (i, j)))
        args.append(res)
        blk += _nbytes((tm, tn), F32)
    return pl.pallas_call(
        body, grid=(m // tm, n // tn), in_specs=in_specs,
        out_specs=pl.BlockSpec((tm, tn), lambda i, j: (i, j)),
        out_shape=jax.ShapeDtypeStruct((m, n), F32),
        compiler_params=_params(("parallel", "arbitrary"), blk), name="matmul")(*args)


def _rope(x, cos, sin_signed):
    return x * cos + pltpu.roll(x, HEAD_DIM_A // 2, 1) * sin_signed


def _branch_multiplicity(delta):
    c = jnp.zeros(delta.shape, F32)
    for window, dilation in DILATED_BRANCHES:
        assert dilation & (dilation - 1) == 0
        hit = jnp.where(delta <= window, 1.0, 0.0)
        if dilation > 1:
            hit = jnp.where((delta & (dilation - 1)) == 0, hit, 0.0)
        c = c + hit
    return jnp.where(delta >= 0, c, 0.0)


_TRANS_B = (((1,), (1,)), ((), ()))


def _attn_prompt_body(q_ref, k_ref, v_ref, cos_ref, sin_ref, o_ref, kout_ref, qs, ks, vs, ctab, *, t_len, tq):
    nq = t_len // tq

    @pl.when((pl.program_id(0) == 0) & (pl.program_id(1) == 0))
    def _():
        rel = (lax.broadcasted_iota(jnp.int32, (tq, tq), 0) - lax.broadcasted_iota(jnp.int32, (tq, tq), 1))
        for d in range(nq):
            ctab[d] = _branch_multiplicity(rel + d * tq)

    cos = cos_ref[...]
    sin = sin_ref[...]
    k = _rope(k_ref[0], cos, sin)
    kout_ref[0] = k
    ks[...] = k.astype(BF16)
    qs[...] = (_rope(q_ref[0], cos, sin) * (HEAD_DIM_A ** -0.5)).astype(BF16)
    vs[...] = v_ref[0].astype(BF16)
    for i in range(nq):
        q = qs[i * tq:(i + 1) * tq, :]

        def body(j, carry, i=i, q=q):
            m, l, acc = carry
            off = pl.multiple_of(j * tq, tq)
            kj = ks[pl.ds(off, tq), :]
            vj = vs[pl.ds(off, tq), :]
            s = lax.dot_general(q, kj, _TRANS_B, preferred_element_type=F32)
            c = ctab[i - j]
            sm = jnp.where(c > 0.0, s, NEG)
            m_new = jnp.maximum(m, jnp.max(sm, axis=-1, keepdims=True))
            alpha = jnp.exp(m - m_new)
            p = jnp.exp(sm - m_new) * c
            l = alpha * l + jnp.sum(p, axis=-1, keepdims=True)
            acc = alpha * acc + jnp.dot(p.astype(BF16), vj, preferred_element_type=F32)
            return m_new, l, acc

        init = (jnp.full((tq, 1), NEG, F32), jnp.zeros((tq, 1), F32), jnp.zeros((tq, HEAD_DIM_A), F32))
        _, l, acc = lax.fori_loop(0, i + 1, body, init)
        o_ref[0, i * tq:(i + 1) * tq, :] = acc / l


def _attn_sample_body(q_ref, k_ref, v_ref, kp_ref, vp_ref, cos_ref, sin_ref, o_ref, kout_ref, *, t_len, n_prev):
    cos = cos_ref[...]
    sin = sin_ref[...]
    d1 = (n_prev + lax.broadcasted_iota(jnp.int32, (t_len, n_prev), 0)
          - lax.broadcasted_iota(jnp.int32, (t_len, n_prev), 1))
    c1 = _branch_multiplicity(d1)
    d2 = (lax.broadcasted_iota(jnp.int32, (t_len, LANES), 0) - lax.broadcasted_iota(jnp.int32, (t_len, LANES), 1))
    c2 = _branch_multiplicity(d2)
    pad = jnp.zeros((LANES - t_len, HEAD_DIM_A), F32)
    for h in range(N_HEADS_A):
        sl = slice(h * HEAD_DIM_A, (h + 1) * HEAD_DIM_A)
        k = _rope(k_ref[0, :, sl], cos, sin)
        kout_ref[0, :, sl] = k
        q = (_rope(q_ref[0, :, sl], cos, sin) * (HEAD_DIM_A ** -0.5)).astype(BF16)
        kn = jnp.concatenate([k, pad], axis=0).astype(BF16)
        vn = jnp.concatenate([v_ref[0, :, sl], pad], axis=0).astype(BF16)
        s1 = lax.dot_general(q, kp_ref[:, h, :].astype(BF16), _TRANS_B, preferred_element_type=F32)
        s2 = lax.dot_general(q, kn, _TRANS_B, preferred_element_type=F32)
        sm1 = jnp.where(c1 > 0.0, s1, NEG)
        sm2 = jnp.where(c2 > 0.0, s2, NEG)
        m = jnp.maximum(jnp.max(sm1, axis=-1, keepdims=True), jnp.max(sm2, axis=-1, keepdims=True))
        p1 = jnp.exp(sm1 - m) * c1
        p2 = jnp.exp(sm2 - m) * c2
        l = jnp.sum(p1, axis=-1, keepdims=True) + jnp.sum(p2, axis=-1, keepdims=True)
        acc = (jnp.dot(p1.astype(BF16), vp_ref[:, h, :].astype(BF16), preferred_element_type=F32)
               + jnp.dot(p2.astype(BF16), vn, preferred_element_type=F32))
        o_ref[0, :, sl] = acc / l


def _rope_tables(pos0, t_len):
    half = HEAD_DIM_A // 2
    inv = ROPE_THETA ** (-jnp.arange(half, dtype=F32) / half)
    ang = (pos0 + jnp.arange(t_len)).astype(F32)[:, None] * inv[None, :]
    cos = jnp.cos(ang)
    sin = jnp.sin(ang)
    return jnp.concatenate([cos, cos], axis=-1), jnp.concatenate([-sin, sin], axis=-1)


def _attention(z3, pos0, kv_prefix, l):
    b, t, _ = z3.shape
    cos, sin = _rope_tables(pos0, t)
    hd = HEAD_DIM_A
    out_shape = [jax.ShapeDtypeStruct((b, t, A_W), F32), jax.ShapeDtypeStruct((b, t, A_W), F32)]
    if kv_prefix is None:
        col = lambda base: (lambda bi, hi: (bi, 0, base // hd + hi))
        tab = pl.BlockSpec((t, hd), lambda bi, hi: (0, 0))
        zspecs = [pl.BlockSpec((1, t, hd), col(Z_QA)), pl.BlockSpec((1, t, hd), col(Z_KA)),
                  pl.BlockSpec((1, t, hd), col(Z_VA))]
        out_specs = [pl.BlockSpec((1, t, hd), col(0)), pl.BlockSpec((1, t, hd), col(0))]
        blk = 7 * _nbytes((t, hd), F32)
        tq = min(t, 256)
        body = functools.partial(_attn_prompt_body, t_len=t, tq=tq)
        scratch = 3 * _nbytes((t, hd), BF16) + _nbytes((t // tq, tq, tq), F32)
        return pl.pallas_call(
            body, grid=(b, N_HEADS_A), in_specs=zspecs + [tab, tab], out_specs=out_specs, out_shape=out_shape,
            scratch_shapes=[pltpu.VMEM((t, hd), BF16)] * 3 + [pltpu.VMEM((t // tq, tq, tq), F32)],
            compiler_params=_params(("arbitrary", "arbitrary"), blk, scratch),
            name="attn_prompt")(z3, z3, z3, cos, sin)
    k_prev, v_prev = kv_prefix
    n_prev = k_prev.shape[2]
    assert t <= LANES and n_prev % LANES == 0 and k_prev.shape[3:] == (N_HEADS_A, hd)
    col = lambda base: (lambda bi: (bi, 0, base // A_W))
    tab = pl.BlockSpec((t, hd), lambda bi: (0, 0))
    zspecs = [pl.BlockSpec((1, t, A_W), col(Z_QA)), pl.BlockSpec((1, t, A_W), col(Z_KA)),
              pl.BlockSpec((1, t, A_W), col(Z_VA))]
    out_specs = [pl.BlockSpec((1, t, A_W), col(0)), pl.BlockSpec((1, t, A_W), col(0))]
    pspec = pl.BlockSpec((None, None, n_prev, N_HEADS_A, hd), lambda bi: (l, bi, 0, 0, 0))
    blk = 5 * _nbytes((t, A_W), F32) + 2 * _nbytes((n_prev, A_W), F32)
    body = functools.partial(_attn_sample_body, t_len=t, n_prev=n_prev)
    return pl.pallas_call(
        body, grid=(b,), in_specs=zspecs + [pspec, pspec, tab, tab], out_specs=out_specs,
        out_shape=out_shape, compiler_params=_params(("parallel",), blk),
        name="attn_sample")(z3, z3, z3, k_prev, v_prev, cos, sin)


def _group_sum(x, ones_blockdiag):
    outs = []
    for j in range(x.shape[-1] // LANES):
        outs.append(jnp.dot(x[:, j * LANES:(j + 1) * LANES], ones_blockdiag, preferred_element_type=F32,
                            precision=lax.Precision.HIGHEST))
    return jnp.concatenate(outs, axis=-1)


def _softplus(y):
    return jnp.maximum(y, 0.0) + jnp.log1p(jnp.exp(-jnp.abs(y)))


def _rwkv_prep_body(x_ref, lo_ref, sx_ref, slo_ref, mux_ref, mulo_ref, w0_ref, a0_ref, kkp_ref, kap_ref, rk_ref,
                    wup_ref, aup_ref, gup_ref, ones_ref,
                    r_o, w_o, k_o, v_o, kk_o, b_o, g_o, bonus_o, last_x, last_lo, *, tt):
    @pl.when(pl.program_id(1) == 0)
    def _():
        last_x[0:1, :] = sx_ref[0]
        last_lo[0:1, :] = slo_ref[0]

    x = x_ref[0]
    lo = lo_ref[0]
    first = lax.broadcasted_iota(jnp.int32, (tt, 1), 0) == 0
    px = jnp.where(first, last_x[0:1, :], pltpu.roll(x, 1, 0))
    plo = jnp.where(first, last_lo[0:1, :], pltpu.roll(lo, 1, 0))
    last_x[0:1, :] = x[tt - 1:tt, :]
    last_lo[0:1, :] = lo[tt - 1:tt, :]
    fx = x + mux_ref[...] * (px - x)
    flo = lo + mulo_ref[...] * (plo - lo)
    r = fx[:, :B_W]
    k = fx[:, B_W:2 * B_W]
    v = fx[:, 2 * B_W:]
    zwa = flo[:, :LANES]
    zg = flo[:, LANES:]
    ones_bd = ones_ref[...]
    wl = w0_ref[...] + jnp.dot(jnp.tanh(zwa).astype(BF16), wup_ref[...], preferred_element_type=F32)
    w_log = -_softplus(-wl) - 0.5
    decay = jnp.exp(-jnp.exp(w_log))
    a = jax.nn.sigmoid(a0_ref[...] + jnp.dot(zwa.astype(BF16), aup_ref[...], preferred_element_type=F32))
    g = jnp.dot(jax.nn.sigmoid(zg).astype(BF16), gup_ref[...], preferred_element_type=F32)
    kk = k * kkp_ref[...]
    kk = kk / jnp.maximum(jnp.sqrt(_group_sum(kk * kk, ones_bd)), 1e-12)
    kmod = k * (1.0 + (a - 1.0) * kap_ref[...])
    r_o[0] = r
    w_o[0] = decay
    k_o[0] = kmod
    v_o[0] = v
    kk_o[0] = kk
    b_o[0] = kk * a
    g_o[0] = g
    bonus_o[0] = _group_sum(r * kmod * rk_ref[...], ones_bd) * v


def _rwkv_prep(z3, shift_prev, p):
    b, t, _ = z3.shape
    tt = min(t, 256)
    sx = shift_prev[:, None, :3 * B_W]
    slo = jnp.pad(shift_prev[:, None, 3 * B_W:], ((0, 0), (0, 0), (0, LORA_PAD - LORA_ALL)))
    row = lambda w: pl.BlockSpec((1, w), lambda bi, ti: (0, 0))
    full = lambda a: pl.BlockSpec(a.shape, lambda bi, ti: (0,) * a.ndim)
    in_specs = [pl.BlockSpec((1, tt, 3 * B_W), lambda bi, ti: (bi, ti, Z_RKV // (3 * B_W))),
                pl.BlockSpec((1, tt, LORA_PAD), lambda bi, ti: (bi, ti, Z_LORA // LORA_PAD)),
                pl.BlockSpec((1, 1, 3 * B_W), lambda bi, ti: (bi, 0, 0)),
                pl.BlockSpec((1, 1, LORA_PAD), lambda bi, ti: (bi, 0, 0)),
                row(3 * B_W), row(LORA_PAD), row(B_W), row(B_W), row(B_W), row(B_W), row(B_W),
                full(p['w_up']), full(p['a_up']), full(p['g_up']), full(p['ones_bd'])]
    ospec = pl.BlockSpec((1, tt, B_W), lambda bi, ti: (bi, ti, 0))
    blk = _nbytes((tt, 3 * B_W + LORA_PAD), F32) + 8 * _nbytes((tt, B_W), F32) + 2 * _nbytes((512, B_W), F32)
    return pl.pallas_call(
        functools.partial(_rwkv_prep_body, tt=tt), grid=(b, t // tt), in_specs=in_specs,
        out_specs=[ospec] * 8, out_shape=[jax.ShapeDtypeStruct((b, t, B_W), F32)] * 8,
        scratch_shapes=[pltpu.VMEM((SUBLANES, 3 * B_W), F32), pltpu.VMEM((SUBLANES, LORA_PAD), F32)],
        compiler_params=_params(("parallel", "arbitrary"), blk), name="rwkv_prep")(
            z3, z3, sx, slo, p['mu_x'], p['mu_lo'], p['w0'], p['a0'], p['k_k'], p['k_a'], p['r_k'],
            p['w_up'], p['a_up'], p['g_up'], p['ones_bd'])


N_PARTIAL = 4


def _rwkv_scan_body(r_ref, w_ref, k_ref, kk_ref, b_ref, v_ref, s0_ref, o_ref, sout_ref, state, *, tc, vp):
    @pl.when(pl.program_id(0) == 0)
    def _():
        state[...] = s0_ref[...]

    def step(t, carry):
        parts = [jnp.zeros((vp, LANES), F32)] * N_PARTIAL
        for k in range(HEAD_DIM_B):
            parts[k % N_PARTIAL] = parts[k % N_PARTIAL] + state[k] * kk_ref[t, pl.ds(k, 1), :]
        sa = -((parts[0] + parts[1]) + (parts[2] + parts[3]))
        vt = v_ref[t]
        parts = [jnp.zeros((vp, LANES), F32)] * N_PARTIAL
        for k in range(HEAD_DIM_B):
            s = (state[k] * w_ref[t, pl.ds(k, 1), :] + sa * b_ref[t, pl.ds(k, 1), :]
                 + vt * k_ref[t, pl.ds(k, 1), :])
            state[k] = s
            parts[k % N_PARTIAL] = parts[k % N_PARTIAL] + s * r_ref[t, pl.ds(k, 1), :]
        o_ref[t] = (parts[0] + parts[1]) + (parts[2] + parts[3])
        return carry

    lax.fori_loop(0, tc, step, 0)

    @pl.when(pl.program_id(0) == pl.num_programs(0) - 1)
    def _():
        sout_ref[...] = state[...]


def _rwkv_scan(r, w, k, kk, bv, v, s0):
    t, vp, _ = v.shape
    tc = min(t, 64)
    kspec = pl.BlockSpec((tc, HEAD_DIM_B, LANES), lambda i: (i, 0, 0))
    vspec = pl.BlockSpec((tc, vp, LANES), lambda i: (i, 0, 0))
    sspec = pl.BlockSpec((HEAD_DIM_B, vp, LANES), lambda i: (0, 0, 0))
    blk = 5 * _nbytes((tc, HEAD_DIM_B, LANES), F32) + 2 * _nbytes((tc, vp, LANES), F32) \
        + 2 * _nbytes((HEAD_DIM_B, vp, LANES), F32)
    return pl.pallas_call(
        functools.partial(_rwkv_scan_body, tc=tc, vp=vp), grid=(t // tc,),
        in_specs=[kspec] * 5 + [vspec, sspec], out_specs=[vspec, sspec],
        out_shape=[jax.ShapeDtypeStruct((t, vp, LANES), F32), jax.ShapeDtypeStruct((HEAD_DIM_B, vp, LANES), F32)],
        scratch_shapes=[pltpu.VMEM((HEAD_DIM_B, vp, LANES), F32)],
        compiler_params=_params(("arbitrary",), blk, _nbytes((HEAD_DIM_B, vp, LANES), F32)),
        name="rwkv_scan")(r, w, k, kk, bv, v, s0)


def _rwkv_mix(z3, shift_prev, wkv0, p):
    b, t, _ = z3.shape
    nh, hd = N_HEADS_B, HEAD_DIM_B
    r, w, kmod, v, kk, bv, g, bonus = _rwkv_prep(z3, shift_prev, p)
    dup = LANES // (b * nh)
    assert dup * b * nh == LANES and hd % dup == 0
    vp = hd // dup

    def key_layout(x):
        y = x.reshape(b, t, nh, hd).transpose(1, 3, 0, 2)[:, :, None]
        return jnp.broadcast_to(y, (t, hd, dup, b, nh)).reshape(t, hd, LANES)

    def val_layout(x):
        return x.reshape(b, t, nh, dup, vp).transpose(1, 4, 3, 0, 2).reshape(t, vp, LANES)

    s0 = wkv0.reshape(b, nh, dup, vp, hd).transpose(4, 3, 2, 0, 1).reshape(hd, vp, LANES)
    o, s = _rwkv_scan(key_layout(r), key_layout(w), key_layout(kmod), key_layout(kk), key_layout(bv),
                      val_layout(v), s0)
    o = o.reshape(t, vp, dup, b, nh).transpose(3, 0, 4, 2, 1).reshape(b, t, B_W)
    s = s.reshape(hd, vp, dup, b, nh).transpose(3, 4, 2, 1, 0).reshape(b, nh, hd, hd)
    return o, bonus, g, s


def _gmlp_body(u_ref, v_ref, ws_ref, b_ref, o_ref, *, tc):
    keep = (lax.broadcasted_iota(jnp.int32, (CHUNK, CHUNK), 1) <= lax.broadcasted_iota(jnp.int32, (CHUNK, CHUNK), 0))
    for g in range(N_GROUPS_C):
        sl = slice(g * CHUNK, (g + 1) * CHUNK)
        w = jnp.where(keep, ws_ref[g], 0.0).astype(BF16)
        v = v_ref[0, :, sl]
        if tc < CHUNK:
            v = jnp.concatenate([v, jnp.zeros((CHUNK - tc, CHUNK), F32)], axis=0)
        s = jnp.dot(w, v.astype(BF16), preferred_element_type=F32) + b_ref[g]
        o_ref[0, :, sl] = u_ref[0, :, sl] * s[:tc]


def _gmlp(z3, w_s, b_s):
    b, t, _ = z3.shape
    tc = min(t, CHUNK)
    assert t % tc == 0
    blk = 3 * _nbytes((tc, C_W), F32) + 2 * _nbytes((N_GROUPS_C, CHUNK, CHUNK), F32)
    return pl.pallas_call(
        functools.partial(_gmlp_body, tc=tc), grid=(b, t // tc),
        in_specs=[pl.BlockSpec((1, tc, C_W), lambda bi, ci: (bi, ci, Z_CU // C_W)),
                  pl.BlockSpec((1, tc, C_W), lambda bi, ci: (bi, ci, Z_CV // C_W)),
                  pl.BlockSpec((N_GROUPS_C, CHUNK, CHUNK), lambda bi, ci: (0, 0, 0)),
                  pl.BlockSpec((N_GROUPS_C, CHUNK, 1), lambda bi, ci: (0, 0, 0))],
        out_specs=pl.BlockSpec((1, tc, C_W), lambda bi, ci: (bi, ci, 0)),
        out_shape=jax.ShapeDtypeStruct((b, t, C_W), F32),
        compiler_params=_params(("parallel", "parallel"), blk), name="gmlp")(z3, z3, w_s, b_s[:, :, None])


POOL_HALO = 16


def _pool_body(p_ref, pre_ref, w_ref, sc_ref, o_ref, ext, *, t_len, pos0, tc):
    ext[0:POOL_HALO, :] = pre_ref[0]
    ext[POOL_HALO:POOL_HALO + t_len, :] = p_ref[0]
    for c0 in range(0, t_len, tc):
        pos = pos0 + c0 + lax.broadcasted_iota(jnp.int32, (tc, 1), 0)
        for g, win in enumerate(POOL_WINDOWS):
            sl = slice(g * POOL_GROUP, (g + 1) * POOL_GROUP)
            base = POOL_HALO + c0
            acc = ext[base:base + tc, sl]
            for i in range(1, win):
                acc = acc + ext[base - i:base - i + tc, sl]
            cnt = jnp.minimum(win, pos + 1).astype(F32)
            pooled = acc / cnt - p_ref[0, c0:c0 + tc, sl]
            y = jnp.dot(pooled.astype(BF16), w_ref[g], preferred_element_type=F32)
            o_ref[0, c0:c0 + tc, sl] = y * sc_ref[:, sl]


def _pool(z3, prefix, pos0, w_pool, scale):
    b, t, _ = z3.shape
    assert POOL_PREV < POOL_HALO
    pre = jnp.pad(prefix, ((0, 0), (POOL_HALO - POOL_PREV, 0), (0, 0)))
    tc = min(t, 256)
    blk = 2 * _nbytes((t, D_W), F32) + _nbytes((POOL_HALO, D_W), F32) + _nbytes(w_pool.shape, BF16)
    scr = _nbytes((t + POOL_HALO, D_W), F32)
    return pl.pallas_call(
        functools.partial(_pool_body, t_len=t, pos0=pos0, tc=tc), grid=(b,),
        in_specs=[pl.BlockSpec((1, t, D_W), lambda bi: (bi, 0, Z_D // D_W)),
                  pl.BlockSpec((1, POOL_HALO, D_W), lambda bi: (bi, 0, 0)),
                  pl.BlockSpec(w_pool.shape, lambda bi: (0, 0, 0)),
                  pl.BlockSpec((1, D_W), lambda bi: (0, 0))],
        out_specs=pl.BlockSpec((1, t, D_W), lambda bi: (bi, 0, 0)),
        out_shape=jax.ShapeDtypeStruct((b, t, D_W), F32),
        scratch_shapes=[pltpu.VMEM((t + POOL_HALO, D_W), F32)],
        compiler_params=_params(("parallel",), blk, scr), name="pool")(z3, pre, w_pool, scale.reshape(1, D_W))


def _mix_body(oa_ref, ob_ref, bonus_ref, g_ref, oc_ref, od_ref, ga_ref, lnw_ref, lnb_ref, gc_ref, gd_ref, ones_ref,
              o_ref):
    def rms(x, gain):
        ms = jnp.mean(x * x, axis=-1, keepdims=True)
        return x * lax.rsqrt(ms + RMS_EPS) * gain

    ones_bd = ones_ref[...]
    o = ob_ref[...]
    mean = _group_sum(o, ones_bd) * (1.0 / HEAD_DIM_B)
    d = o - mean
    var = _group_sum(d * d, ones_bd) * (1.0 / HEAD_DIM_B)
    ob = d * lax.rsqrt(var + GN_EPS) * lnw_ref[...] + lnb_ref[...]
    ob = (ob + bonus_ref[...]) * g_ref[...]
    o_ref[:, 0:A_W] = rms(oa_ref[...], ga_ref[...]).astype(BF16)
    o_ref[:, A_W:A_W + B_W] = ob.astype(BF16)
    o_ref[:, A_W + B_W:A_W + B_W + C_W] = rms(oc_ref[...], gc_ref[...]).astype(BF16)
    o_ref[:, A_W + B_W + C_W:] = rms(od_ref[...], gd_ref[...]).astype(BF16)


def _mix(oa, ob, bonus, g, oc, od, ga, lnw, lnb, gc, gd, ones_bd):
    m = oa.shape[0]
    tm = min(m, 256)
    act = pl.BlockSpec((tm, A_W), lambda i: (i, 0))
    row = pl.BlockSpec((1, A_W), lambda i: (0, 0))
    blk = 6 * _nbytes((tm, A_W), F32) + _nbytes((tm, D_MODEL), BF16) + _nbytes((512, A_W), F32)
    r1 = lambda a: a.reshape(1, -1)
    return pl.pallas_call(
        _mix_body, grid=(m // tm,),
        in_specs=[act] * 6 + [row] * 5 + [pl.BlockSpec((LANES, LANES), lambda i: (0, 0))],
        out_specs=pl.BlockSpec((tm, D_MODEL), lambda i: (i, 0)),
        out_shape=jax.ShapeDtypeStruct((m, D_MODEL), BF16),
        compiler_params=_params(("parallel",), blk), name="mix")(
            oa, ob, bonus, g, oc, od, r1(ga), r1(lnw), r1(lnb), r1(gc), r1(gd), ones_bd)


def _prep_matmul_weights(w):
    w_in = w['w_in']
    zpad = jnp.zeros((DEPTH, D_MODEL, Z_W - Z_LORA - LORA_ALL), w_in.dtype)
    w_in_z = jnp.concatenate(
        [w_in[..., :OFF_B + 3 * B_W], w_in[..., OFF_C:], w_in[..., OFF_B + 3 * B_W:OFF_C], zpad], axis=-1)
    out = {n: w[n].astype(BF16) for n in ('w1_gate', 'w1_up', 'w1_down', 'w2_gate', 'w2_up', 'w2_down', 'w_out')}
    out['w_in'] = w_in_z.astype(BF16)
    return out


def _prep_layer_weights(l, w):
    mu = w['mu_b'][l]
    pad_rows = lambda a, before, total: jnp.pad(a, ((before, total - before - a.shape[0]), (0, 0))).astype(BF16)
    ones_bd = jnp.kron(jnp.eye(LANES // HEAD_DIM_B, dtype=F32), jnp.ones((HEAD_DIM_B, HEAD_DIM_B), F32))
    rw = dict(
        mu_x=mu[None, :3 * B_W], mu_lo=jnp.pad(mu[None, 3 * B_W:], ((0, 0), (0, LORA_PAD - LORA_ALL))),
        w0=w['w0'][l][None], a0=w['a0'][l][None], k_k=w['k_k'][l][None], k_a=w['k_a'][l][None],
        r_k=w['r_k'][l].reshape(1, B_W),
        w_up=pad_rows(w['w_up'][l], 0, LANES), a_up=pad_rows(w['a_up'][l], LORA_W, LANES),
        g_up=pad_rows(w['g_up'][l], 0, LORA_PAD - LANES), ones_bd=ones_bd)
    return dict(w_pool=w['w_pool'][l].astype(BF16), rwkv=rw)


def _layer(x, l, w, wm, wl, pos0, kv_prefix, shift_prev, wkv0, pool_prefix):
    b, t, _ = x.shape
    m = b * t
    x2 = x.reshape(m, D_MODEL)
    hid = _ffn_up(_rmsnorm(x2, w['ln_ffn1'][l], BF16), wm['w1_gate'], wm['w1_up'], l)
    h = _matmul(hid, wm['w1_down'], l, res=x2, scale=0.5, tm=512, tn=512)
    z = _matmul(_rmsnorm(h, w['ln_mix'][l], BF16), wm['w_in'], l, tm=1024, tn=512)
    z3 = z.reshape(b, t, Z_W)
    out_a, ka = _attention(z3, pos0, kv_prefix, l)
    o_b, bonus, gate, wkv_new = _rwkv_mix(z3, shift_prev, wkv0, wl['rwkv'])
    out_c = _gmlp(z3, w['w_s'][l], w['b_s'][l])
    out_d = _pool(z3, pool_prefix, pos0, wl['w_pool'], w['pool_scale'][l])
    r2 = lambda a: a.reshape(m, -1)
    mix = _mix(r2(out_a), r2(o_b), r2(bonus), r2(gate), r2(out_c), r2(out_d), w['g_out_a'][l], w['ln_x_w'][l],
               w['ln_x_b'][l], w['g_out_c'][l], w['g_out_d'][l], wl['rwkv']['ones_bd'])
    h = _matmul(mix, wm['w_out'], l, res=h, scale=1.0, tm=1024, tn=512)
    hid = _ffn_up(_rmsnorm(h, w['ln_ffn2'][l], BF16), wm['w2_gate'], wm['w2_up'], l)
    y = _matmul(hid, wm['w2_down'], l, res=h, scale=0.5, tm=512, tn=512)
    hs = (b, t, N_HEADS_A, HEAD_DIM_A)
    va = z3[..., Z_VA:Z_VA + A_W]
    shift_new = jnp.concatenate([z3[:, -1, Z_RKV:Z_RKV + 3 * B_W], z3[:, -1, Z_LORA:Z_LORA + LORA_ALL]], axis=-1)
    p = z3[..., Z_D:Z_D + D_W]
    pool_new = jnp.concatenate([pool_prefix, p], axis=1)[:, -POOL_PREV:]
    vc = z3[..., Z_CV:Z_CV + C_W]
    return y.reshape(b, t, D_MODEL), (ka.reshape(hs), va.reshape(hs), wkv_new, shift_new, pool_new, vc)


def kernel(x_prompt, x_sample, cache_k_swa, cache_v_swa, state_rwkv_wkv, state_rwkv_shift, state_pool, ln_ffn1, w1_gate, w1_up, w1_down, ln_mix, w_in, g_out_a, mu_b, w0, w_up, a0, a_up, g_up, k_k, k_a, r_k, ln_x_w, ln_x_b, w_s, b_s, g_out_c, w_pool, pool_scale, g_out_d, w_out, ln_ffn2, w2_gate, w2_up, w2_down, ln_final):
    w = dict(ln_ffn1=ln_ffn1, w1_gate=w1_gate, w1_up=w1_up, w1_down=w1_down, ln_mix=ln_mix, w_in=w_in,
             g_out_a=g_out_a, mu_b=mu_b, w0=w0, w_up=w_up, a0=a0, a_up=a_up, g_up=g_up, k_k=k_k, k_a=k_a,
             r_k=r_k, ln_x_w=ln_x_w, ln_x_b=ln_x_b, w_s=w_s, b_s=b_s, g_out_c=g_out_c, w_pool=w_pool,
             pool_scale=pool_scale, g_out_d=g_out_d, w_out=w_out, ln_ffn2=ln_ffn2, w2_gate=w2_gate,
             w2_up=w2_up, w2_down=w2_down)
    nbp, t_prompt, _ = x_prompt.shape
    yp, ys = x_prompt, x_sample
    p_states, s_states = [], []
    wm = _prep_matmul_weights(w)
    for l in range(DEPTH):
        wl = _prep_layer_weights(l, w)
        yp, sp = _layer(yp, l, w, wm, wl, 0, None,
                        jnp.zeros((nbp, B_FEAT), F32),
                        jnp.zeros((nbp, N_HEADS_B, HEAD_DIM_B, HEAD_DIM_B), F32),
                        jnp.zeros((nbp, POOL_PREV, D_W), F32))
        ys, ss = _layer(ys, l, w, wm, wl, PAST_LEN, (cache_k_swa, cache_v_swa),
                        state_rwkv_shift[l], state_rwkv_wkv[l], state_pool[l])
        p_states.append(sp)
        s_states.append(ss)
    keep = min(WIN_MAX, t_prompt)
    stack = lambda states, i: jnp.stack([s[i] for s in states])
    y_prompt = _rmsnorm(yp.reshape(-1, D_MODEL), ln_final, F32).reshape(yp.shape)
    y_sample = _rmsnorm(ys.reshape(-1, D_MODEL), ln_final, F32).reshape(ys.shape)
    return (y_prompt, y_sample,
            jnp.stack([s[0][:, -keep:] for s in p_states]), jnp.stack([s[1][:, -keep:] for s in p_states]),
            stack(p_states, 2), stack(p_states, 3), stack(p_states, 4),
            stack(s_states, 0), stack(s_states, 1), stack(s_states, 2), stack(s_states, 3), stack(s_states, 4),
            stack(s_states, 5))
```

```python
import functools

import jax
import jax.numpy as jnp
from jax import lax
from jax.experimental import pallas as pl
from jax.experimental.pallas import tpu as pltpu

F32 = jnp.float32
BF16 = jnp.bfloat16

D_MODEL = 4096
DEPTH = 2
PAST_LEN = 16384
A_W = B_W = C_W = D_W = D_MODEL // 4
RMS_EPS = 1e-6
HEAD_DIM_A = 128
N_HEADS_A = A_W // HEAD_DIM_A
DILATED_BRANCHES = ((128, 1), (512, 4), (2048, 16))
WIN_MAX = 2048
ROPE_THETA = 10000.0
HEAD_DIM_B = 64
N_HEADS_B = B_W // HEAD_DIM_B
LORA_W, LORA_A, LORA_G = 64, 64, 160
LORA_ALL = LORA_W + LORA_A + LORA_G
GN_EPS = 64e-5
B_FEAT = 3 * B_W + LORA_ALL
CHUNK = 128
N_GROUPS_C = 8
POOL_WINDOWS = (2, 4, 8, 16)
POOL_PREV = max(POOL_WINDOWS) - 1
POOL_GROUP = D_W // len(POOL_WINDOWS)
OFF_B = 3 * A_W
OFF_C = OFF_B + B_FEAT
OFF_D = OFF_C + 2 * C_W

LANES = 128
SUBLANES = 8
VMEM_BYTES_V7X = 64 * 2**20
VMEM_INTERNAL_RESERVE = 12 * 2**20

Z_QA, Z_KA, Z_VA = 0, A_W, 2 * A_W
Z_RKV = 3 * A_W
Z_CU = Z_RKV + 3 * B_W
Z_CV = Z_CU + C_W
Z_D = Z_CV + C_W
Z_LORA = Z_D + D_W
LORA_PAD = 3 * LANES
MXU_COLS_V7X = 256
Z_W = -(-(Z_LORA + LORA_PAD) // (2 * MXU_COLS_V7X)) * (2 * MXU_COLS_V7X)

NEG = -1e30


def _vmem_limit(block_bytes, scratch_bytes=0):
    need = 2 * block_bytes + scratch_bytes + VMEM_INTERNAL_RESERVE
    return int(min(max(need, 16 * 2**20), VMEM_BYTES_V7X - 4 * 2**20))


def _params(sem, block_bytes, scratch_bytes=0):
    return pltpu.CompilerParams(dimension_semantics=sem,
                                vmem_limit_bytes=_vmem_limit(block_bytes, scratch_bytes))


def _nbytes(shape, dtype):
    n = 1
    for s in shape:
        n *= s
    return n * jnp.dtype(dtype).itemsize


def _rmsnorm_body(x_ref, g_ref, o_ref):
    x = x_ref[...]
    ms = jnp.mean(x * x, axis=-1, keepdims=True)
    o_ref[...] = (x * lax.rsqrt(ms + RMS_EPS) * g_ref[...]).astype(o_ref.dtype)


def _rmsnorm(x, g, out_dtype):
    m, d = x.shape
    tm = min(m, 256)
    blk = _nbytes((tm, d), F32) + _nbytes((tm, d), out_dtype)
    return pl.pallas_call(
        _rmsnorm_body, grid=(m // tm,),
        in_specs=[pl.BlockSpec((tm, d), lambda i: (i, 0)), pl.BlockSpec((1, d), lambda i: (0, 0))],
        out_specs=pl.BlockSpec((tm, d), lambda i: (i, 0)),
        out_shape=jax.ShapeDtypeStruct((m, d), out_dtype),
        compiler_params=_params(("parallel",), blk), name="rmsnorm")(x, g.reshape(1, d))


def _ffn_up_body(x_ref, wg_ref, wu_ref, o_ref):
    x = x_ref[...]
    g = jnp.dot(x, wg_ref[...], preferred_element_type=F32)
    u = jnp.dot(x, wu_ref[...], preferred_element_type=F32)
    o_ref[...] = (g * jax.nn.sigmoid(g) * u).astype(o_ref.dtype)


def _ffn_up(x, wg, wu, l):
    m, k = x.shape
    n = wg.shape[2]
    tm = min(m, 1024)
    tn = MXU_COLS_V7X
    blk = _nbytes((tm, k), BF16) + 2 * _nbytes((k, tn), BF16) + _nbytes((tm, tn), BF16)
    return pl.pallas_call(
        _ffn_up_body, grid=(m // tm, n // tn),
        in_specs=[pl.BlockSpec((tm, k), lambda i, j: (i, 0)),
                  pl.BlockSpec((None, k, tn), lambda i, j: (l, 0, j)),
                  pl.BlockSpec((None, k, tn), lambda i, j: (l, 0, j))],
        out_specs=pl.BlockSpec((tm, tn), lambda i, j: (i, j)),
        out_shape=jax.ShapeDtypeStruct((m, n), BF16),
        compiler_params=_params(("parallel", "arbitrary"), blk), name="ffn_up")(x, wg, wu)


def _mm_body(a_ref, b_ref, o_ref):
    o_ref[...] = jnp.dot(a_ref[...], b_ref[...], preferred_element_type=F32)


def _mm_res_body(a_ref, b_ref, r_ref, o_ref, *, scale):
    acc = jnp.dot(a_ref[...], b_ref[...], preferred_element_type=F32)
    o_ref[...] = r_ref[...] + scale * acc


def _matmul(a, b, l, res=None, scale=1.0, tm=1024, tn=512):
    m, k = a.shape
    n = b.shape[2]
    tm = min(m, tm)
    assert m % tm == 0 and n % tn == 0
    blk = _nbytes((tm, k), BF16) + _nbytes((k, tn), BF16) + _nbytes((tm, tn), F32)
    in_specs = [pl.BlockSpec((tm, k), lambda i, j: (i, 0)), pl.BlockSpec((None, k, tn), lambda i, j: (l, 0, j))]
    args = [a, b]
    if res is None:
        body = _mm_body
    else:
        body = functools.partial(_mm_res_body, scale=scale)
        in_specs.append(pl.BlockSpec((tm, tn), lambda i, j: (i, j)))
        args.append(res)
        blk += _nbytes((tm, tn), F32)
    return pl.pallas_call(
        body, grid=(m // tm, n // tn), in_specs=in_specs,
        out_specs=pl.BlockSpec((tm, tn), lambda i, j: (i, j)),
        out_shape=jax.ShapeDtypeStruct((m, n), F32),
        compiler_params=_params(("parallel", "arbitrary"), blk), name="matmul")(*args)


def _rope(x, cos, sin_signed):
    return x * cos + pltpu.roll(x, HEAD_DIM_A // 2, 1) * sin_signed


def _branch_multiplicity(delta):
    c = jnp.zeros(delta.shape, F32)
    for window, dilation in DILATED_BRANCHES:
        assert dilation & (dilation - 1) == 0
        hit = jnp.where(delta <= window, 1.0, 0.0)
        if dilation > 1:
            hit = jnp.where((delta & (dilation - 1)) == 0, hit, 0.0)
        c = c + hit
    return jnp.where(delta >= 0, c, 0.0)


_TRANS_B = (((1,), (1,)), ((), ()))


def _attn_prompt_body(q_ref, k_ref, v_ref, cos_ref, sin_ref, o_ref, kout_ref, qs, ks, vs, ctab, *, t_len, tq):
    nq = t_len // tq

    @pl.when((pl.program_id(0) == 0) & (pl.program_id(1) == 0))
    def _():
        rel = (lax.broadcasted_iota(jnp.int32, (tq, tq), 0) - lax.broadcasted_iota(jnp.int32, (tq, tq), 1))
        for d in range(nq):
            ctab[:, d * tq:(d + 1) * tq] = _branch_multiplicity(rel + (nq - 1 - d) * tq)

    cos = cos_ref[...]
    sin = sin_ref[...]
    k = _rope(k_ref[0], cos, sin)
    kout_ref[0] = k
    ks[...] = k.astype(BF16)
    qs[...] = (_rope(q_ref[0], cos, sin) * (HEAD_DIM_A ** -0.5)).astype(BF16)
    vs[...] = v_ref[0].astype(BF16)
    for i in range(nq):
        kw = (i + 1) * tq
        s = lax.dot_general(qs[i * tq:(i + 1) * tq, :], ks[0:kw, :], _TRANS_B, preferred_element_type=F32)
        c = ctab[:, (nq - 1 - i) * tq:]
        sm = jnp.where(c > 0.0, s, NEG)
        p = jnp.exp(sm - jnp.max(sm, axis=-1, keepdims=True)) * c
        l = jnp.sum(p, axis=-1, keepdims=True)
        acc = jnp.dot(p.astype(BF16), vs[0:kw, :], preferred_element_type=F32)
        o_ref[0, i * tq:(i + 1) * tq, :] = acc / l


def _attn_sample_body(q_ref, k_ref, v_ref, kp_ref, vp_ref, cos_ref, sin_ref, o_ref, kout_ref, *, t_len, n_prev):
    cos = cos_ref[...]
    sin = sin_ref[...]
    d1 = (n_prev + lax.broadcasted_iota(jnp.int32, (t_len, n_prev), 0)
          - lax.broadcasted_iota(jnp.int32, (t_len, n_prev), 1))
    c1 = _branch_multiplicity(d1)
    d2 = (lax.broadcasted_iota(jnp.int32, (t_len, LANES), 0) - lax.broadcasted_iota(jnp.int32, (t_len, LANES), 1))
    c2 = _branch_multiplicity(d2)
    pad = jnp.zeros((LANES - t_len, HEAD_DIM_A), F32)
    for h in range(N_HEADS_A):
        sl = slice(h * HEAD_DIM_A, (h + 1) * HEAD_DIM_A)
        k = _rope(k_ref[0, :, sl], cos, sin)
        kout_ref[0, :, sl] = k
        q = (_rope(q_ref[0, :, sl], cos, sin) * (HEAD_DIM_A ** -0.5)).astype(BF16)
        kn = jnp.concatenate([k, pad], axis=0).astype(BF16)
        vn = jnp.concatenate([v_ref[0, :, sl], pad], axis=0).astype(BF16)
        s1 = lax.dot_general(q, kp_ref[:, h, :].astype(BF16), _TRANS_B, preferred_element_type=F32)
        s2 = lax.dot_general(q, kn, _TRANS_B, preferred_element_type=F32)
        sm1 = jnp.where(c1 > 0.0, s1, NEG)
        sm2 = jnp.where(c2 > 0.0, s2, NEG)
        m = jnp.maximum(jnp.max(sm1, axis=-1, keepdims=True), jnp.max(sm2, axis=-1, keepdims=True))
        p1 = jnp.exp(sm1 - m) * c1
        p2 = jnp.exp(sm2 - m) * c2
        l = jnp.sum(p1, axis=-1, keepdims=True) + jnp.sum(p2, axis=-1, keepdims=True)
        acc = (jnp.dot(p1.astype(BF16), vp_ref[:, h, :].astype(BF16), preferred_element_type=F32)
               + jnp.dot(p2.astype(BF16), vn, preferred_element_type=F32))
        o_ref[0, :, sl] = acc / l


def _rope_tables(pos0, t_len):
    half = HEAD_DIM_A // 2
    inv = ROPE_THETA ** (-jnp.arange(half, dtype=F32) / half)
    ang = (pos0 + jnp.arange(t_len)).astype(F32)[:, None] * inv[None, :]
    cos = jnp.cos(ang)
    sin = jnp.sin(ang)
    return jnp.concatenate([cos, cos], axis=-1), jnp.concatenate([-sin, sin], axis=-1)


def _attention(z3, pos0, kv_prefix, l):
    b, t, _ = z3.shape
    cos, sin = _rope_tables(pos0, t)
    hd = HEAD_DIM_A
    out_shape = [jax.ShapeDtypeStruct((b, t, A_W), F32), jax.ShapeDtypeStruct((b, t, A_W), F32)]
    if kv_prefix is None:
        col = lambda base: (lambda bi, hi: (bi, 0, base // hd + hi))
        tab = pl.BlockSpec((t, hd), lambda bi, hi: (0, 0))
        zspecs = [pl.BlockSpec((1, t, hd), col(Z_QA)), pl.BlockSpec((1, t, hd), col(Z_KA)),
                  pl.BlockSpec((1, t, hd), col(Z_VA))]
        out_specs = [pl.BlockSpec((1, t, hd), col(0)), pl.BlockSpec((1, t, hd), col(0))]
        blk = 7 * _nbytes((t, hd), F32)
        tq = min(t, 256)
        body = functools.partial(_attn_prompt_body, t_len=t, tq=tq)
        scratch = 3 * _nbytes((t, hd), BF16) + 5 * _nbytes((tq, t), F32)
        return pl.pallas_call(
            body, grid=(b, N_HEADS_A), in_specs=zspecs + [tab, tab], out_specs=out_specs, out_shape=out_shape,
            scratch_shapes=[pltpu.VMEM((t, hd), BF16)] * 3 + [pltpu.VMEM((tq, t), F32)],
            compiler_params=_params(("arbitrary", "arbitrary"), blk, scratch),
            name="attn_prompt")(z3, z3, z3, cos, sin)
    k_prev, v_prev = kv_prefix
    n_prev = k_prev.shape[2]
    assert t <= LANES and n_prev % LANES == 0 and k_prev.shape[3:] == (N_HEADS_A, hd)
    col = lambda base: (lambda bi: (bi, 0, base // A_W))
    tab = pl.BlockSpec((t, hd), lambda bi: (0, 0))
    zspecs = [pl.BlockSpec((1, t, A_W), col(Z_QA)), pl.BlockSpec((1, t, A_W), col(Z_KA)),
              pl.BlockSpec((1, t, A_W), col(Z_VA))]
    out_specs = [pl.BlockSpec((1, t, A_W), col(0)), pl.BlockSpec((1, t, A_W), col(0))]
    pspec = pl.BlockSpec((None, None, n_prev, N_HEADS_A, hd), lambda bi: (l, bi, 0, 0, 0))
    blk = 5 * _nbytes((t, A_W), F32) + 2 * _nbytes((n_prev, A_W), F32)
    body = functools.partial(_attn_sample_body, t_len=t, n_prev=n_prev)
    return pl.pallas_call(
        body, grid=(b,), in_specs=zspecs + [pspec, pspec, tab, tab], out_specs=out_specs,
        out_shape=out_shape, compiler_params=_params(("parallel",), blk),
        name="attn_sample")(z3, z3, z3, k_prev, v_prev, cos, sin)


def _group_sum(x, ones_blockdiag):
    outs = []
    for j in range(x.shape[-1] // LANES):
        outs.append(jnp.dot(x[:, j * LANES:(j + 1) * LANES], ones_blockdiag, preferred_element_type=F32,
                            precision=lax.Precision.HIGHEST))
    return jnp.concatenate(outs, axis=-1)


def _softplus(y):
    return jnp.maximum(y, 0.0) + jnp.log1p(jnp.exp(-jnp.abs(y)))


def _rwkv_prep_body(x_ref, lo_ref, sx_ref, slo_ref, mux_ref, mulo_ref, w0_ref, a0_ref, kkp_ref, kap_ref, rk_ref,
                    wup_ref, aup_ref, gup_ref, ones_ref,
                    r_o, w_o, k_o, v_o, kk_o, b_o, g_o, bonus_o, last_x, last_lo, *, tt):
    @pl.when(pl.program_id(1) == 0)
    def _():
        last_x[0:1, :] = sx_ref[0]
        last_lo[0:1, :] = slo_ref[0]

    x = x_ref[0]
    lo = lo_ref[0]
    first = lax.broadcasted_iota(jnp.int32, (tt, 1), 0) == 0
    px = jnp.where(first, last_x[0:1, :], pltpu.roll(x, 1, 0))
    plo = jnp.where(first, last_lo[0:1, :], pltpu.roll(lo, 1, 0))
    last_x[0:1, :] = x[tt - 1:tt, :]
    last_lo[0:1, :] = lo[tt - 1:tt, :]
    fx = x + mux_ref[...] * (px - x)
    flo = lo + mulo_ref[...] * (plo - lo)
    r = fx[:, :B_W]
    k = fx[:, B_W:2 * B_W]
    v = fx[:, 2 * B_W:]
    zwa = flo[:, :LANES]
    zg = flo[:, LANES:]
    ones_bd = ones_ref[...]
    wl = w0_ref[...] + jnp.dot(jnp.tanh(zwa).astype(BF16), wup_ref[...], preferred_element_type=F32)
    w_log = -_softplus(-wl) - 0.5
    decay = jnp.exp(-jnp.exp(w_log))
    a = jax.nn.sigmoid(a0_ref[...] + jnp.dot(zwa.astype(BF16), aup_ref[...], preferred_element_type=F32))
    g = jnp.dot(jax.nn.sigmoid(zg).astype(BF16), gup_ref[...], preferred_element_type=F32)
    kk = k * kkp_ref[...]
    kk = kk / jnp.maximum(jnp.sqrt(_group_sum(kk * kk, ones_bd)), 1e-12)
    kmod = k * (1.0 + (a - 1.0) * kap_ref[...])
    r_o[0] = r
    w_o[0] = decay
    k_o[0] = kmod
    v_o[0] = v
    kk_o[0] = kk
    b_o[0] = kk * a
    g_o[0] = g
    bonus_o[0] = _group_sum(r * kmod * rk_ref[...], ones_bd) * v


def _rwkv_prep(z3, shift_prev, p):
    b, t, _ = z3.shape
    tt = min(t, 256)
    sx = shift_prev[:, None, :3 * B_W]
    slo = jnp.pad(shift_prev[:, None, 3 * B_W:], ((0, 0), (0, 0), (0, LORA_PAD - LORA_ALL)))
    row = lambda w: pl.BlockSpec((1, w), lambda bi, ti: (0, 0))
    full = lambda a: pl.BlockSpec(a.shape, lambda bi, ti: (0,) * a.ndim)
    in_specs = [pl.BlockSpec((1, tt, 3 * B_W), lambda bi, ti: (bi, ti, Z_RKV // (3 * B_W))),
                pl.BlockSpec((1, tt, LORA_PAD), lambda bi, ti: (bi, ti, Z_LORA // LORA_PAD)),
                pl.BlockSpec((1, 1, 3 * B_W), lambda bi, ti: (bi, 0, 0)),
                pl.BlockSpec((1, 1, LORA_PAD), lambda bi, ti: (bi, 0, 0)),
                row(3 * B_W), row(LORA_PAD), row(B_W), row(B_W), row(B_W), row(B_W), row(B_W),
                full(p['w_up']), full(p['a_up']), full(p['g_up']), full(p['ones_bd'])]
    ospec = pl.BlockSpec((1, tt, B_W), lambda bi, ti: (bi, ti, 0))
    blk = _nbytes((tt, 3 * B_W + LORA_PAD), F32) + 8 * _nbytes((tt, B_W), F32) + 2 * _nbytes((512, B_W), F32)
    return pl.pallas_call(
        functools.partial(_rwkv_prep_body, tt=tt), grid=(b, t // tt), in_specs=in_specs,
        out_specs=[ospec] * 8, out_shape=[jax.ShapeDtypeStruct((b, t, B_W), F32)] * 8,
        scratch_shapes=[pltpu.VMEM((SUBLANES, 3 * B_W), F32), pltpu.VMEM((SUBLANES, LORA_PAD), F32)],
        compiler_params=_params(("parallel", "arbitrary"), blk), name="rwkv_prep")(
            z3, z3, sx, slo, p['mu_x'], p['mu_lo'], p['w0'], p['a0'], p['k_k'], p['k_a'], p['r_k'],
            p['w_up'], p['a_up'], p['g_up'], p['ones_bd'])


N_PARTIAL = 4


def _rwkv_steps(r_ref, w_ref, k_ref, kk_ref, b_ref, v_ref, o_ref, state, *, tc, vp):
    def step(t, carry):
        parts = [jnp.zeros((vp, LANES), F32)] * N_PARTIAL
        for k in range(HEAD_DIM_B):
            parts[k % N_PARTIAL] = parts[k % N_PARTIAL] + state[k] * kk_ref[t, pl.ds(k, 1), :]
        sa = -((parts[0] + parts[1]) + (parts[2] + parts[3]))
        vt = v_ref[t]
        parts = [jnp.zeros((vp, LANES), F32)] * N_PARTIAL
        for k in range(HEAD_DIM_B):
            s = (state[k] * w_ref[t, pl.ds(k, 1), :] + sa * b_ref[t, pl.ds(k, 1), :]
                 + vt * k_ref[t, pl.ds(k, 1), :])
            state[k] = s
            parts[k % N_PARTIAL] = parts[k % N_PARTIAL] + s * r_ref[t, pl.ds(k, 1), :]
        o_ref[t] = (parts[0] + parts[1]) + (parts[2] + parts[3])
        return carry

    lax.fori_loop(0, tc, step, 0)


def _rwkv_scan_body(r_ref, w_ref, k_ref, kk_ref, b_ref, v_ref, s0_ref, o_ref, sout_ref, state, *, tc, vp):
    @pl.when(pl.program_id(0) == 0)
    def _():
        state[...] = s0_ref[...]

    _rwkv_steps(r_ref, w_ref, k_ref, kk_ref, b_ref, v_ref, o_ref, state, tc=tc, vp=vp)

    @pl.when(pl.program_id(0) == pl.num_programs(0) - 1)
    def _():
        sout_ref[...] = state[...]


def _rwkv_scan_rows_body(r_ref, w_ref, k_ref, kk_ref, b_ref, v_ref, s0_ref, o_ref, sout_ref,
                         state, yt, rk, wk, kkey, kkk, bk, vk, ok, *, tc, nb, dup):
    vp = HEAD_DIM_B // dup
    nh = N_HEADS_B

    @pl.when(pl.program_id(0) == 0)
    def _():
        state[...] = s0_ref[...]

    def to_lanes(x_ref, dst, n_rows, row_of):
        for b in range(nb):
            yt[b] = x_ref[b].T
        for j in range(n_rows):
            pieces = [yt[b, pl.ds(row_of(j, vh), nh, stride=HEAD_DIM_B), :] for b in range(nb) for vh in range(dup)]
            dst[:, j, :] = jnp.concatenate(pieces, axis=0).T

    for x_ref, dst in ((r_ref, rk), (w_ref, wk), (k_ref, kkey), (kk_ref, kkk), (b_ref, bk)):
        to_lanes(x_ref, dst, HEAD_DIM_B, lambda j, vh: j)
    to_lanes(v_ref, vk, vp, lambda j, vh: vh * vp + j)

    _rwkv_steps(rk, wk, kkey, kkk, bk, vk, ok, state, tc=tc, vp=vp)

    for j in range(vp):
        m = ok[:, j, :].T
        for b in range(nb):
            for vh in range(dup):
                lane0 = (b * dup + vh) * nh
                yt[b, pl.ds(vh * vp + j, nh, stride=HEAD_DIM_B), :] = m[lane0:lane0 + nh, :]
    for b in range(nb):
        o_ref[b] = yt[b].T

    @pl.when(pl.program_id(0) == pl.num_programs(0) - 1)
    def _():
        sout_ref[...] = state[...]


def _rwkv_scan_rows(r, w, k, kk, bv, v, s0, dup):
    nb, t, _ = r.shape
    vp = HEAD_DIM_B // dup
    tc = LANES
    assert t % tc == 0 and nb * dup * N_HEADS_B == LANES
    xspec = pl.BlockSpec((nb, tc, B_W), lambda i: (0, i, 0))
    sspec = pl.BlockSpec((HEAD_DIM_B, vp, LANES), lambda i: (0, 0, 0))
    key_tile = pltpu.VMEM((tc, HEAD_DIM_B, LANES), F32)
    val_tile = pltpu.VMEM((tc, vp, LANES), F32)
    blk = 7 * _nbytes((nb, tc, B_W), F32) + 2 * _nbytes((HEAD_DIM_B, vp, LANES), F32)
    scr = (_nbytes((HEAD_DIM_B, vp, LANES), F32) + _nbytes((nb, B_W, tc), F32)
           + 5 * _nbytes((tc, HEAD_DIM_B, LANES), F32) + 2 * _nbytes((tc, vp, LANES), F32))
    return pl.pallas_call(
        functools.partial(_rwkv_scan_rows_body, tc=tc, nb=nb, dup=dup), grid=(t // tc,),
        in_specs=[xspec] * 6 + [sspec], out_specs=[xspec, sspec],
        out_shape=[jax.ShapeDtypeStruct((nb, t, B_W), F32), jax.ShapeDtypeStruct((HEAD_DIM_B, vp, LANES), F32)],
        scratch_shapes=[pltpu.VMEM((HEAD_DIM_B, vp, LANES), F32), pltpu.VMEM((nb, B_W, tc), F32)]
        + [key_tile] * 5 + [val_tile] * 2,
        compiler_params=_params(("arbitrary",), blk, scr), name="rwkv_scan_rows")(r, w, k, kk, bv, v, s0)


def _rwkv_scan(r, w, k, kk, bv, v, s0):
    t, vp, _ = v.shape
    tc = min(t, 64)
    kspec = pl.BlockSpec((tc, HEAD_DIM_B, LANES), lambda i: (i, 0, 0))
    vspec = pl.BlockSpec((tc, vp, LANES), lambda i: (i, 0, 0))
    sspec = pl.BlockSpec((HEAD_DIM_B, vp, LANES), lambda i: (0, 0, 0))
    blk = 5 * _nbytes((tc, HEAD_DIM_B, LANES), F32) + 2 * _nbytes((tc, vp, LANES), F32) \
        + 2 * _nbytes((HEAD_DIM_B, vp, LANES), F32)
    return pl.pallas_call(
        functools.partial(_rwkv_scan_body, tc=tc, vp=vp), grid=(t // tc,),
        in_specs=[kspec] * 5 + [vspec, sspec], out_specs=[vspec, sspec],
        out_shape=[jax.ShapeDtypeStruct((t, vp, LANES), F32), jax.ShapeDtypeStruct((HEAD_DIM_B, vp, LANES), F32)],
        scratch_shapes=[pltpu.VMEM((HEAD_DIM_B, vp, LANES), F32)],
        compiler_params=_params(("arbitrary",), blk, _nbytes((HEAD_DIM_B, vp, LANES), F32)),
        name="rwkv_scan")(r, w, k, kk, bv, v, s0)


def _rwkv_mix(z3, shift_prev, wkv0, p):
    b, t, _ = z3.shape
    nh, hd = N_HEADS_B, HEAD_DIM_B
    r, w, kmod, v, kk, bv, g, bonus = _rwkv_prep(z3, shift_prev, p)
    dup = LANES // (b * nh)
    assert dup * b * nh == LANES and hd % dup == 0
    vp = hd // dup

    def key_layout(x):
        y = x.reshape(b, t, nh, hd).transpose(1, 3, 0, 2)[:, :, :, None]
        return jnp.broadcast_to(y, (t, hd, b, dup, nh)).reshape(t, hd, LANES)

    def val_layout(x):
        return x.reshape(b, t, nh, dup, vp).transpose(1, 4, 0, 3, 2).reshape(t, vp, LANES)

    s0 = wkv0.reshape(b, nh, dup, vp, hd).transpose(4, 3, 0, 2, 1).reshape(hd, vp, LANES)
    if t % LANES == 0:
        o, s = _rwkv_scan_rows(r, w, kmod, kk, bv, v, s0, dup)
    else:
        o, s = _rwkv_scan(key_layout(r), key_layout(w), key_layout(kmod), key_layout(kk), key_layout(bv),
                          val_layout(v), s0)
        o = o.reshape(t, vp, b, dup, nh).transpose(2, 0, 4, 3, 1).reshape(b, t, B_W)
    s = s.reshape(hd, vp, b, dup, nh).transpose(2, 4, 3, 1, 0).reshape(b, nh, hd, hd)
    return o, bonus, g, s


def _gmlp_body(u_ref, v_ref, ws_ref, b_ref, o_ref, *, tc):
    keep = (lax.broadcasted_iota(jnp.int32, (CHUNK, CHUNK), 1) <= lax.broadcasted_iota(jnp.int32, (CHUNK, CHUNK), 0))
    for g in range(N_GROUPS_C):
        sl = slice(g * CHUNK, (g + 1) * CHUNK)
        w = jnp.where(keep, ws_ref[g], 0.0).astype(BF16)
        v = v_ref[0, :, sl]
        if tc < CHUNK:
            v = jnp.concatenate([v, jnp.zeros((CHUNK - tc, CHUNK), F32)], axis=0)
        s = jnp.dot(w, v.astype(BF16), preferred_element_type=F32) + b_ref[g]
        o_ref[0, :, sl] = u_ref[0, :, sl] * s[:tc]


def _gmlp(z3, w_s, b_s):
    b, t, _ = z3.shape
    tc = min(t, CHUNK)
    assert t % tc == 0
    blk = 3 * _nbytes((tc, C_W), F32) + 2 * _nbytes((N_GROUPS_C, CHUNK, CHUNK), F32)
    return pl.pallas_call(
        functools.partial(_gmlp_body, tc=tc), grid=(b, t // tc),
        in_specs=[pl.BlockSpec((1, tc, C_W), lambda bi, ci: (bi, ci, Z_CU // C_W)),
                  pl.BlockSpec((1, tc, C_W), lambda bi, ci: (bi, ci, Z_CV // C_W)),
                  pl.BlockSpec((N_GROUPS_C, CHUNK, CHUNK), lambda bi, ci: (0, 0, 0)),
                  pl.BlockSpec((N_GROUPS_C, CHUNK, 1), lambda bi, ci: (0, 0, 0))],
        out_specs=pl.BlockSpec((1, tc, C_W), lambda bi, ci: (bi, ci, 0)),
        out_shape=jax.ShapeDtypeStruct((b, t, C_W), F32),
        compiler_params=_params(("parallel", "parallel"), blk), name="gmlp")(z3, z3, w_s, b_s[:, :, None])


POOL_HALO = 16


def _pool_body(p_ref, pre_ref, w_ref, sc_ref, o_ref, ext, *, t_len, pos0, tc):
    ext[0:POOL_HALO, :] = pre_ref[0]
    ext[POOL_HALO:POOL_HALO + t_len, :] = p_ref[0]
    for c0 in range(0, t_len, tc):
        pos = pos0 + c0 + lax.broadcasted_iota(jnp.int32, (tc, 1), 0)
        for g, win in enumerate(POOL_WINDOWS):
            sl = slice(g * POOL_GROUP, (g + 1) * POOL_GROUP)
            base = POOL_HALO + c0
            acc = ext[base:base + tc, sl]
            for i in range(1, win):
                acc = acc + ext[base - i:base - i + tc, sl]
            cnt = jnp.minimum(win, pos + 1).astype(F32)
            pooled = acc / cnt - p_ref[0, c0:c0 + tc, sl]
            y = jnp.dot(pooled.astype(BF16), w_ref[g], preferred_element_type=F32)
            o_ref[0, c0:c0 + tc, sl] = y * sc_ref[:, sl]


def _pool(z3, prefix, pos0, w_pool, scale):
    b, t, _ = z3.shape
    assert POOL_PREV < POOL_HALO
    pre = jnp.pad(prefix, ((0, 0), (POOL_HALO - POOL_PREV, 0), (0, 0)))
    tc = min(t, 256)
    blk = 2 * _nbytes((t, D_W), F32) + _nbytes((POOL_HALO, D_W), F32) + _nbytes(w_pool.shape, BF16)
    scr = _nbytes((t + POOL_HALO, D_W), F32)
    return pl.pallas_call(
        functools.partial(_pool_body, t_len=t, pos0=pos0, tc=tc), grid=(b,),
        in_specs=[pl.BlockSpec((1, t, D_W), lambda bi: (bi, 0, Z_D // D_W)),
                  pl.BlockSpec((1, POOL_HALO, D_W), lambda bi: (bi, 0, 0)),
                  pl.BlockSpec(w_pool.shape, lambda bi: (0, 0, 0)),
                  pl.BlockSpec((1, D_W), lambda bi: (0, 0))],
        out_specs=pl.BlockSpec((1, t, D_W), lambda bi: (bi, 0, 0)),
        out_shape=jax.ShapeDtypeStruct((b, t, D_W), F32),
        scratch_shapes=[pltpu.VMEM((t + POOL_HALO, D_W), F32)],
        compiler_params=_params(("parallel",), blk, scr), name="pool")(z3, pre, w_pool, scale.reshape(1, D_W))


def _mix_body(oa_ref, ob_ref, bonus_ref, g_ref, oc_ref, od_ref, ga_ref, lnw_ref, lnb_ref, gc_ref, gd_ref, ones_ref,
              o_ref):
    def rms(x, gain):
        ms = jnp.mean(x * x, axis=-1, keepdims=True)
        return x * lax.rsqrt(ms + RMS_EPS) * gain

    ones_bd = ones_ref[...]
    o = ob_ref[...]
    mean = _group_sum(o, ones_bd) * (1.0 / HEAD_DIM_B)
    d = o - mean
    var = _group_sum(d * d, ones_bd) * (1.0 / HEAD_DIM_B)
    ob = d * lax.rsqrt(var + GN_EPS) * lnw_ref[...] + lnb_ref[...]
    ob = (ob + bonus_ref[...]) * g_ref[...]
    o_ref[:, 0:A_W] = rms(oa_ref[...], ga_ref[...]).astype(BF16)
    o_ref[:, A_W:A_W + B_W] = ob.astype(BF16)
    o_ref[:, A_W + B_W:A_W + B_W + C_W] = rms(oc_ref[...], gc_ref[...]).astype(BF16)
    o_ref[:, A_W + B_W + C_W:] = rms(od_ref[...], gd_ref[...]).astype(BF16)


def _mix(oa, ob, bonus, g, oc, od, ga, lnw, lnb, gc, gd, ones_bd):
    m = oa.shape[0]
    tm = min(m, 256)
    act = pl.BlockSpec((tm, A_W), lambda i: (i, 0))
    row = pl.BlockSpec((1, A_W), lambda i: (0, 0))
    blk = 6 * _nbytes((tm, A_W), F32) + _nbytes((tm, D_MODEL), BF16) + _nbytes((512, A_W), F32)
    r1 = lambda a: a.reshape(1, -1)
    return pl.pallas_call(
        _mix_body, grid=(m // tm,),
        in_specs=[act] * 6 + [row] * 5 + [pl.BlockSpec((LANES, LANES), lambda i: (0, 0))],
        out_specs=pl.BlockSpec((tm, D_MODEL), lambda i: (i, 0)),
        out_shape=jax.ShapeDtypeStruct((m, D_MODEL), BF16),
        compiler_params=_params(("parallel",), blk), name="mix")(
            oa, ob, bonus, g, oc, od, r1(ga), r1(lnw), r1(lnb), r1(gc), r1(gd), ones_bd)


def _prep_matmul_weights(w):
    w_in = w['w_in']
    zpad = jnp.zeros((DEPTH, D_MODEL, Z_W - Z_LORA - LORA_ALL), w_in.dtype)
    w_in_z = jnp.concatenate(
        [w_in[..., :OFF_B + 3 * B_W], w_in[..., OFF_C:], w_in[..., OFF_B + 3 * B_W:OFF_C], zpad], axis=-1)
    out = {n: w[n].astype(BF16) for n in ('w1_gate', 'w1_up', 'w1_down', 'w2_gate', 'w2_up', 'w2_down', 'w_out')}
    out['w_in'] = w_in_z.astype(BF16)
    return out


def _prep_layer_weights(l, w):
    mu = w['mu_b'][l]
    pad_rows = lambda a, before, total: jnp.pad(a, ((before, total - before - a.shape[0]), (0, 0))).astype(BF16)
    ones_bd = jnp.kron(jnp.eye(LANES // HEAD_DIM_B, dtype=F32), jnp.ones((HEAD_DIM_B, HEAD_DIM_B), F32))
    rw = dict(
        mu_x=mu[None, :3 * B_W], mu_lo=jnp.pad(mu[None, 3 * B_W:], ((0, 0), (0, LORA_PAD - LORA_ALL))),
        w0=w['w0'][l][None], a0=w['a0'][l][None], k_k=w['k_k'][l][None], k_a=w['k_a'][l][None],
        r_k=w['r_k'][l].reshape(1, B_W),
        w_up=pad_rows(w['w_up'][l], 0, LANES), a_up=pad_rows(w['a_up'][l], LORA_W, LANES),
        g_up=pad_rows(w['g_up'][l], 0, LORA_PAD - LANES), ones_bd=ones_bd)
    return dict(w_pool=w['w_pool'][l].astype(BF16), rwkv=rw)


def _layer(x, l, w, wm, wl, pos0, kv_prefix, shift_prev, wkv0, pool_prefix):
    b, t, _ = x.shape
    m = b * t
    x2 = x.reshape(m, D_MODEL)
    hid = _ffn_up(_rmsnorm(x2, w['ln_ffn1'][l], BF16), wm['w1_gate'], wm['w1_up'], l)
    h = _matmul(hid, wm['w1_down'], l, res=x2, scale=0.5, tm=512, tn=512)
    z = _matmul(_rmsnorm(h, w['ln_mix'][l], BF16), wm['w_in'], l, tm=1024, tn=512)
    z3 = z.reshape(b, t, Z_W)
    out_a, ka = _attention(z3, pos0, kv_prefix, l)
    o_b, bonus, gate, wkv_new = _rwkv_mix(z3, shift_prev, wkv0, wl['rwkv'])
    out_c = _gmlp(z3, w['w_s'][l], w['b_s'][l])
    out_d = _pool(z3, pool_prefix, pos0, wl['w_pool'], w['pool_scale'][l])
    r2 = lambda a: a.reshape(m, -1)
    mix = _mix(r2(out_a), r2(o_b), r2(bonus), r2(gate), r2(out_c), r2(out_d), w['g_out_a'][l], w['ln_x_w'][l],
               w['ln_x_b'][l], w['g_out_c'][l], w['g_out_d'][l], wl['rwkv']['ones_bd'])
    h = _matmul(mix, wm['w_out'], l, res=h, scale=1.0, tm=1024, tn=512)
    hid = _ffn_up(_rmsnorm(h, w['ln_ffn2'][l], BF16), wm['w2_gate'], wm['w2_up'], l)
    y = _matmul(hid, wm['w2_down'], l, res=h, scale=0.5, tm=512, tn=512)
    hs = (b, t, N_HEADS_A, HEAD_DIM_A)
    va = z3[..., Z_VA:Z_VA + A_W]
    shift_new = jnp.concatenate([z3[:, -1, Z_RKV:Z_RKV + 3 * B_W], z3[:, -1, Z_LORA:Z_LORA + LORA_ALL]], axis=-1)
    p = z3[..., Z_D:Z_D + D_W]
    pool_new = jnp.concatenate([pool_prefix, p], axis=1)[:, -POOL_PREV:]
    vc = z3[..., Z_CV:Z_CV + C_W]
    return y.reshape(b, t, D_MODEL), (ka.reshape(hs), va.reshape(hs), wkv_new, shift_new, pool_new, vc)


def kernel(x_prompt, x_sample, cache_k_swa, cache_v_swa, state_rwkv_wkv, state_rwkv_shift, state_pool, ln_ffn1, w1_gate, w1_up, w1_down, ln_mix, w_in, g_out_a, mu_b, w0, w_up, a0, a_up, g_up, k_k, k_a, r_k, ln_x_w, ln_x_b, w_s, b_s, g_out_c, w_pool, pool_scale, g_out_d, w_out, ln_ffn2, w2_gate, w2_up, w2_down, ln_final):
    w = dict(ln_ffn1=ln_ffn1, w1_gate=w1_gate, w1_up=w1_up, w1_down=w1_down, ln_mix=ln_mix, w_in=w_in,
             g_out_a=g_out_a, mu_b=mu_b, w0=w0, w_up=w_up, a0=a0, a_up=a_up, g_up=g_up, k_k=k_k, k_a=k_a,
             r_k=r_k, ln_x_w=ln_x_w, ln_x_b=ln_x_b, w_s=w_s, b_s=b_s, g_out_c=g_out_c, w_pool=w_pool,
             pool_scale=pool_scale, g_out_d=g_out_d, w_out=w_out, ln_ffn2=ln_ffn2, w2_gate=w2_gate,
             w2_up=w2_up, w2_down=w2_down)
    nbp, t_prompt, _ = x_prompt.shape
    yp, ys = x_prompt, x_sample
    p_states, s_states = [], []
    wm = _prep_matmul_weights(w)
    for l in range(DEPTH):
        wl = _prep_layer_weights(l, w)
        yp, sp = _layer(yp, l, w, wm, wl, 0, None,
                        jnp.zeros((nbp, B_FEAT), F32),
                        jnp.zeros((nbp, N_HEADS_B, HEAD_DIM_B, HEAD_DIM_B), F32),
                        jnp.zeros((nbp, POOL_PREV, D_W), F32))
        ys, ss = _layer(ys, l, w, wm, wl, PAST_LEN, (cache_k_swa, cache_v_swa),
                        state_rwkv_shift[l], state_rwkv_wkv[l], state_pool[l])
        p_states.append(sp)
        s_states.append(ss)
    keep = min(WIN_MAX, t_prompt)
    stack = lambda states, i: jnp.stack([s[i] for s in states])
    y_prompt = _rmsnorm(yp.reshape(-1, D_MODEL), ln_final, F32).reshape(yp.shape)
    y_sample = _rmsnorm(ys.reshape(-1, D_MODEL), ln_final, F32).reshape(ys.shape)
    return (y_prompt, y_sample,
            jnp.stack([s[0][:, -keep:] for s in p_states]), jnp.stack([s[1][:, -keep:] for s in p_states]),
            stack(p_states, 2), stack(p_states, 3), stack(p_states, 4),
            stack(s_states, 0), stack(s_states, 1), stack(s_states, 2), stack(s_states, 3), stack(s_states, 4),
            stack(s_states, 5))
```

```python
import functools

import jax
import jax.numpy as jnp
from jax import lax
from jax.experimental import pallas as pl
from jax.experimental.pallas import tpu as pltpu

F32 = jnp.float32
BF16 = jnp.bfloat16

D_MODEL = 4096
DEPTH = 2
PAST_LEN = 16384
A_W = B_W = C_W = D_W = D_MODEL // 4
RMS_EPS = 1e-6
HEAD_DIM_A = 128
N_HEADS_A = A_W // HEAD_DIM_A
DILATED_BRANCHES = ((128, 1), (512, 4), (2048, 16))
WIN_MAX = 2048
ROPE_THETA = 10000.0
HEAD_DIM_B = 64
N_HEADS_B = B_W // HEAD_DIM_B
LORA_W, LORA_A, LORA_G = 64, 64, 160
LORA_ALL = LORA_W + LORA_A + LORA_G
GN_EPS = 64e-5
B_FEAT = 3 * B_W + LORA_ALL
CHUNK = 128
N_GROUPS_C = 8
POOL_WINDOWS = (2, 4, 8, 16)
POOL_PREV = max(POOL_WINDOWS) - 1
POOL_GROUP = D_W // len(POOL_WINDOWS)
OFF_B = 3 * A_W
OFF_C = OFF_B + B_FEAT
OFF_D = OFF_C + 2 * C_W

LANES = 128
SUBLANES = 8
VMEM_BYTES_V7X = 64 * 2**20
VMEM_INTERNAL_RESERVE = 12 * 2**20

MXU_COLS_V7X = 256
MM_TN = 2 * MXU_COLS_V7X

ZM_QA, ZM_KA, ZM_VA = 0, A_W, 2 * A_W
ZM_RKV = 3 * A_W
ZM_W = ZM_RKV + 3 * B_W
ZR_CU = 0
ZR_CV = ZR_CU + C_W
ZR_D = ZR_CV + C_W
ZR_LORA = ZR_D + D_W
LORA_PAD = 3 * LANES
ZR_W = -(-(ZR_LORA + LORA_PAD) // MM_TN) * MM_TN

NEG = -1e30


def _vmem_limit(block_bytes, scratch_bytes=0):
    need = 2 * block_bytes + scratch_bytes + VMEM_INTERNAL_RESERVE
    return int(min(max(need, 16 * 2**20), VMEM_BYTES_V7X - 4 * 2**20))


def _params(sem, block_bytes, scratch_bytes=0):
    return pltpu.CompilerParams(dimension_semantics=sem,
                                vmem_limit_bytes=_vmem_limit(block_bytes, scratch_bytes))


def _nbytes(shape, dtype):
    n = 1
    for s in shape:
        n *= s
    return n * jnp.dtype(dtype).itemsize


def _rmsnorm_body(x_ref, g_ref, o_ref):
    x = x_ref[...]
    ms = jnp.mean(x * x, axis=-1, keepdims=True)
    o_ref[...] = (x * lax.rsqrt(ms + RMS_EPS) * g_ref[...]).astype(o_ref.dtype)


def _rmsnorm(x, g, out_dtype):
    m, d = x.shape
    tm = min(m, 256)
    blk = _nbytes((tm, d), F32) + _nbytes((tm, d), out_dtype)
    return pl.pallas_call(
        _rmsnorm_body, grid=(m // tm,),
        in_specs=[pl.BlockSpec((tm, d), lambda i: (i, 0)), pl.BlockSpec((1, d), lambda i: (0, 0))],
        out_specs=pl.BlockSpec((tm, d), lambda i: (i, 0)),
        out_shape=jax.ShapeDtypeStruct((m, d), out_dtype),
        compiler_params=_params(("parallel",), blk), name="rmsnorm")(x, g.reshape(1, d))


def _ffn_up_body(x_ref, wg_ref, wu_ref, o_ref):
    x = x_ref[...]
    g = jnp.dot(x, wg_ref[...], preferred_element_type=F32)
    u = jnp.dot(x, wu_ref[...], preferred_element_type=F32)
    o_ref[...] = (g * jax.nn.sigmoid(g) * u).astype(o_ref.dtype)


def _ffn_up_cast_body(x_ref, wg_ref, wu_ref, o_ref, wgo_ref, wuo_ref):
    wg = wg_ref[...].astype(BF16)
    wu = wu_ref[...].astype(BF16)
    wgo_ref[...] = wg
    wuo_ref[...] = wu
    x = x_ref[...]
    g = jnp.dot(x, wg, preferred_element_type=F32)
    u = jnp.dot(x, wu, preferred_element_type=F32)
    o_ref[...] = (g * jax.nn.sigmoid(g) * u).astype(o_ref.dtype)


def _ffn_up(x, wg, wu):
    m, k = x.shape
    n = wg.shape[1]
    tm = min(m, 1024)
    tn = MXU_COLS_V7X
    blk = _nbytes((tm, k), BF16) + 2 * _nbytes((k, tn), BF16) + _nbytes((tm, tn), BF16)
    return pl.pallas_call(
        _ffn_up_body, grid=(m // tm, n // tn),
        in_specs=[pl.BlockSpec((tm, k), lambda i, j: (i, 0)),
                  pl.BlockSpec((k, tn), lambda i, j: (0, j)),
                  pl.BlockSpec((k, tn), lambda i, j: (0, j))],
        out_specs=pl.BlockSpec((tm, tn), lambda i, j: (i, j)),
        out_shape=jax.ShapeDtypeStruct((m, n), BF16),
        compiler_params=_params(("parallel", "arbitrary"), blk), name="ffn_up")(x, wg, wu)


def _ffn_up_cast(x, wg, wu, l):
    m, k = x.shape
    n = wg.shape[2]
    tn = MXU_COLS_V7X
    blk = _nbytes((m, k), BF16) + 2 * _nbytes((k, tn), F32) + 2 * _nbytes((k, tn), BF16) + _nbytes((m, tn), BF16)
    wspec = pl.BlockSpec((None, k, tn), lambda j: (l, 0, j))
    ospec = pl.BlockSpec((k, tn), lambda j: (0, j))
    return pl.pallas_call(
        _ffn_up_cast_body, grid=(n // tn,),
        in_specs=[pl.BlockSpec((m, k), lambda j: (0, 0)), wspec, wspec],
        out_specs=[pl.BlockSpec((m, tn), lambda j: (0, j)), ospec, ospec],
        out_shape=[jax.ShapeDtypeStruct((m, n), BF16), jax.ShapeDtypeStruct((k, n), BF16),
                   jax.ShapeDtypeStruct((k, n), BF16)],
        compiler_params=_params(("parallel",), blk), name="ffn_up_cast")(x, wg, wu)


def _mm_body(a_ref, b_ref, o_ref):
    o_ref[...] = jnp.dot(a_ref[...], b_ref[...], preferred_element_type=F32)


def _mm_res_body(a_ref, b_ref, r_ref, o_ref, *, scale):
    acc = jnp.dot(a_ref[...], b_ref[...], preferred_element_type=F32)
    o_ref[...] = r_ref[...] + scale * acc


def _mm_cast_body(a_ref, b_ref, *rest, scale, has_res):
    r_ref = rest[0] if has_res else None
    o_ref, bo_ref = rest[-2:]
    b = b_ref[...].astype(BF16)
    bo_ref[...] = b
    acc = jnp.dot(a_ref[...], b, preferred_element_type=F32)
    o_ref[...] = r_ref[...] + scale * acc if has_res else acc


def _matmul(a, b, l=None, res=None, scale=1.0, tm=1024):
    m, k = a.shape
    n = b.shape[-1]
    tm = min(m, tm)
    tn = MM_TN
    assert m % tm == 0 and n % tn == 0
    blk = _nbytes((tm, k), BF16) + _nbytes((k, tn), BF16) + _nbytes((tm, tn), F32)
    bspec = (pl.BlockSpec((k, tn), lambda i, j: (0, j)) if b.ndim == 2
             else pl.BlockSpec((None, k, tn), lambda i, j: (l, 0, j)))
    in_specs = [pl.BlockSpec((tm, k), lambda i, j: (i, 0)), bspec]
    args = [a, b]
    if res is None:
        body = _mm_body
    else:
        body = functools.partial(_mm_res_body, scale=scale)
        in_specs.append(pl.BlockSpec((tm, tn), lambda i, j: (i, j)))
        args.append(res)
        blk += _nbytes((tm, tn), F32)
    return pl.pallas_call(
        body, grid=(m // tm, n // tn), in_specs=in_specs,
        out_specs=pl.BlockSpec((tm, tn), lambda i, j: (i, j)),
        out_shape=jax.ShapeDtypeStruct((m, n), F32),
        compiler_params=_params(("parallel", "arbitrary"), blk), name="matmul")(*args)


CAST_TILE_BYTES = 6 * 2**20


def _matmul_cast(a, b, l, n, res=None, scale=1.0):
    m, k = a.shape
    tn = MM_TN
    while _nbytes((k, tn), F32) > CAST_TILE_BYTES:
        tn //= 2
    assert n % tn == 0 and tn % LANES == 0
    blk = _nbytes((m, k), BF16) + _nbytes((k, tn), F32) + _nbytes((k, tn), BF16) + 2 * _nbytes((m, tn), F32)
    in_specs = [pl.BlockSpec((m, k), lambda j: (0, 0)), pl.BlockSpec((None, k, tn), lambda j: (l, 0, j))]
    args = [a, b]
    if res is not None:
        in_specs.append(pl.BlockSpec((m, tn), lambda j: (0, j)))
        args.append(res)
    return pl.pallas_call(
        functools.partial(_mm_cast_body, scale=scale, has_res=res is not None), grid=(n // tn,),
        in_specs=in_specs,
        out_specs=[pl.BlockSpec((m, tn), lambda j: (0, j)), pl.BlockSpec((k, tn), lambda j: (0, j))],
        out_shape=[jax.ShapeDtypeStruct((m, n), F32), jax.ShapeDtypeStruct((k, n), BF16)],
        compiler_params=_params(("parallel",), blk), name="matmul_cast")(*args)


def _rope(x, cos, sin_signed):
    return x * cos + pltpu.roll(x, HEAD_DIM_A // 2, 1) * sin_signed


def _branch_multiplicity(delta):
    c = jnp.zeros(delta.shape, F32)
    for window, dilation in DILATED_BRANCHES:
        assert dilation & (dilation - 1) == 0
        hit = jnp.where(delta <= window, 1.0, 0.0)
        if dilation > 1:
            hit = jnp.where((delta & (dilation - 1)) == 0, hit, 0.0)
        c = c + hit
    return jnp.where(delta >= 0, c, 0.0)


_TRANS_B = (((1,), (1,)), ((), ()))


def _attn_prompt_body(q_ref, k_ref, v_ref, cos_ref, sin_ref, o_ref, kout_ref, qs, ks, vs, ctab, *, t_len, tq):
    nq = t_len // tq

    @pl.when((pl.program_id(0) == 0) & (pl.program_id(1) == 0))
    def _():
        rel = (lax.broadcasted_iota(jnp.int32, (tq, tq), 0) - lax.broadcasted_iota(jnp.int32, (tq, tq), 1))
        for d in range(nq):
            ctab[:, d * tq:(d + 1) * tq] = _branch_multiplicity(rel + (nq - 1 - d) * tq)

    cos = cos_ref[...]
    sin = sin_ref[...]
    k = _rope(k_ref[0], cos, sin)
    kout_ref[0] = k
    ks[...] = k.astype(BF16)
    qs[...] = (_rope(q_ref[0], cos, sin) * (HEAD_DIM_A ** -0.5)).astype(BF16)
    vs[...] = v_ref[0].astype(BF16)
    for i in range(nq):
        kw = (i + 1) * tq
        s = lax.dot_general(qs[i * tq:(i + 1) * tq, :], ks[0:kw, :], _TRANS_B, preferred_element_type=F32)
        c = ctab[:, (nq - 1 - i) * tq:]
        sm = jnp.where(c > 0.0, s, NEG)
        p = jnp.exp(sm - jnp.max(sm, axis=-1, keepdims=True)) * c
        l = jnp.sum(p, axis=-1, keepdims=True)
        acc = jnp.dot(p.astype(BF16), vs[0:kw, :], preferred_element_type=F32)
        o_ref[0, i * tq:(i + 1) * tq, :] = acc / l


def _attn_sample_body(q_ref, k_ref, v_ref, kp_ref, vp_ref, cos_ref, sin_ref, o_ref, kout_ref, *, t_len, n_prev):
    cos = cos_ref[...]
    sin = sin_ref[...]
    d1 = (n_prev + lax.broadcasted_iota(jnp.int32, (t_len, n_prev), 0)
          - lax.broadcasted_iota(jnp.int32, (t_len, n_prev), 1))
    c1 = _branch_multiplicity(d1)
    d2 = (lax.broadcasted_iota(jnp.int32, (t_len, LANES), 0) - lax.broadcasted_iota(jnp.int32, (t_len, LANES), 1))
    c2 = _branch_multiplicity(d2)
    pad = jnp.zeros((LANES - t_len, HEAD_DIM_A), F32)
    for h in range(N_HEADS_A):
        sl = slice(h * HEAD_DIM_A, (h + 1) * HEAD_DIM_A)
        k = _rope(k_ref[0, :, sl], cos, sin)
        kout_ref[0, :, sl] = k
        q = (_rope(q_ref[0, :, sl], cos, sin) * (HEAD_DIM_A ** -0.5)).astype(BF16)
        kn = jnp.concatenate([k, pad], axis=0).astype(BF16)
        vn = jnp.concatenate([v_ref[0, :, sl], pad], axis=0).astype(BF16)
        s1 = lax.dot_general(q, kp_ref[:, h, :].astype(BF16), _TRANS_B, preferred_element_type=F32)
        s2 = lax.dot_general(q, kn, _TRANS_B, preferred_element_type=F32)
        sm1 = jnp.where(c1 > 0.0, s1, NEG)
        sm2 = jnp.where(c2 > 0.0, s2, NEG)
        m = jnp.maximum(jnp.max(sm1, axis=-1, keepdims=True), jnp.max(sm2, axis=-1, keepdims=True))
        p1 = jnp.exp(sm1 - m) * c1
        p2 = jnp.exp(sm2 - m) * c2
        l = jnp.sum(p1, axis=-1, keepdims=True) + jnp.sum(p2, axis=-1, keepdims=True)
        acc = (jnp.dot(p1.astype(BF16), vp_ref[:, h, :].astype(BF16), preferred_element_type=F32)
               + jnp.dot(p2.astype(BF16), vn, preferred_element_type=F32))
        o_ref[0, :, sl] = acc / l


def _rope_tables(pos0, t_len):
    half = HEAD_DIM_A // 2
    inv = ROPE_THETA ** (-jnp.arange(half, dtype=F32) / half)
    ang = (pos0 + jnp.arange(t_len)).astype(F32)[:, None] * inv[None, :]
    cos = jnp.cos(ang)
    sin = jnp.sin(ang)
    return jnp.concatenate([cos, cos], axis=-1), jnp.concatenate([-sin, sin], axis=-1)


def _attention(z3, pos0, kv_prefix, l):
    b, t, _ = z3.shape
    cos, sin = _rope_tables(pos0, t)
    hd = HEAD_DIM_A
    out_shape = [jax.ShapeDtypeStruct((b, t, A_W), F32), jax.ShapeDtypeStruct((b, t, A_W), F32)]
    if kv_prefix is None:
        col = lambda base: (lambda bi, hi: (bi, 0, base // hd + hi))
        tab = pl.BlockSpec((t, hd), lambda bi, hi: (0, 0))
        zspecs = [pl.BlockSpec((1, t, hd), col(ZM_QA)), pl.BlockSpec((1, t, hd), col(ZM_KA)),
                  pl.BlockSpec((1, t, hd), col(ZM_VA))]
        out_specs = [pl.BlockSpec((1, t, hd), col(0)), pl.BlockSpec((1, t, hd), col(0))]
        blk = 7 * _nbytes((t, hd), F32)
        tq = min(t, 256)
        body = functools.partial(_attn_prompt_body, t_len=t, tq=tq)
        scratch = 3 * _nbytes((t, hd), BF16) + 5 * _nbytes((tq, t), F32)
        return pl.pallas_call(
            body, grid=(b, N_HEADS_A), in_specs=zspecs + [tab, tab], out_specs=out_specs, out_shape=out_shape,
            scratch_shapes=[pltpu.VMEM((t, hd), BF16)] * 3 + [pltpu.VMEM((tq, t), F32)],
            compiler_params=_params(("arbitrary", "arbitrary"), blk, scratch),
            name="attn_prompt")(z3, z3, z3, cos, sin)
    k_prev, v_prev = kv_prefix
    n_prev = k_prev.shape[2]
    assert t <= LANES and n_prev % LANES == 0 and k_prev.shape[3:] == (N_HEADS_A, hd)
    col = lambda base: (lambda bi: (bi, 0, base // A_W))
    tab = pl.BlockSpec((t, hd), lambda bi: (0, 0))
    zspecs = [pl.BlockSpec((1, t, A_W), col(ZM_QA)), pl.BlockSpec((1, t, A_W), col(ZM_KA)),
              pl.BlockSpec((1, t, A_W), col(ZM_VA))]
    out_specs = [pl.BlockSpec((1, t, A_W), col(0)), pl.BlockSpec((1, t, A_W), col(0))]
    pspec = pl.BlockSpec((None, None, n_prev, N_HEADS_A, hd), lambda bi: (l, bi, 0, 0, 0))
    blk = 5 * _nbytes((t, A_W), F32) + 2 * _nbytes((n_prev, A_W), F32)
    body = functools.partial(_attn_sample_body, t_len=t, n_prev=n_prev)
    return pl.pallas_call(
        body, grid=(b,), in_specs=zspecs + [pspec, pspec, tab, tab], out_specs=out_specs,
        out_shape=out_shape, compiler_params=_params(("parallel",), blk),
        name="attn_sample")(z3, z3, z3, k_prev, v_prev, cos, sin)


def _group_sum(x, ones_blockdiag):
    outs = []
    for j in range(x.shape[-1] // LANES):
        outs.append(jnp.dot(x[:, j * LANES:(j + 1) * LANES], ones_blockdiag, preferred_element_type=F32,
                            precision=lax.Precision.HIGHEST))
    return jnp.concatenate(outs, axis=-1)


def _softplus(y):
    return jnp.maximum(y, 0.0) + jnp.log1p(jnp.exp(-jnp.abs(y)))


def _rwkv_prep_body(x_ref, lo_ref, sx_ref, slo_ref, mux_ref, mulo_ref, w0_ref, a0_ref, kkp_ref, kap_ref, rk_ref,
                    wup_ref, aup_ref, gup_ref, ones_ref,
                    r_o, w_o, k_o, v_o, kk_o, b_o, g_o, bonus_o, last_x, last_lo, *, tt):
    @pl.when(pl.program_id(1) == 0)
    def _():
        last_x[0:1, :] = sx_ref[0]
        last_lo[0:1, :] = slo_ref[0]

    x = x_ref[0]
    lo = lo_ref[0]
    first = lax.broadcasted_iota(jnp.int32, (tt, 1), 0) == 0
    px = jnp.where(first, last_x[0:1, :], pltpu.roll(x, 1, 0))
    plo = jnp.where(first, last_lo[0:1, :], pltpu.roll(lo, 1, 0))
    last_x[0:1, :] = x[tt - 1:tt, :]
    last_lo[0:1, :] = lo[tt - 1:tt, :]
    fx = x + mux_ref[...] * (px - x)
    flo = lo + mulo_ref[...] * (plo - lo)
    r = fx[:, :B_W]
    k = fx[:, B_W:2 * B_W]
    v = fx[:, 2 * B_W:]
    zwa = flo[:, :LANES]
    zg = flo[:, LANES:]
    ones_bd = ones_ref[...]
    wl = w0_ref[...] + jnp.dot(jnp.tanh(zwa).astype(BF16), wup_ref[...], preferred_element_type=F32)
    w_log = -_softplus(-wl) - 0.5
    decay = jnp.exp(-jnp.exp(w_log))
    a = jax.nn.sigmoid(a0_ref[...] + jnp.dot(zwa.astype(BF16), aup_ref[...], preferred_element_type=F32))
    g = jnp.dot(jax.nn.sigmoid(zg).astype(BF16), gup_ref[...], preferred_element_type=F32)
    kk = k * kkp_ref[...]
    kk = kk / jnp.maximum(jnp.sqrt(_group_sum(kk * kk, ones_bd)), 1e-12)
    kmod = k * (1.0 + (a - 1.0) * kap_ref[...])
    r_o[0] = r
    w_o[0] = decay
    k_o[0] = kmod
    v_o[0] = v
    kk_o[0] = kk
    b_o[0] = kk * a
    g_o[0] = g
    bonus_o[0] = _group_sum(r * kmod * rk_ref[...], ones_bd) * v


def _rwkv_prep(zm3, zr3, shift_prev, p):
    b, t, _ = zm3.shape
    tt = min(t, 256)
    sx = shift_prev[:, None, :3 * B_W]
    slo = jnp.pad(shift_prev[:, None, 3 * B_W:], ((0, 0), (0, 0), (0, LORA_PAD - LORA_ALL)))
    row = lambda w: pl.BlockSpec((1, w), lambda bi, ti: (0, 0))
    full = lambda a: pl.BlockSpec(a.shape, lambda bi, ti: (0,) * a.ndim)
    in_specs = [pl.BlockSpec((1, tt, 3 * B_W), lambda bi, ti: (bi, ti, ZM_RKV // (3 * B_W))),
                pl.BlockSpec((1, tt, LORA_PAD), lambda bi, ti: (bi, ti, ZR_LORA // LORA_PAD)),
                pl.BlockSpec((1, 1, 3 * B_W), lambda bi, ti: (bi, 0, 0)),
                pl.BlockSpec((1, 1, LORA_PAD), lambda bi, ti: (bi, 0, 0)),
                row(3 * B_W), row(LORA_PAD), row(B_W), row(B_W), row(B_W), row(B_W), row(B_W),
                full(p['w_up']), full(p['a_up']), full(p['g_up']), full(p['ones_bd'])]
    ospec = pl.BlockSpec((1, tt, B_W), lambda bi, ti: (bi, ti, 0))
    blk = _nbytes((tt, 3 * B_W + LORA_PAD), F32) + 8 * _nbytes((tt, B_W), F32) + 2 * _nbytes((512, B_W), F32)
    return pl.pallas_call(
        functools.partial(_rwkv_prep_body, tt=tt), grid=(b, t // tt), in_specs=in_specs,
        out_specs=[ospec] * 8, out_shape=[jax.ShapeDtypeStruct((b, t, B_W), F32)] * 8,
        scratch_shapes=[pltpu.VMEM((SUBLANES, 3 * B_W), F32), pltpu.VMEM((SUBLANES, LORA_PAD), F32)],
        compiler_params=_params(("parallel", "arbitrary"), blk), name="rwkv_prep")(
            zm3, zr3, sx, slo, p['mu_x'], p['mu_lo'], p['w0'], p['a0'], p['k_k'], p['k_a'], p['r_k'],
            p['w_up'], p['a_up'], p['g_up'], p['ones_bd'])


N_PARTIAL = 4


def _rwkv_steps(r_ref, w_ref, k_ref, kk_ref, b_ref, v_ref, o_ref, state, *, tc, vp, time_major):
    def step(t, row):
        parts = [jnp.zeros((vp, LANES), F32)] * N_PARTIAL
        for k in range(HEAD_DIM_B):
            parts[k % N_PARTIAL] = parts[k % N_PARTIAL] + state[k] * row(kk_ref, k)
        sa = -((parts[0] + parts[1]) + (parts[2] + parts[3]))
        vt = v_ref[t]
        parts = [jnp.zeros((vp, LANES), F32)] * N_PARTIAL
        for k in range(HEAD_DIM_B):
            s = state[k] * row(w_ref, k) + sa * row(b_ref, k) + vt * row(k_ref, k)
            state[k] = s
            parts[k % N_PARTIAL] = parts[k % N_PARTIAL] + s * row(r_ref, k)
        o_ref[t] = (parts[0] + parts[1]) + (parts[2] + parts[3])

    if time_major:
        def one(t, carry):
            step(t, lambda ref, k: ref[t, pl.ds(k, 1), :])
            return carry

        lax.fori_loop(0, tc, one, 0)
    else:
        def eight(tb, carry):
            for s in range(SUBLANES):
                step(tb * SUBLANES + s, lambda ref, k, s=s: ref[k, tb, pl.ds(s, 1), :])
            return carry

        lax.fori_loop(0, tc // SUBLANES, eight, 0)


def _rwkv_scan_body(r_ref, w_ref, k_ref, kk_ref, b_ref, v_ref, s0_ref, o_ref, sout_ref, state, *, tc, vp):
    @pl.when(pl.program_id(0) == 0)
    def _():
        state[...] = s0_ref[...]

    _rwkv_steps(r_ref, w_ref, k_ref, kk_ref, b_ref, v_ref, o_ref, state, tc=tc, vp=vp, time_major=True)

    @pl.when(pl.program_id(0) == pl.num_programs(0) - 1)
    def _():
        sout_ref[...] = state[...]


def _rwkv_scan_rows_body(r_ref, w_ref, k_ref, kk_ref, b_ref, v_ref, s0_ref, o_ref, sout_ref,
                         state, yt, rk, wk, kkey, kkk, bk, vk, ok, *, tc, nb, dup):
    vp = HEAD_DIM_B // dup
    nh = N_HEADS_B

    @pl.when(pl.program_id(0) == 0)
    def _():
        state[...] = s0_ref[...]

    def to_lanes(x_ref, store, n_rows, row_of):
        for b in range(nb):
            yt[b] = x_ref[b].T
        for j in range(n_rows):
            pieces = [yt[b, pl.ds(row_of(j, vh), nh, stride=HEAD_DIM_B), :] for b in range(nb) for vh in range(dup)]
            store(j, jnp.concatenate(pieces, axis=0).T)

    def key_store(dst):
        def store(j, x):
            dst[j] = x.reshape(tc // SUBLANES, SUBLANES, LANES)
        return store

    def val_store(j, x):
        vk[:, j, :] = x

    for x_ref, dst in ((r_ref, rk), (w_ref, wk), (k_ref, kkey), (kk_ref, kkk), (b_ref, bk)):
        to_lanes(x_ref, key_store(dst), HEAD_DIM_B, lambda j, vh: j)
    to_lanes(v_ref, val_store, vp, lambda j, vh: vh * vp + j)

    _rwkv_steps(rk, wk, kkey, kkk, bk, vk, ok, state, tc=tc, vp=vp, time_major=False)

    for j in range(vp):
        m = ok[:, j, :].T
        for b in range(nb):
            for vh in range(dup):
                lane0 = (b * dup + vh) * nh
                yt[b, pl.ds(vh * vp + j, nh, stride=HEAD_DIM_B), :] = m[lane0:lane0 + nh, :]
    for b in range(nb):
        o_ref[b] = yt[b].T

    @pl.when(pl.program_id(0) == pl.num_programs(0) - 1)
    def _():
        sout_ref[...] = state[...]


def _rwkv_scan_rows(r, w, k, kk, bv, v, s0, dup):
    nb, t, _ = r.shape
    vp = HEAD_DIM_B // dup
    tc = LANES
    assert t % tc == 0 and nb * dup * N_HEADS_B == LANES
    xspec = pl.BlockSpec((nb, tc, B_W), lambda i: (0, i, 0))
    sspec = pl.BlockSpec((HEAD_DIM_B, vp, LANES), lambda i: (0, 0, 0))
    key_tile = pltpu.VMEM((HEAD_DIM_B, tc // SUBLANES, SUBLANES, LANES), F32)
    val_tile = pltpu.VMEM((tc, vp, LANES), F32)
    blk = 7 * _nbytes((nb, tc, B_W), F32) + 2 * _nbytes((HEAD_DIM_B, vp, LANES), F32)
    scr = (_nbytes((HEAD_DIM_B, vp, LANES), F32) + _nbytes((nb, B_W, tc), F32)
           + 5 * _nbytes((tc, HEAD_DIM_B, LANES), F32) + 2 * _nbytes((tc, vp, LANES), F32))
    return pl.pallas_call(
        functools.partial(_rwkv_scan_rows_body, tc=tc, nb=nb, dup=dup), grid=(t // tc,),
        in_specs=[xspec] * 6 + [sspec], out_specs=[xspec, sspec],
        out_shape=[jax.ShapeDtypeStruct((nb, t, B_W), F32), jax.ShapeDtypeStruct((HEAD_DIM_B, vp, LANES), F32)],
        scratch_shapes=[pltpu.VMEM((HEAD_DIM_B, vp, LANES), F32), pltpu.VMEM((nb, B_W, tc), F32)]
        + [key_tile] * 5 + [val_tile] * 2,
        compiler_params=_params(("arbitrary",), blk, scr), name="rwkv_scan_rows")(r, w, k, kk, bv, v, s0)


def _rwkv_scan(r, w, k, kk, bv, v, s0):
    t, vp, _ = v.shape
    tc = min(t, 64)
    kspec = pl.BlockSpec((tc, HEAD_DIM_B, LANES), lambda i: (i, 0, 0))
    vspec = pl.BlockSpec((tc, vp, LANES), lambda i: (i, 0, 0))
    sspec = pl.BlockSpec((HEAD_DIM_B, vp, LANES), lambda i: (0, 0, 0))
    blk = 5 * _nbytes((tc, HEAD_DIM_B, LANES), F32) + 2 * _nbytes((tc, vp, LANES), F32) \
        + 2 * _nbytes((HEAD_DIM_B, vp, LANES), F32)
    return pl.pallas_call(
        functools.partial(_rwkv_scan_body, tc=tc, vp=vp), grid=(t // tc,),
        in_specs=[kspec] * 5 + [vspec, sspec], out_specs=[vspec, sspec],
        out_shape=[jax.ShapeDtypeStruct((t, vp, LANES), F32), jax.ShapeDtypeStruct((HEAD_DIM_B, vp, LANES), F32)],
        scratch_shapes=[pltpu.VMEM((HEAD_DIM_B, vp, LANES), F32)],
        compiler_params=_params(("arbitrary",), blk, _nbytes((HEAD_DIM_B, vp, LANES), F32)),
        name="rwkv_scan")(r, w, k, kk, bv, v, s0)


def _rwkv_mix(zm3, zr3, shift_prev, wkv0, p):
    b, t, _ = zm3.shape
    nh, hd = N_HEADS_B, HEAD_DIM_B
    r, w, kmod, v, kk, bv, g, bonus = _rwkv_prep(zm3, zr3, shift_prev, p)
    dup = LANES // (b * nh)
    assert dup * b * nh == LANES and hd % dup == 0
    vp = hd // dup

    def key_layout(x):
        y = x.reshape(b, t, nh, hd).transpose(1, 3, 0, 2)[:, :, :, None]
        return jnp.broadcast_to(y, (t, hd, b, dup, nh)).reshape(t, hd, LANES)

    def val_layout(x):
        return x.reshape(b, t, nh, dup, vp).transpose(1, 4, 0, 3, 2).reshape(t, vp, LANES)

    s0 = wkv0.reshape(b, nh, dup, vp, hd).transpose(4, 3, 0, 2, 1).reshape(hd, vp, LANES)
    if t % LANES == 0:
        o, s = _rwkv_scan_rows(r, w, kmod, kk, bv, v, s0, dup)
    else:
        o, s = _rwkv_scan(key_layout(r), key_layout(w), key_layout(kmod), key_layout(kk), key_layout(bv),
                          val_layout(v), s0)
        o = o.reshape(t, vp, b, dup, nh).transpose(2, 0, 4, 3, 1).reshape(b, t, B_W)
    s = s.reshape(hd, vp, b, dup, nh).transpose(2, 4, 3, 1, 0).reshape(b, nh, hd, hd)
    return o, bonus, g, s


def _gmlp_body(u_ref, v_ref, ws_ref, b_ref, o_ref, *, tc):
    keep = (lax.broadcasted_iota(jnp.int32, (CHUNK, CHUNK), 1) <= lax.broadcasted_iota(jnp.int32, (CHUNK, CHUNK), 0))
    for g in range(N_GROUPS_C):
        sl = slice(g * CHUNK, (g + 1) * CHUNK)
        w = jnp.where(keep, ws_ref[g], 0.0).astype(BF16)
        v = v_ref[0, :, sl]
        if tc < CHUNK:
            v = jnp.concatenate([v, jnp.zeros((CHUNK - tc, CHUNK), F32)], axis=0)
        s = jnp.dot(w, v.astype(BF16), preferred_element_type=F32) + b_ref[g]
        o_ref[0, :, sl] = u_ref[0, :, sl] * s[:tc]


def _gmlp(z3, w_s, b_s):
    b, t, _ = z3.shape
    tc = min(t, CHUNK)
    assert t % tc == 0
    blk = 3 * _nbytes((tc, C_W), F32) + 2 * _nbytes((N_GROUPS_C, CHUNK, CHUNK), F32)
    return pl.pallas_call(
        functools.partial(_gmlp_body, tc=tc), grid=(b, t // tc),
        in_specs=[pl.BlockSpec((1, tc, C_W), lambda bi, ci: (bi, ci, ZR_CU // C_W)),
                  pl.BlockSpec((1, tc, C_W), lambda bi, ci: (bi, ci, ZR_CV // C_W)),
                  pl.BlockSpec((N_GROUPS_C, CHUNK, CHUNK), lambda bi, ci: (0, 0, 0)),
                  pl.BlockSpec((N_GROUPS_C, CHUNK, 1), lambda bi, ci: (0, 0, 0))],
        out_specs=pl.BlockSpec((1, tc, C_W), lambda bi, ci: (bi, ci, 0)),
        out_shape=jax.ShapeDtypeStruct((b, t, C_W), F32),
        compiler_params=_params(("parallel", "parallel"), blk), name="gmlp")(z3, z3, w_s, b_s[:, :, None])


POOL_HALO = 16


def _pool_body(p_ref, pre_ref, w_ref, sc_ref, o_ref, ext, *, t_len, pos0, tc):
    ext[0:POOL_HALO, :] = pre_ref[0]
    ext[POOL_HALO:POOL_HALO + t_len, :] = p_ref[0]
    for c0 in range(0, t_len, tc):
        pos = pos0 + c0 + lax.broadcasted_iota(jnp.int32, (tc, 1), 0)
        for g, win in enumerate(POOL_WINDOWS):
            sl = slice(g * POOL_GROUP, (g + 1) * POOL_GROUP)
            base = POOL_HALO + c0
            acc = ext[base:base + tc, sl]
            for i in range(1, win):
                acc = acc + ext[base - i:base - i + tc, sl]
            cnt = jnp.minimum(win, pos + 1).astype(F32)
            pooled = acc / cnt - p_ref[0, c0:c0 + tc, sl]
            y = jnp.dot(pooled.astype(BF16), w_ref[g], preferred_element_type=F32)
            o_ref[0, c0:c0 + tc, sl] = y * sc_ref[:, sl]


def _pool(z3, prefix, pos0, w_pool, scale):
    b, t, _ = z3.shape
    assert POOL_PREV < POOL_HALO
    pre = jnp.pad(prefix, ((0, 0), (POOL_HALO - POOL_PREV, 0), (0, 0)))
    tc = min(t, 256)
    blk = 2 * _nbytes((t, D_W), F32) + _nbytes((POOL_HALO, D_W), F32) + _nbytes(w_pool.shape, BF16)
    scr = _nbytes((t + POOL_HALO, D_W), F32)
    return pl.pallas_call(
        functools.partial(_pool_body, t_len=t, pos0=pos0, tc=tc), grid=(b,),
        in_specs=[pl.BlockSpec((1, t, D_W), lambda bi: (bi, 0, ZR_D // D_W)),
                  pl.BlockSpec((1, POOL_HALO, D_W), lambda bi: (bi, 0, 0)),
                  pl.BlockSpec(w_pool.shape, lambda bi: (0, 0, 0)),
                  pl.BlockSpec((1, D_W), lambda bi: (0, 0))],
        out_specs=pl.BlockSpec((1, t, D_W), lambda bi: (bi, 0, 0)),
        out_shape=jax.ShapeDtypeStruct((b, t, D_W), F32),
        scratch_shapes=[pltpu.VMEM((t + POOL_HALO, D_W), F32)],
        compiler_params=_params(("parallel",), blk, scr), name="pool")(z3, pre, w_pool, scale.reshape(1, D_W))


def _mix_body(oa_ref, ob_ref, bonus_ref, g_ref, oc_ref, od_ref, ga_ref, lnw_ref, lnb_ref, gc_ref, gd_ref, ones_ref,
              o_ref):
    def rms(x, gain):
        ms = jnp.mean(x * x, axis=-1, keepdims=True)
        return x * lax.rsqrt(ms + RMS_EPS) * gain

    ones_bd = ones_ref[...]
    o = ob_ref[...]
    mean = _group_sum(o, ones_bd) * (1.0 / HEAD_DIM_B)
    d = o - mean
    var = _group_sum(d * d, ones_bd) * (1.0 / HEAD_DIM_B)
    ob = d * lax.rsqrt(var + GN_EPS) * lnw_ref[...] + lnb_ref[...]
    ob = (ob + bonus_ref[...]) * g_ref[...]
    o_ref[:, 0:A_W] = rms(oa_ref[...], ga_ref[...]).astype(BF16)
    o_ref[:, A_W:A_W + B_W] = ob.astype(BF16)
    o_ref[:, A_W + B_W:A_W + B_W + C_W] = rms(oc_ref[...], gc_ref[...]).astype(BF16)
    o_ref[:, A_W + B_W + C_W:] = rms(od_ref[...], gd_ref[...]).astype(BF16)


def _mix(oa, ob, bonus, g, oc, od, ga, lnw, lnb, gc, gd, ones_bd):
    m = oa.shape[0]
    tm = min(m, 256)
    act = pl.BlockSpec((tm, A_W), lambda i: (i, 0))
    row = pl.BlockSpec((1, A_W), lambda i: (0, 0))
    blk = 6 * _nbytes((tm, A_W), F32) + _nbytes((tm, D_MODEL), BF16) + _nbytes((512, A_W), F32)
    r1 = lambda a: a.reshape(1, -1)
    return pl.pallas_call(
        _mix_body, grid=(m // tm,),
        in_specs=[act] * 6 + [row] * 5 + [pl.BlockSpec((LANES, LANES), lambda i: (0, 0))],
        out_specs=pl.BlockSpec((tm, D_MODEL), lambda i: (i, 0)),
        out_shape=jax.ShapeDtypeStruct((m, D_MODEL), BF16),
        compiler_params=_params(("parallel",), blk), name="mix")(
            oa, ob, bonus, g, oc, od, r1(ga), r1(lnw), r1(lnb), r1(gc), r1(gd), ones_bd)


def _prep_rest_weights(w_in):
    assert ZM_W == OFF_B + 3 * B_W
    zpad = jnp.zeros((DEPTH, D_MODEL, ZR_W - ZR_LORA - LORA_ALL), w_in.dtype)
    return jnp.concatenate([w_in[..., OFF_C:], w_in[..., ZM_W:OFF_C], zpad], axis=-1).astype(BF16)


def _prep_layer_weights(l, w):
    mu = w['mu_b'][l]
    pad_rows = lambda a, before, total: jnp.pad(a, ((before, total - before - a.shape[0]), (0, 0))).astype(BF16)
    ones_bd = jnp.kron(jnp.eye(LANES // HEAD_DIM_B, dtype=F32), jnp.ones((HEAD_DIM_B, HEAD_DIM_B), F32))
    rw = dict(
        mu_x=mu[None, :3 * B_W], mu_lo=jnp.pad(mu[None, 3 * B_W:], ((0, 0), (0, LORA_PAD - LORA_ALL))),
        w0=w['w0'][l][None], a0=w['a0'][l][None], k_k=w['k_k'][l][None], k_a=w['k_a'][l][None],
        r_k=w['r_k'][l].reshape(1, B_W),
        w_up=pad_rows(w['w_up'][l], 0, LANES), a_up=pad_rows(w['a_up'][l], LORA_W, LANES),
        g_up=pad_rows(w['g_up'][l], 0, LORA_PAD - LANES), ones_bd=ones_bd)
    return dict(w_pool=w['w_pool'][l].astype(BF16), rwkv=rw)


def _layer(x, l, w, w_rest, wl, wb, pos0, kv_prefix, shift_prev, wkv0, pool_prefix):
    cast = wb is None
    wb = {} if cast else wb
    b, t, _ = x.shape
    m = b * t

    def up(xn, gate, upw):
        if not cast:
            return _ffn_up(xn, wb[gate], wb[upw])
        hid, wb[gate], wb[upw] = _ffn_up_cast(xn, w[gate], w[upw], l)
        return hid

    def mm(a, name, n, tm, res=None, scale=1.0):
        if not cast:
            return _matmul(a, wb[name], res=res, scale=scale, tm=tm)
        out, wb[name] = _matmul_cast(a, w[name], l, n, res=res, scale=scale)
        return out

    x2 = x.reshape(m, D_MODEL)
    hid = up(_rmsnorm(x2, w['ln_ffn1'][l], BF16), 'w1_gate', 'w1_up')
    h = mm(hid, 'w1_down', D_MODEL, 512, res=x2, scale=0.5)
    hn = _rmsnorm(h, w['ln_mix'][l], BF16)
    zm3 = mm(hn, 'w_in', ZM_W, 1024).reshape(b, t, ZM_W)
    zr3 = _matmul(hn, w_rest, l).reshape(b, t, ZR_W)
    out_a, ka = _attention(zm3, pos0, kv_prefix, l)
    o_b, bonus, gate, wkv_new = _rwkv_mix(zm3, zr3, shift_prev, wkv0, wl['rwkv'])
    out_c = _gmlp(zr3, w['w_s'][l], w['b_s'][l])
    out_d = _pool(zr3, pool_prefix, pos0, wl['w_pool'], w['pool_scale'][l])
    r2 = lambda a: a.reshape(m, -1)
    mix = _mix(r2(out_a), r2(o_b), r2(bonus), r2(gate), r2(out_c), r2(out_d), w['g_out_a'][l], w['ln_x_w'][l],
               w['ln_x_b'][l], w['g_out_c'][l], w['g_out_d'][l], wl['rwkv']['ones_bd'])
    h = mm(mix, 'w_out', D_MODEL, 1024, res=h, scale=1.0)
    hid = up(_rmsnorm(h, w['ln_ffn2'][l], BF16), 'w2_gate', 'w2_up')
    y = mm(hid, 'w2_down', D_MODEL, 512, res=h, scale=0.5)
    hs = (b, t, N_HEADS_A, HEAD_DIM_A)
    va = zm3[..., ZM_VA:ZM_VA + A_W]
    shift_new = jnp.concatenate([zm3[:, -1, ZM_RKV:], zr3[:, -1, ZR_LORA:ZR_LORA + LORA_ALL]], axis=-1)
    p = zr3[..., ZR_D:ZR_D + D_W]
    pool_new = jnp.concatenate([pool_prefix, p], axis=1)[:, -POOL_PREV:]
    vc = zr3[..., ZR_CV:ZR_CV + C_W]
    return y.reshape(b, t, D_MODEL), (ka.reshape(hs), va.reshape(hs), wkv_new, shift_new, pool_new, vc), wb


def kernel(x_prompt, x_sample, cache_k_swa, cache_v_swa, state_rwkv_wkv, state_rwkv_shift, state_pool, ln_ffn1, w1_gate, w1_up, w1_down, ln_mix, w_in, g_out_a, mu_b, w0, w_up, a0, a_up, g_up, k_k, k_a, r_k, ln_x_w, ln_x_b, w_s, b_s, g_out_c, w_pool, pool_scale, g_out_d, w_out, ln_ffn2, w2_gate, w2_up, w2_down, ln_final):
    w = dict(ln_ffn1=ln_ffn1, w1_gate=w1_gate, w1_up=w1_up, w1_down=w1_down, ln_mix=ln_mix, w_in=w_in,
             g_out_a=g_out_a, mu_b=mu_b, w0=w0, w_up=w_up, a0=a0, a_up=a_up, g_up=g_up, k_k=k_k, k_a=k_a,
             r_k=r_k, ln_x_w=ln_x_w, ln_x_b=ln_x_b, w_s=w_s, b_s=b_s, g_out_c=g_out_c, w_pool=w_pool,
             pool_scale=pool_scale, g_out_d=g_out_d, w_out=w_out, ln_ffn2=ln_ffn2, w2_gate=w2_gate,
             w2_up=w2_up, w2_down=w2_down)
    nbp, t_prompt, _ = x_prompt.shape
    yp, ys = x_prompt, x_sample
    p_states, s_states = [], []
    w_rest = _prep_rest_weights(w_in)
    for l in range(DEPTH):
        wl = _prep_layer_weights(l, w)
        ys, ss, wb = _layer(ys, l, w, w_rest, wl, None, PAST_LEN, (cache_k_swa, cache_v_swa),
                            state_rwkv_shift[l], state_rwkv_wkv[l], state_pool[l])
        yp, sp, _ = _layer(yp, l, w, w_rest, wl, wb, 0, None,
                           jnp.zeros((nbp, B_FEAT), F32),
                           jnp.zeros((nbp, N_HEADS_B, HEAD_DIM_B, HEAD_DIM_B), F32),
                           jnp.zeros((nbp, POOL_PREV, D_W), F32))
        p_states.append(sp)
        s_states.append(ss)
    keep = min(WIN_MAX, t_prompt)
    stack = lambda states, i: jnp.stack([s[i] for s in states])
    y_prompt = _rmsnorm(yp.reshape(-1, D_MODEL), ln_final, F32).reshape(yp.shape)
    y_sample = _rmsnorm(ys.reshape(-1, D_MODEL), ln_final, F32).reshape(ys.shape)
    return (y_prompt, y_sample,
            jnp.stack([s[0][:, -keep:] for s in p_states]), jnp.stack([s[1][:, -keep:] for s in p_states]),
            stack(p_states, 2), stack(p_states, 3), stack(p_states, 4),
            stack(s_states, 0), stack(s_states, 1), stack(s_states, 2), stack(s_states, 3), stack(s_states, 4),
            stack(s_states, 5))
```

```python
import functools

import jax
import jax.numpy as jnp
from jax import lax
from jax.experimental import pallas as pl
from jax.experimental.pallas import tpu as pltpu

F32 = jnp.float32
BF16 = jnp.bfloat16

D_MODEL = 4096
DEPTH = 2
PAST_LEN = 16384
A_W = B_W = C_W = D_W = D_MODEL // 4
RMS_EPS = 1e-6
HEAD_DIM_A = 128
N_HEADS_A = A_W // HEAD_DIM_A
DILATED_BRANCHES = ((128, 1), (512, 4), (2048, 16))
WIN_MAX = 2048
ROPE_THETA = 10000.0
HEAD_DIM_B = 64
N_HEADS_B = B_W // HEAD_DIM_B
LORA_W, LORA_A, LORA_G = 64, 64, 160
LORA_ALL = LORA_W + LORA_A + LORA_G
GN_EPS = 64e-5
B_FEAT = 3 * B_W + LORA_ALL
CHUNK = 128
N_GROUPS_C = 8
POOL_WINDOWS = (2, 4, 8, 16)
POOL_PREV = max(POOL_WINDOWS) - 1
POOL_GROUP = D_W // len(POOL_WINDOWS)
OFF_B = 3 * A_W
OFF_C = OFF_B + B_FEAT
OFF_D = OFF_C + 2 * C_W

LANES = 128
SUBLANES = 8
VMEM_BYTES_V7X = 64 * 2**20
VMEM_INTERNAL_RESERVE = 12 * 2**20

MXU_COLS_V7X = 256
MM_TN = 2 * MXU_COLS_V7X

ZM_QA, ZM_KA, ZM_VA = 0, A_W, 2 * A_W
ZM_RKV = 3 * A_W
ZM_W = ZM_RKV + 3 * B_W
ZR_CU = 0
ZR_CV = ZR_CU + C_W
ZR_D = ZR_CV + C_W
ZR_LORA = ZR_D + D_W
LORA_PAD = 3 * LANES
ZR_W = -(-(ZR_LORA + LORA_PAD) // MM_TN) * MM_TN

NEG = -1e30


def _vmem_limit(block_bytes, scratch_bytes=0):
    need = 2 * block_bytes + scratch_bytes + VMEM_INTERNAL_RESERVE
    return int(min(max(need, 16 * 2**20), VMEM_BYTES_V7X - 4 * 2**20))


def _params(sem, block_bytes, scratch_bytes=0):
    return pltpu.CompilerParams(dimension_semantics=sem,
                                vmem_limit_bytes=_vmem_limit(block_bytes, scratch_bytes))


def _nbytes(shape, dtype):
    n = 1
    for s in shape:
        n *= s
    return n * jnp.dtype(dtype).itemsize


def _rmsnorm_body(x_ref, g_ref, o_ref):
    x = x_ref[...]
    ms = jnp.mean(x * x, axis=-1, keepdims=True)
    o_ref[...] = (x * lax.rsqrt(ms + RMS_EPS) * g_ref[...]).astype(o_ref.dtype)


def _rmsnorm(x, g, out_dtype):
    m, d = x.shape
    tm = min(m, 256)
    blk = _nbytes((tm, d), F32) + _nbytes((tm, d), out_dtype)
    return pl.pallas_call(
        _rmsnorm_body, grid=(m // tm,),
        in_specs=[pl.BlockSpec((tm, d), lambda i: (i, 0)), pl.BlockSpec((1, d), lambda i: (0, 0))],
        out_specs=pl.BlockSpec((tm, d), lambda i: (i, 0)),
        out_shape=jax.ShapeDtypeStruct((m, d), out_dtype),
        compiler_params=_params(("parallel",), blk), name="rmsnorm")(x, g.reshape(1, d))


def _ffn_up_body(x_ref, wg_ref, wu_ref, o_ref):
    x = x_ref[...]
    g = jnp.dot(x, wg_ref[...], preferred_element_type=F32)
    u = jnp.dot(x, wu_ref[...], preferred_element_type=F32)
    o_ref[...] = (g * jax.nn.sigmoid(g) * u).astype(o_ref.dtype)


def _ffn_up_cast_body(x_ref, wg_ref, wu_ref, o_ref, wgo_ref, wuo_ref):
    wg = wg_ref[...].astype(BF16)
    wu = wu_ref[...].astype(BF16)
    wgo_ref[...] = wg
    wuo_ref[...] = wu
    x = x_ref[...]
    g = jnp.dot(x, wg, preferred_element_type=F32)
    u = jnp.dot(x, wu, preferred_element_type=F32)
    o_ref[...] = (g * jax.nn.sigmoid(g) * u).astype(o_ref.dtype)


def _ffn_up(x, wg, wu):
    m, k = x.shape
    n = wg.shape[1]
    tm = min(m, 1024)
    tn = MXU_COLS_V7X
    blk = _nbytes((tm, k), BF16) + 2 * _nbytes((k, tn), BF16) + _nbytes((tm, tn), BF16)
    return pl.pallas_call(
        _ffn_up_body, grid=(m // tm, n // tn),
        in_specs=[pl.BlockSpec((tm, k), lambda i, j: (i, 0)),
                  pl.BlockSpec((k, tn), lambda i, j: (0, j)),
                  pl.BlockSpec((k, tn), lambda i, j: (0, j))],
        out_specs=pl.BlockSpec((tm, tn), lambda i, j: (i, j)),
        out_shape=jax.ShapeDtypeStruct((m, n), BF16),
        compiler_params=_params(("parallel", "arbitrary"), blk), name="ffn_up")(x, wg, wu)


def _ffn_up_cast(x, wg, wu, l):
    m, k = x.shape
    n = wg.shape[2]
    tn = MXU_COLS_V7X
    blk = _nbytes((m, k), BF16) + 2 * _nbytes((k, tn), F32) + 2 * _nbytes((k, tn), BF16) + _nbytes((m, tn), BF16)
    wspec = pl.BlockSpec((None, k, tn), lambda j: (l, 0, j))
    ospec = pl.BlockSpec((k, tn), lambda j: (0, j))
    return pl.pallas_call(
        _ffn_up_cast_body, grid=(n // tn,),
        in_specs=[pl.BlockSpec((m, k), lambda j: (0, 0)), wspec, wspec],
        out_specs=[pl.BlockSpec((m, tn), lambda j: (0, j)), ospec, ospec],
        out_shape=[jax.ShapeDtypeStruct((m, n), BF16), jax.ShapeDtypeStruct((k, n), BF16),
                   jax.ShapeDtypeStruct((k, n), BF16)],
        compiler_params=_params(("parallel",), blk), name="ffn_up_cast")(x, wg, wu)


def _mm_body(a_ref, b_ref, o_ref):
    o_ref[...] = jnp.dot(a_ref[...], b_ref[...], preferred_element_type=F32)


def _mm_res_body(a_ref, b_ref, r_ref, o_ref, *, scale):
    acc = jnp.dot(a_ref[...], b_ref[...], preferred_element_type=F32)
    o_ref[...] = r_ref[...] + scale * acc


_TRANS_B = (((1,), (1,)), ((), ()))


def _mm_t_body(a_ref, bt_ref, o_ref):
    o_ref[...] = lax.dot_general(a_ref[...], bt_ref[...], _TRANS_B, preferred_element_type=F32)


def _mm_cast_body(a_ref, b_ref, *rest, scale, has_res, trans_b):
    r_ref = rest[0] if has_res else None
    o_ref, bo_ref = rest[-2:]
    b = b_ref[...].astype(BF16)
    bo_ref[...] = b
    if trans_b:
        acc = lax.dot_general(a_ref[...], b, _TRANS_B, preferred_element_type=F32)
    else:
        acc = jnp.dot(a_ref[...], b, preferred_element_type=F32)
    o_ref[...] = r_ref[...] + scale * acc if has_res else acc


def _matmul(a, b, l=None, res=None, scale=1.0, tm=1024, trans_b=False):
    m, k = a.shape
    n = b.shape[-2] if trans_b else b.shape[-1]
    tm = min(m, tm)
    tn = MM_TN
    assert m % tm == 0 and n % tn == 0
    blk = _nbytes((tm, k), BF16) + _nbytes((k, tn), BF16) + _nbytes((tm, tn), F32)
    bshape, bidx = ((tn, k), lambda i, j: (j, 0)) if trans_b else ((k, tn), lambda i, j: (0, j))
    bspec = (pl.BlockSpec(bshape, bidx) if b.ndim == 2
             else pl.BlockSpec((None,) + bshape, lambda i, j: (l,) + bidx(i, j)))
    in_specs = [pl.BlockSpec((tm, k), lambda i, j: (i, 0)), bspec]
    args = [a, b]
    if trans_b:
        assert res is None
        body = _mm_t_body
    elif res is None:
        body = _mm_body
    else:
        body = functools.partial(_mm_res_body, scale=scale)
        in_specs.append(pl.BlockSpec((tm, tn), lambda i, j: (i, j)))
        args.append(res)
        blk += _nbytes((tm, tn), F32)
    return pl.pallas_call(
        body, grid=(m // tm, n // tn), in_specs=in_specs,
        out_specs=pl.BlockSpec((tm, tn), lambda i, j: (i, j)),
        out_shape=jax.ShapeDtypeStruct((m, n), F32),
        compiler_params=_params(("parallel", "arbitrary"), blk), name="matmul")(*args)


CAST_TILE_BYTES = 6 * 2**20


def _matmul_cast(a, b, l, n, res=None, scale=1.0, trans_b=False):
    m, k = a.shape
    tn = MM_TN
    while _nbytes((k, tn), F32) > CAST_TILE_BYTES:
        tn //= 2
    assert n % tn == 0 and tn % LANES == 0
    blk = _nbytes((m, k), BF16) + _nbytes((k, tn), F32) + _nbytes((k, tn), BF16) + 2 * _nbytes((m, tn), F32)
    if trans_b:
        wspec = pl.BlockSpec((None, tn, k), lambda j: (l, j, 0))
        cspec, cshape = pl.BlockSpec((tn, k), lambda j: (j, 0)), (n, k)
    else:
        wspec = pl.BlockSpec((None, k, tn), lambda j: (l, 0, j))
        cspec, cshape = pl.BlockSpec((k, tn), lambda j: (0, j)), (k, n)
    in_specs = [pl.BlockSpec((m, k), lambda j: (0, 0)), wspec]
    args = [a, b]
    if res is not None:
        in_specs.append(pl.BlockSpec((m, tn), lambda j: (0, j)))
        args.append(res)
    return pl.pallas_call(
        functools.partial(_mm_cast_body, scale=scale, has_res=res is not None, trans_b=trans_b), grid=(n // tn,),
        in_specs=in_specs, out_specs=[pl.BlockSpec((m, tn), lambda j: (0, j)), cspec],
        out_shape=[jax.ShapeDtypeStruct((m, n), F32), jax.ShapeDtypeStruct(cshape, BF16)],
        compiler_params=_params(("parallel",), blk), name="matmul_cast")(*args)


def _rope(x, cos, sin_signed):
    return x * cos + pltpu.roll(x, HEAD_DIM_A // 2, 1) * sin_signed


def _branch_multiplicity(delta):
    c = jnp.zeros(delta.shape, F32)
    for window, dilation in DILATED_BRANCHES:
        assert dilation & (dilation - 1) == 0
        hit = jnp.where(delta <= window, 1.0, 0.0)
        if dilation > 1:
            hit = jnp.where((delta & (dilation - 1)) == 0, hit, 0.0)
        c = c + hit
    return jnp.where(delta >= 0, c, 0.0)


def _attn_prompt_body(q_ref, k_ref, v_ref, cos_ref, sin_ref, o_ref, kout_ref, qs, ks, vs, ctab, *, t_len, tq):
    nq = t_len // tq

    @pl.when((pl.program_id(0) == 0) & (pl.program_id(1) == 0))
    def _():
        rel = (lax.broadcasted_iota(jnp.int32, (tq, tq), 0) - lax.broadcasted_iota(jnp.int32, (tq, tq), 1))
        for d in range(nq):
            ctab[:, d * tq:(d + 1) * tq] = _branch_multiplicity(rel + (nq - 1 - d) * tq)

    cos = cos_ref[...]
    sin = sin_ref[...]
    k = _rope(k_ref[0], cos, sin)
    kout_ref[0] = k
    ks[...] = k.astype(BF16)
    qs[...] = (_rope(q_ref[0], cos, sin) * (HEAD_DIM_A ** -0.5)).astype(BF16)
    vs[...] = v_ref[0].astype(BF16)
    for i in range(nq):
        kw = (i + 1) * tq
        s = lax.dot_general(qs[i * tq:(i + 1) * tq, :], ks[0:kw, :], _TRANS_B, preferred_element_type=F32)
        c = ctab[:, (nq - 1 - i) * tq:]
        sm = jnp.where(c > 0.0, s, NEG)
        p = jnp.exp(sm - jnp.max(sm, axis=-1, keepdims=True)) * c
        l = jnp.sum(p, axis=-1, keepdims=True)
        acc = jnp.dot(p.astype(BF16), vs[0:kw, :], preferred_element_type=F32)
        o_ref[0, i * tq:(i + 1) * tq, :] = acc / l


def _attn_sample_body(q_ref, k_ref, v_ref, kp_ref, vp_ref, cos_ref, sin_ref, o_ref, kout_ref, *, t_len, n_prev):
    cos = cos_ref[...]
    sin = sin_ref[...]
    d1 = (n_prev + lax.broadcasted_iota(jnp.int32, (t_len, n_prev), 0)
          - lax.broadcasted_iota(jnp.int32, (t_len, n_prev), 1))
    c1 = _branch_multiplicity(d1)
    d2 = (lax.broadcasted_iota(jnp.int32, (t_len, LANES), 0) - lax.broadcasted_iota(jnp.int32, (t_len, LANES), 1))
    c2 = _branch_multiplicity(d2)
    pad = jnp.zeros((LANES - t_len, HEAD_DIM_A), F32)
    for h in range(N_HEADS_A):
        sl = slice(h * HEAD_DIM_A, (h + 1) * HEAD_DIM_A)
        k = _rope(k_ref[0, :, sl], cos, sin)
        kout_ref[0, :, sl] = k
        q = (_rope(q_ref[0, :, sl], cos, sin) * (HEAD_DIM_A ** -0.5)).astype(BF16)
        kn = jnp.concatenate([k, pad], axis=0).astype(BF16)
        vn = jnp.concatenate([v_ref[0, :, sl], pad], axis=0).astype(BF16)
        s1 = lax.dot_general(q, kp_ref[:, h, :].astype(BF16), _TRANS_B, preferred_element_type=F32)
        s2 = lax.dot_general(q, kn, _TRANS_B, preferred_element_type=F32)
        sm1 = jnp.where(c1 > 0.0, s1, NEG)
        sm2 = jnp.where(c2 > 0.0, s2, NEG)
        m = jnp.maximum(jnp.max(sm1, axis=-1, keepdims=True), jnp.max(sm2, axis=-1, keepdims=True))
        p1 = jnp.exp(sm1 - m) * c1
        p2 = jnp.exp(sm2 - m) * c2
        l = jnp.sum(p1, axis=-1, keepdims=True) + jnp.sum(p2, axis=-1, keepdims=True)
        acc = (jnp.dot(p1.astype(BF16), vp_ref[:, h, :].astype(BF16), preferred_element_type=F32)
               + jnp.dot(p2.astype(BF16), vn, preferred_element_type=F32))
        o_ref[0, :, sl] = acc / l


def _rope_tables(pos0, t_len):
    half = HEAD_DIM_A // 2
    inv = ROPE_THETA ** (-jnp.arange(half, dtype=F32) / half)
    ang = (pos0 + jnp.arange(t_len)).astype(F32)[:, None] * inv[None, :]
    cos = jnp.cos(ang)
    sin = jnp.sin(ang)
    return jnp.concatenate([cos, cos], axis=-1), jnp.concatenate([-sin, sin], axis=-1)


def _attention(z3, pos0, kv_prefix, l):
    b, t, _ = z3.shape
    cos, sin = _rope_tables(pos0, t)
    hd = HEAD_DIM_A
    out_shape = [jax.ShapeDtypeStruct((b, t, A_W), F32), jax.ShapeDtypeStruct((b, t, A_W), F32)]
    if kv_prefix is None:
        col = lambda base: (lambda bi, hi: (bi, 0, base // hd + hi))
        tab = pl.BlockSpec((t, hd), lambda bi, hi: (0, 0))
        zspecs = [pl.BlockSpec((1, t, hd), col(ZM_QA)), pl.BlockSpec((1, t, hd), col(ZM_KA)),
                  pl.BlockSpec((1, t, hd), col(ZM_VA))]
        out_specs = [pl.BlockSpec((1, t, hd), col(0)), pl.BlockSpec((1, t, hd), col(0))]
        blk = 7 * _nbytes((t, hd), F32)
        tq = min(t, 256)
        body = functools.partial(_attn_prompt_body, t_len=t, tq=tq)
        scratch = 3 * _nbytes((t, hd), BF16) + 5 * _nbytes((tq, t), F32)
        return pl.pallas_call(
            body, grid=(b, N_HEADS_A), in_specs=zspecs + [tab, tab], out_specs=out_specs, out_shape=out_shape,
            scratch_shapes=[pltpu.VMEM((t, hd), BF16)] * 3 + [pltpu.VMEM((tq, t), F32)],
            compiler_params=_params(("arbitrary", "arbitrary"), blk, scratch),
            name="attn_prompt")(z3, z3, z3, cos, sin)
    k_prev, v_prev = kv_prefix
    n_prev = k_prev.shape[2]
    assert t <= LANES and n_prev % LANES == 0 and k_prev.shape[3:] == (N_HEADS_A, hd)
    col = lambda base: (lambda bi: (bi, 0, base // A_W))
    tab = pl.BlockSpec((t, hd), lambda bi: (0, 0))
    zspecs = [pl.BlockSpec((1, t, A_W), col(ZM_QA)), pl.BlockSpec((1, t, A_W), col(ZM_KA)),
              pl.BlockSpec((1, t, A_W), col(ZM_VA))]
    out_specs = [pl.BlockSpec((1, t, A_W), col(0)), pl.BlockSpec((1, t, A_W), col(0))]
    pspec = pl.BlockSpec((None, None, n_prev, N_HEADS_A, hd), lambda bi: (l, bi, 0, 0, 0))
    blk = 5 * _nbytes((t, A_W), F32) + 2 * _nbytes((n_prev, A_W), F32)
    body = functools.partial(_attn_sample_body, t_len=t, n_prev=n_prev)
    return pl.pallas_call(
        body, grid=(b,), in_specs=zspecs + [pspec, pspec, tab, tab], out_specs=out_specs,
        out_shape=out_shape, compiler_params=_params(("parallel",), blk),
        name="attn_sample")(z3, z3, z3, k_prev, v_prev, cos, sin)


def _group_sum(x, ones_blockdiag):
    outs = []
    for j in range(x.shape[-1] // LANES):
        outs.append(jnp.dot(x[:, j * LANES:(j + 1) * LANES], ones_blockdiag, preferred_element_type=F32,
                            precision=lax.Precision.HIGHEST))
    return jnp.concatenate(outs, axis=-1)


def _softplus(y):
    return jnp.maximum(y, 0.0) + jnp.log1p(jnp.exp(-jnp.abs(y)))


def _rwkv_prep_body(x_ref, lo_ref, sx_ref, slo_ref, mux_ref, mulo_ref, w0_ref, a0_ref, kkp_ref, kap_ref, rk_ref,
                    wup_ref, aup_ref, gup_ref, ones_ref,
                    r_o, w_o, k_o, v_o, kk_o, b_o, g_o, bonus_o, last_x, last_lo, *, tt):
    @pl.when(pl.program_id(1) == 0)
    def _():
        last_x[0:1, :] = sx_ref[0]
        last_lo[0:1, :] = slo_ref[0]

    x = x_ref[0]
    lo = lo_ref[0]
    first = lax.broadcasted_iota(jnp.int32, (tt, 1), 0) == 0
    px = jnp.where(first, last_x[0:1, :], pltpu.roll(x, 1, 0))
    plo = jnp.where(first, last_lo[0:1, :], pltpu.roll(lo, 1, 0))
    last_x[0:1, :] = x[tt - 1:tt, :]
    last_lo[0:1, :] = lo[tt - 1:tt, :]
    fx = x + mux_ref[...] * (px - x)
    flo = lo + mulo_ref[...] * (plo - lo)
    r = fx[:, :B_W]
    k = fx[:, B_W:2 * B_W]
    v = fx[:, 2 * B_W:]
    zwa = flo[:, :LANES]
    zg = flo[:, LANES:]
    ones_bd = ones_ref[...]
    wl = w0_ref[...] + jnp.dot(jnp.tanh(zwa).astype(BF16), wup_ref[...], preferred_element_type=F32)
    w_log = -_softplus(-wl) - 0.5
    decay = jnp.exp(-jnp.exp(w_log))
    a = jax.nn.sigmoid(a0_ref[...] + jnp.dot(zwa.astype(BF16), aup_ref[...], preferred_element_type=F32))
    g = jnp.dot(jax.nn.sigmoid(zg).astype(BF16), gup_ref[...], preferred_element_type=F32)
    kk = k * kkp_ref[...]
    kk = kk / jnp.maximum(jnp.sqrt(_group_sum(kk * kk, ones_bd)), 1e-12)
    kmod = k * (1.0 + (a - 1.0) * kap_ref[...])
    r_o[0] = r
    w_o[0] = decay
    k_o[0] = kmod
    v_o[0] = v
    kk_o[0] = kk
    b_o[0] = kk * a
    g_o[0] = g
    bonus_o[0] = _group_sum(r * kmod * rk_ref[...], ones_bd) * v


def _rwkv_prep(zm3, zr3, shift_prev, p):
    b, t, _ = zm3.shape
    tt = min(t, 256)
    sx = shift_prev[:, None, :3 * B_W]
    slo = jnp.pad(shift_prev[:, None, 3 * B_W:], ((0, 0), (0, 0), (0, LORA_PAD - LORA_ALL)))
    row = lambda w: pl.BlockSpec((1, w), lambda bi, ti: (0, 0))
    full = lambda a: pl.BlockSpec(a.shape, lambda bi, ti: (0,) * a.ndim)
    in_specs = [pl.BlockSpec((1, tt, 3 * B_W), lambda bi, ti: (bi, ti, ZM_RKV // (3 * B_W))),
                pl.BlockSpec((1, tt, LORA_PAD), lambda bi, ti: (bi, ti, ZR_LORA // LORA_PAD)),
                pl.BlockSpec((1, 1, 3 * B_W), lambda bi, ti: (bi, 0, 0)),
                pl.BlockSpec((1, 1, LORA_PAD), lambda bi, ti: (bi, 0, 0)),
                row(3 * B_W), row(LORA_PAD), row(B_W), row(B_W), row(B_W), row(B_W), row(B_W),
                full(p['w_up']), full(p['a_up']), full(p['g_up']), full(p['ones_bd'])]
    ospec = pl.BlockSpec((1, tt, B_W), lambda bi, ti: (bi, ti, 0))
    blk = _nbytes((tt, 3 * B_W + LORA_PAD), F32) + 8 * _nbytes((tt, B_W), F32) + 2 * _nbytes((512, B_W), F32)
    return pl.pallas_call(
        functools.partial(_rwkv_prep_body, tt=tt), grid=(b, t // tt), in_specs=in_specs,
        out_specs=[ospec] * 8, out_shape=[jax.ShapeDtypeStruct((b, t, B_W), F32)] * 8,
        scratch_shapes=[pltpu.VMEM((SUBLANES, 3 * B_W), F32), pltpu.VMEM((SUBLANES, LORA_PAD), F32)],
        compiler_params=_params(("parallel", "arbitrary"), blk), name="rwkv_prep")(
            zm3, zr3, sx, slo, p['mu_x'], p['mu_lo'], p['w0'], p['a0'], p['k_k'], p['k_a'], p['r_k'],
            p['w_up'], p['a_up'], p['g_up'], p['ones_bd'])


N_PARTIAL = 4


def _rwkv_steps(r_ref, w_ref, k_ref, kk_ref, b_ref, v_ref, o_ref, state, *, tc, vp, time_major):
    def step(t, row):
        parts = [jnp.zeros((vp, LANES), F32)] * N_PARTIAL
        for k in range(HEAD_DIM_B):
            parts[k % N_PARTIAL] = parts[k % N_PARTIAL] + state[k] * row(kk_ref, k)
        sa = -((parts[0] + parts[1]) + (parts[2] + parts[3]))
        vt = v_ref[t]
        parts = [jnp.zeros((vp, LANES), F32)] * N_PARTIAL
        for k in range(HEAD_DIM_B):
            s = state[k] * row(w_ref, k) + sa * row(b_ref, k) + vt * row(k_ref, k)
            state[k] = s
            parts[k % N_PARTIAL] = parts[k % N_PARTIAL] + s * row(r_ref, k)
        o_ref[t] = (parts[0] + parts[1]) + (parts[2] + parts[3])

    if time_major:
        def one(t, carry):
            step(t, lambda ref, k: ref[t, pl.ds(k, 1), :])
            return carry

        lax.fori_loop(0, tc, one, 0)
    else:
        def eight(tb, carry):
            for s in range(SUBLANES):
                step(tb * SUBLANES + s, lambda ref, k, s=s: ref[k, tb, pl.ds(s, 1), :])
            return carry

        lax.fori_loop(0, tc // SUBLANES, eight, 0)


def _rwkv_scan_body(r_ref, w_ref, k_ref, kk_ref, b_ref, v_ref, s0_ref, o_ref, sout_ref, state, *, tc, vp):
    @pl.when(pl.program_id(0) == 0)
    def _():
        state[...] = s0_ref[...]

    _rwkv_steps(r_ref, w_ref, k_ref, kk_ref, b_ref, v_ref, o_ref, state, tc=tc, vp=vp, time_major=True)

    @pl.when(pl.program_id(0) == pl.num_programs(0) - 1)
    def _():
        sout_ref[...] = state[...]


def _rwkv_scan_rows_body(r_ref, w_ref, k_ref, kk_ref, b_ref, v_ref, s0_ref, o_ref, sout_ref,
                         state, yt, rk, wk, kkey, kkk, bk, vk, ok, *, tc, nb, dup):
    vp = HEAD_DIM_B // dup
    nh = N_HEADS_B

    @pl.when(pl.program_id(0) == 0)
    def _():
        state[...] = s0_ref[...]

    def to_lanes(x_ref, store, n_rows, row_of):
        for b in range(nb):
            yt[b] = x_ref[b].T
        for j in range(n_rows):
            pieces = [yt[b, pl.ds(row_of(j, vh), nh, stride=HEAD_DIM_B), :] for b in range(nb) for vh in range(dup)]
            store(j, jnp.concatenate(pieces, axis=0).T)

    def key_store(dst):
        def store(j, x):
            dst[j] = x.reshape(tc // SUBLANES, SUBLANES, LANES)
        return store

    def val_store(j, x):
        vk[:, j, :] = x

    for x_ref, dst in ((r_ref, rk), (w_ref, wk), (k_ref, kkey), (kk_ref, kkk), (b_ref, bk)):
        to_lanes(x_ref, key_store(dst), HEAD_DIM_B, lambda j, vh: j)
    to_lanes(v_ref, val_store, vp, lambda j, vh: vh * vp + j)

    _rwkv_steps(rk, wk, kkey, kkk, bk, vk, ok, state, tc=tc, vp=vp, time_major=False)

    for j in range(vp):
        m = ok[:, j, :].T
        for b in range(nb):
            for vh in range(dup):
                lane0 = (b * dup + vh) * nh
                yt[b, pl.ds(vh * vp + j, nh, stride=HEAD_DIM_B), :] = m[lane0:lane0 + nh, :]
    for b in range(nb):
        o_ref[b] = yt[b].T

    @pl.when(pl.program_id(0) == pl.num_programs(0) - 1)
    def _():
        sout_ref[...] = state[...]


def _rwkv_scan_rows(r, w, k, kk, bv, v, s0, dup):
    nb, t, _ = r.shape
    vp = HEAD_DIM_B // dup
    tc = LANES
    assert t % tc == 0 and nb * dup * N_HEADS_B == LANES
    xspec = pl.BlockSpec((nb, tc, B_W), lambda i: (0, i, 0))
    sspec = pl.BlockSpec((HEAD_DIM_B, vp, LANES), lambda i: (0, 0, 0))
    key_tile = pltpu.VMEM((HEAD_DIM_B, tc // SUBLANES, SUBLANES, LANES), F32)
    val_tile = pltpu.VMEM((tc, vp, LANES), F32)
    blk = 7 * _nbytes((nb, tc, B_W), F32) + 2 * _nbytes((HEAD_DIM_B, vp, LANES), F32)
    scr = (_nbytes((HEAD_DIM_B, vp, LANES), F32) + _nbytes((nb, B_W, tc), F32)
           + 5 * _nbytes((tc, HEAD_DIM_B, LANES), F32) + 2 * _nbytes((tc, vp, LANES), F32))
    return pl.pallas_call(
        functools.partial(_rwkv_scan_rows_body, tc=tc, nb=nb, dup=dup), grid=(t // tc,),
        in_specs=[xspec] * 6 + [sspec], out_specs=[xspec, sspec],
        out_shape=[jax.ShapeDtypeStruct((nb, t, B_W), F32), jax.ShapeDtypeStruct((HEAD_DIM_B, vp, LANES), F32)],
        scratch_shapes=[pltpu.VMEM((HEAD_DIM_B, vp, LANES), F32), pltpu.VMEM((nb, B_W, tc), F32)]
        + [key_tile] * 5 + [val_tile] * 2,
        compiler_params=_params(("arbitrary",), blk, scr), name="rwkv_scan_rows")(r, w, k, kk, bv, v, s0)


def _rwkv_scan(r, w, k, kk, bv, v, s0):
    t, vp, _ = v.shape
    tc = min(t, 64)
    kspec = pl.BlockSpec((tc, HEAD_DIM_B, LANES), lambda i: (i, 0, 0))
    vspec = pl.BlockSpec((tc, vp, LANES), lambda i: (i, 0, 0))
    sspec = pl.BlockSpec((HEAD_DIM_B, vp, LANES), lambda i: (0, 0, 0))
    blk = 5 * _nbytes((tc, HEAD_DIM_B, LANES), F32) + 2 * _nbytes((tc, vp, LANES), F32) \
        + 2 * _nbytes((HEAD_DIM_B, vp, LANES), F32)
    return pl.pallas_call(
        functools.partial(_rwkv_scan_body, tc=tc, vp=vp), grid=(t // tc,),
        in_specs=[kspec] * 5 + [vspec, sspec], out_specs=[vspec, sspec],
        out_shape=[jax.ShapeDtypeStruct((t, vp, LANES), F32), jax.ShapeDtypeStruct((HEAD_DIM_B, vp, LANES), F32)],
        scratch_shapes=[pltpu.VMEM((HEAD_DIM_B, vp, LANES), F32)],
        compiler_params=_params(("arbitrary",), blk, _nbytes((HEAD_DIM_B, vp, LANES), F32)),
        name="rwkv_scan")(r, w, k, kk, bv, v, s0)


def _rwkv_mix(zm3, zr3, shift_prev, wkv0, p):
    b, t, _ = zm3.shape
    nh, hd = N_HEADS_B, HEAD_DIM_B
    r, w, kmod, v, kk, bv, g, bonus = _rwkv_prep(zm3, zr3, shift_prev, p)
    dup = LANES // (b * nh)
    assert dup * b * nh == LANES and hd % dup == 0
    vp = hd // dup

    def key_layout(x):
        y = x.reshape(b, t, nh, hd).transpose(1, 3, 0, 2)[:, :, :, None]
        return jnp.broadcast_to(y, (t, hd, b, dup, nh)).reshape(t, hd, LANES)

    def val_layout(x):
        return x.reshape(b, t, nh, dup, vp).transpose(1, 4, 0, 3, 2).reshape(t, vp, LANES)

    s0 = wkv0.reshape(b, nh, dup, vp, hd).transpose(4, 3, 0, 2, 1).reshape(hd, vp, LANES)
    if t % LANES == 0:
        o, s = _rwkv_scan_rows(r, w, kmod, kk, bv, v, s0, dup)
    else:
        o, s = _rwkv_scan(key_layout(r), key_layout(w), key_layout(kmod), key_layout(kk), key_layout(bv),
                          val_layout(v), s0)
        o = o.reshape(t, vp, b, dup, nh).transpose(2, 0, 4, 3, 1).reshape(b, t, B_W)
    s = s.reshape(hd, vp, b, dup, nh).transpose(2, 4, 3, 1, 0).reshape(b, nh, hd, hd)
    return o, bonus, g, s


def _gmlp_body(u_ref, v_ref, ws_ref, b_ref, o_ref, *, tc):
    keep = (lax.broadcasted_iota(jnp.int32, (CHUNK, CHUNK), 1) <= lax.broadcasted_iota(jnp.int32, (CHUNK, CHUNK), 0))
    for g in range(N_GROUPS_C):
        sl = slice(g * CHUNK, (g + 1) * CHUNK)
        w = jnp.where(keep, ws_ref[g], 0.0).astype(BF16)
        v = v_ref[0, :, sl]
        if tc < CHUNK:
            v = jnp.concatenate([v, jnp.zeros((CHUNK - tc, CHUNK), F32)], axis=0)
        s = jnp.dot(w, v.astype(BF16), preferred_element_type=F32) + b_ref[g]
        o_ref[0, :, sl] = u_ref[0, :, sl] * s[:tc]


def _gmlp(z3, w_s, b_s):
    b, t, _ = z3.shape
    tc = min(t, CHUNK)
    assert t % tc == 0
    blk = 3 * _nbytes((tc, C_W), F32) + 2 * _nbytes((N_GROUPS_C, CHUNK, CHUNK), F32)
    return pl.pallas_call(
        functools.partial(_gmlp_body, tc=tc), grid=(b, t // tc),
        in_specs=[pl.BlockSpec((1, tc, C_W), lambda bi, ci: (bi, ci, ZR_CU // C_W)),
                  pl.BlockSpec((1, tc, C_W), lambda bi, ci: (bi, ci, ZR_CV // C_W)),
                  pl.BlockSpec((N_GROUPS_C, CHUNK, CHUNK), lambda bi, ci: (0, 0, 0)),
                  pl.BlockSpec((N_GROUPS_C, CHUNK, 1), lambda bi, ci: (0, 0, 0))],
        out_specs=pl.BlockSpec((1, tc, C_W), lambda bi, ci: (bi, ci, 0)),
        out_shape=jax.ShapeDtypeStruct((b, t, C_W), F32),
        compiler_params=_params(("parallel", "parallel"), blk), name="gmlp")(z3, z3, w_s, b_s[:, :, None])


POOL_HALO = 16


def _pool_body(p_ref, pre_ref, w_ref, sc_ref, o_ref, ext, *, t_len, pos0, tc):
    ext[0:POOL_HALO, :] = pre_ref[0]
    ext[POOL_HALO:POOL_HALO + t_len, :] = p_ref[0]
    for c0 in range(0, t_len, tc):
        pos = pos0 + c0 + lax.broadcasted_iota(jnp.int32, (tc, 1), 0)
        for g, win in enumerate(POOL_WINDOWS):
            sl = slice(g * POOL_GROUP, (g + 1) * POOL_GROUP)
            base = POOL_HALO + c0
            acc = ext[base:base + tc, sl]
            for i in range(1, win):
                acc = acc + ext[base - i:base - i + tc, sl]
            cnt = jnp.minimum(win, pos + 1).astype(F32)
            pooled = acc / cnt - p_ref[0, c0:c0 + tc, sl]
            y = jnp.dot(pooled.astype(BF16), w_ref[g], preferred_element_type=F32)
            o_ref[0, c0:c0 + tc, sl] = y * sc_ref[:, sl]


def _pool(z3, prefix, pos0, w_pool, scale):
    b, t, _ = z3.shape
    assert POOL_PREV < POOL_HALO
    pre = jnp.pad(prefix, ((0, 0), (POOL_HALO - POOL_PREV, 0), (0, 0)))
    tc = min(t, 256)
    blk = 2 * _nbytes((t, D_W), F32) + _nbytes((POOL_HALO, D_W), F32) + _nbytes(w_pool.shape, BF16)
    scr = _nbytes((t + POOL_HALO, D_W), F32)
    return pl.pallas_call(
        functools.partial(_pool_body, t_len=t, pos0=pos0, tc=tc), grid=(b,),
        in_specs=[pl.BlockSpec((1, t, D_W), lambda bi: (bi, 0, ZR_D // D_W)),
                  pl.BlockSpec((1, POOL_HALO, D_W), lambda bi: (bi, 0, 0)),
                  pl.BlockSpec(w_pool.shape, lambda bi: (0, 0, 0)),
                  pl.BlockSpec((1, D_W), lambda bi: (0, 0))],
        out_specs=pl.BlockSpec((1, t, D_W), lambda bi: (bi, 0, 0)),
        out_shape=jax.ShapeDtypeStruct((b, t, D_W), F32),
        scratch_shapes=[pltpu.VMEM((t + POOL_HALO, D_W), F32)],
        compiler_params=_params(("parallel",), blk, scr), name="pool")(z3, pre, w_pool, scale.reshape(1, D_W))


def _mix_body(oa_ref, ob_ref, bonus_ref, g_ref, oc_ref, od_ref, ga_ref, lnw_ref, lnb_ref, gc_ref, gd_ref, ones_ref,
              o_ref):
    def rms(x, gain):
        ms = jnp.mean(x * x, axis=-1, keepdims=True)
        return x * lax.rsqrt(ms + RMS_EPS) * gain

    ones_bd = ones_ref[...]
    o = ob_ref[...]
    mean = _group_sum(o, ones_bd) * (1.0 / HEAD_DIM_B)
    d = o - mean
    var = _group_sum(d * d, ones_bd) * (1.0 / HEAD_DIM_B)
    ob = d * lax.rsqrt(var + GN_EPS) * lnw_ref[...] + lnb_ref[...]
    ob = (ob + bonus_ref[...]) * g_ref[...]
    o_ref[:, 0:A_W] = rms(oa_ref[...], ga_ref[...]).astype(BF16)
    o_ref[:, A_W:A_W + B_W] = ob.astype(BF16)
    o_ref[:, A_W + B_W:A_W + B_W + C_W] = rms(oc_ref[...], gc_ref[...]).astype(BF16)
    o_ref[:, A_W + B_W + C_W:] = rms(od_ref[...], gd_ref[...]).astype(BF16)


def _mix(oa, ob, bonus, g, oc, od, ga, lnw, lnb, gc, gd, ones_bd):
    m = oa.shape[0]
    tm = min(m, 256)
    act = pl.BlockSpec((tm, A_W), lambda i: (i, 0))
    row = pl.BlockSpec((1, A_W), lambda i: (0, 0))
    blk = 6 * _nbytes((tm, A_W), F32) + _nbytes((tm, D_MODEL), BF16) + _nbytes((512, A_W), F32)
    r1 = lambda a: a.reshape(1, -1)
    return pl.pallas_call(
        _mix_body, grid=(m // tm,),
        in_specs=[act] * 6 + [row] * 5 + [pl.BlockSpec((LANES, LANES), lambda i: (0, 0))],
        out_specs=pl.BlockSpec((tm, D_MODEL), lambda i: (i, 0)),
        out_shape=jax.ShapeDtypeStruct((m, D_MODEL), BF16),
        compiler_params=_params(("parallel",), blk), name="mix")(
            oa, ob, bonus, g, oc, od, r1(ga), r1(lnw), r1(lnb), r1(gc), r1(gd), ones_bd)


def _prep_rest_weights(w_in_t):
    assert ZM_W == OFF_B + 3 * B_W
    zpad = jnp.zeros((DEPTH, ZR_W - ZR_LORA - LORA_ALL, D_MODEL), w_in_t.dtype)
    return jnp.concatenate([w_in_t[:, OFF_C:], w_in_t[:, ZM_W:OFF_C], zpad], axis=1).astype(BF16)


def _prep_layer_weights(l, w):
    mu = w['mu_b'][l]
    pad_rows = lambda a, before, total: jnp.pad(a, ((before, total - before - a.shape[0]), (0, 0))).astype(BF16)
    ones_bd = jnp.kron(jnp.eye(LANES // HEAD_DIM_B, dtype=F32), jnp.ones((HEAD_DIM_B, HEAD_DIM_B), F32))
    rw = dict(
        mu_x=mu[None, :3 * B_W], mu_lo=jnp.pad(mu[None, 3 * B_W:], ((0, 0), (0, LORA_PAD - LORA_ALL))),
        w0=w['w0'][l][None], a0=w['a0'][l][None], k_k=w['k_k'][l][None], k_a=w['k_a'][l][None],
        r_k=w['r_k'][l].reshape(1, B_W),
        w_up=pad_rows(w['w_up'][l], 0, LANES), a_up=pad_rows(w['a_up'][l], LORA_W, LANES),
        g_up=pad_rows(w['g_up'][l], 0, LORA_PAD - LANES), ones_bd=ones_bd)
    return dict(w_pool=w['w_pool'][l].astype(BF16), rwkv=rw)


def _layer(x, l, w, w_rest, wl, wb, pos0, kv_prefix, shift_prev, wkv0, pool_prefix):
    cast = wb is None
    wb = {} if cast else wb
    b, t, _ = x.shape
    m = b * t

    def up(xn, gate, upw):
        if not cast:
            return _ffn_up(xn, wb[gate], wb[upw])
        hid, wb[gate], wb[upw] = _ffn_up_cast(xn, w[gate], w[upw], l)
        return hid

    def mm(a, name, n, tm, res=None, scale=1.0, trans_b=False):
        if not cast:
            return _matmul(a, wb[name], res=res, scale=scale, tm=tm, trans_b=trans_b)
        out, wb[name] = _matmul_cast(a, w[name], l, n, res=res, scale=scale, trans_b=trans_b)
        return out

    x2 = x.reshape(m, D_MODEL)
    hid = up(_rmsnorm(x2, w['ln_ffn1'][l], BF16), 'w1_gate', 'w1_up')
    h = mm(hid, 'w1_down', D_MODEL, 512, res=x2, scale=0.5)
    hn = _rmsnorm(h, w['ln_mix'][l], BF16)
    zm3 = mm(hn, 'w_in_t', ZM_W, 1024, trans_b=True).reshape(b, t, ZM_W)
    zr3 = _matmul(hn, w_rest, l, trans_b=True).reshape(b, t, ZR_W)
    out_a, ka = _attention(zm3, pos0, kv_prefix, l)
    o_b, bonus, gate, wkv_new = _rwkv_mix(zm3, zr3, shift_prev, wkv0, wl['rwkv'])
    out_c = _gmlp(zr3, w['w_s'][l], w['b_s'][l])
    out_d = _pool(zr3, pool_prefix, pos0, wl['w_pool'], w['pool_scale'][l])
    r2 = lambda a: a.reshape(m, -1)
    mix = _mix(r2(out_a), r2(o_b), r2(bonus), r2(gate), r2(out_c), r2(out_d), w['g_out_a'][l], w['ln_x_w'][l],
               w['ln_x_b'][l], w['g_out_c'][l], w['g_out_d'][l], wl['rwkv']['ones_bd'])
    h = mm(mix, 'w_out', D_MODEL, 1024, res=h, scale=1.0)
    hid = up(_rmsnorm(h, w['ln_ffn2'][l], BF16), 'w2_gate', 'w2_up')
    y = mm(hid, 'w2_down', D_MODEL, 512, res=h, scale=0.5)
    hs = (b, t, N_HEADS_A, HEAD_DIM_A)
    va = zm3[..., ZM_VA:ZM_VA + A_W]
    shift_new = jnp.concatenate([zm3[:, -1, ZM_RKV:], zr3[:, -1, ZR_LORA:ZR_LORA + LORA_ALL]], axis=-1)
    p = zr3[..., ZR_D:ZR_D + D_W]
    pool_new = jnp.concatenate([pool_prefix, p], axis=1)[:, -POOL_PREV:]
    vc = zr3[..., ZR_CV:ZR_CV + C_W]
    return y.reshape(b, t, D_MODEL), (ka.reshape(hs), va.reshape(hs), wkv_new, shift_new, pool_new, vc), wb


def kernel(x_prompt, x_sample, cache_k_swa, cache_v_swa, state_rwkv_wkv, state_rwkv_shift, state_pool, ln_ffn1, w1_gate, w1_up, w1_down, ln_mix, w_in, g_out_a, mu_b, w0, w_up, a0, a_up, g_up, k_k, k_a, r_k, ln_x_w, ln_x_b, w_s, b_s, g_out_c, w_pool, pool_scale, g_out_d, w_out, ln_ffn2, w2_gate, w2_up, w2_down, ln_final):
    w = dict(ln_ffn1=ln_ffn1, w1_gate=w1_gate, w1_up=w1_up, w1_down=w1_down, ln_mix=ln_mix, w_in=w_in,
             g_out_a=g_out_a, mu_b=mu_b, w0=w0, w_up=w_up, a0=a0, a_up=a_up, g_up=g_up, k_k=k_k, k_a=k_a,
             r_k=r_k, ln_x_w=ln_x_w, ln_x_b=ln_x_b, w_s=w_s, b_s=b_s, g_out_c=g_out_c, w_pool=w_pool,
             pool_scale=pool_scale, g_out_d=g_out_d, w_out=w_out, ln_ffn2=ln_ffn2, w2_gate=w2_gate,
             w2_up=w2_up, w2_down=w2_down)
    nbp, t_prompt, _ = x_prompt.shape
    yp, ys = x_prompt, x_sample
    p_states, s_states = [], []
    w['w_in_t'] = jnp.swapaxes(w_in, 1, 2)
    w_rest = _prep_rest_weights(w['w_in_t'])
    for l in range(DEPTH):
        wl = _prep_layer_weights(l, w)
        ys, ss, wb = _layer(ys, l, w, w_rest, wl, None, PAST_LEN, (cache_k_swa, cache_v_swa),
                            state_rwkv_shift[l], state_rwkv_wkv[l], state_pool[l])
        yp, sp, _ = _layer(yp, l, w, w_rest, wl, wb, 0, None,
                           jnp.zeros((nbp, B_FEAT), F32),
                           jnp.zeros((nbp, N_HEADS_B, HEAD_DIM_B, HEAD_DIM_B), F32),
                           jnp.zeros((nbp, POOL_PREV, D_W), F32))
        p_states.append(sp)
        s_states.append(ss)
    keep = min(WIN_MAX, t_prompt)
    stack = lambda states, i: jnp.stack([s[i] for s in states])
    y_prompt = _rmsnorm(yp.reshape(-1, D_MODEL), ln_final, F32).reshape(yp.shape)
    y_sample = _rmsnorm(ys.reshape(-1, D_MODEL), ln_final, F32).reshape(ys.shape)
    return (y_prompt, y_sample,
            jnp.stack([s[0][:, -keep:] for s in p_states]), jnp.stack([s[1][:, -keep:] for s in p_states]),
            stack(p_states, 2), stack(p_states, 3), stack(p_states, 4),
            stack(s_states, 0), stack(s_states, 1), stack(s_states, 2), stack(s_states, 3), stack(s_states, 4),
            stack(s_states, 5))
```

```python
import functools

import jax
import jax.numpy as jnp
from jax import lax
from jax.experimental import pallas as pl
from jax.experimental.pallas import tpu as pltpu

F32 = jnp.float32
BF16 = jnp.bfloat16

D_MODEL = 4096
DEPTH = 2
PAST_LEN = 16384
A_W = B_W = C_W = D_W = D_MODEL // 4
RMS_EPS = 1e-6
HEAD_DIM_A = 128
N_HEADS_A = A_W // HEAD_DIM_A
DILATED_BRANCHES = ((128, 1), (512, 4), (2048, 16))
WIN_MAX = 2048
ROPE_THETA = 10000.0
HEAD_DIM_B = 64
N_HEADS_B = B_W // HEAD_DIM_B
LORA_W, LORA_A, LORA_G = 64, 64, 160
LORA_ALL = LORA_W + LORA_A + LORA_G
GN_EPS = 64e-5
B_FEAT = 3 * B_W + LORA_ALL
CHUNK = 128
N_GROUPS_C = 8
POOL_WINDOWS = (2, 4, 8, 16)
POOL_PREV = max(POOL_WINDOWS) - 1
POOL_GROUP = D_W // len(POOL_WINDOWS)
OFF_B = 3 * A_W
OFF_C = OFF_B + B_FEAT
OFF_D = OFF_C + 2 * C_W

LANES = 128
SUBLANES = 8
VMEM_BYTES_V7X = 64 * 2**20
VMEM_INTERNAL_RESERVE = 12 * 2**20

MXU_COLS_V7X = 256
MM_TN = 2 * MXU_COLS_V7X

ZM_QA, ZM_KA, ZM_VA = 0, A_W, 2 * A_W
ZM_RKV = 3 * A_W
ZM_W = ZM_RKV + 3 * B_W
ZR_CU = 0
ZR_CV = ZR_CU + C_W
ZR_D = ZR_CV + C_W
ZR_LORA = ZR_D + D_W
LORA_PAD = 3 * LANES
ZR_W = -(-(ZR_LORA + LORA_PAD) // MM_TN) * MM_TN

NEG = -1e30


def _vmem_limit(block_bytes, scratch_bytes=0):
    need = 2 * block_bytes + scratch_bytes + VMEM_INTERNAL_RESERVE
    return int(min(max(need, 16 * 2**20), VMEM_BYTES_V7X - 4 * 2**20))


def _params(sem, block_bytes, scratch_bytes=0):
    return pltpu.CompilerParams(dimension_semantics=sem,
                                vmem_limit_bytes=_vmem_limit(block_bytes, scratch_bytes))


def _nbytes(shape, dtype):
    n = 1
    for s in shape:
        n *= s
    return n * jnp.dtype(dtype).itemsize


def _rmsnorm_body(x_ref, g_ref, o_ref):
    x = x_ref[...]
    ms = jnp.mean(x * x, axis=-1, keepdims=True)
    o_ref[...] = (x * lax.rsqrt(ms + RMS_EPS) * g_ref[...]).astype(o_ref.dtype)


def _rmsnorm(x, g, out_dtype):
    m, d = x.shape
    tm = min(m, 256)
    blk = _nbytes((tm, d), F32) + _nbytes((tm, d), out_dtype)
    return pl.pallas_call(
        _rmsnorm_body, grid=(m // tm,),
        in_specs=[pl.BlockSpec((tm, d), lambda i: (i, 0)), pl.BlockSpec((1, d), lambda i: (0, 0))],
        out_specs=pl.BlockSpec((tm, d), lambda i: (i, 0)),
        out_shape=jax.ShapeDtypeStruct((m, d), out_dtype),
        compiler_params=_params(("parallel",), blk), name="rmsnorm")(x, g.reshape(1, d))


def _ffn_up_body(x_ref, wg_ref, wu_ref, o_ref):
    x = x_ref[...]
    g = jnp.dot(x, wg_ref[...], preferred_element_type=F32)
    u = jnp.dot(x, wu_ref[...], preferred_element_type=F32)
    o_ref[...] = (g * jax.nn.sigmoid(g) * u).astype(o_ref.dtype)


MM_TM_BIG = 2048


def _row_block_spec(tm, k):
    if tm < MM_TM_BIG:
        return pl.BlockSpec((tm, k), lambda i, j: (i, 0))
    return pl.BlockSpec((tm, k), lambda i, j: (i, 0), pipeline_mode=pl.Buffered(1))


def _ffn_up_cast_body(x_ref, wg_ref, wu_ref, o_ref, wgo_ref, wuo_ref):
    wg = wg_ref[...].astype(BF16)
    wu = wu_ref[...].astype(BF16)
    wgo_ref[...] = wg
    wuo_ref[...] = wu
    x = x_ref[...]
    g = jnp.dot(x, wg, preferred_element_type=F32)
    u = jnp.dot(x, wu, preferred_element_type=F32)
    o_ref[...] = (g * jax.nn.sigmoid(g) * u).astype(o_ref.dtype)


def _ffn_up(x, wg, wu):
    m, k = x.shape
    n = wg.shape[1]
    tm = min(m, MM_TM_BIG)
    tn = MXU_COLS_V7X
    blk = _nbytes((tm, k), BF16) + 2 * _nbytes((k, tn), BF16) + _nbytes((tm, tn), BF16)
    return pl.pallas_call(
        _ffn_up_body, grid=(m // tm, n // tn),
        in_specs=[_row_block_spec(tm, k),
                  pl.BlockSpec((k, tn), lambda i, j: (0, j)),
                  pl.BlockSpec((k, tn), lambda i, j: (0, j))],
        out_specs=pl.BlockSpec((tm, tn), lambda i, j: (i, j)),
        out_shape=jax.ShapeDtypeStruct((m, n), BF16),
        compiler_params=_params(("parallel", "arbitrary"), blk), name="ffn_up")(x, wg, wu)


def _ffn_up_cast(x, wg, wu, l):
    m, k = x.shape
    n = wg.shape[2]
    tn = MXU_COLS_V7X
    blk = _nbytes((m, k), BF16) + 2 * _nbytes((k, tn), F32) + 2 * _nbytes((k, tn), BF16) + _nbytes((m, tn), BF16)
    wspec = pl.BlockSpec((None, k, tn), lambda j: (l, 0, j))
    ospec = pl.BlockSpec((k, tn), lambda j: (0, j))
    return pl.pallas_call(
        _ffn_up_cast_body, grid=(n // tn,),
        in_specs=[pl.BlockSpec((m, k), lambda j: (0, 0)), wspec, wspec],
        out_specs=[pl.BlockSpec((m, tn), lambda j: (0, j)), ospec, ospec],
        out_shape=[jax.ShapeDtypeStruct((m, n), BF16), jax.ShapeDtypeStruct((k, n), BF16),
                   jax.ShapeDtypeStruct((k, n), BF16)],
        compiler_params=_params(("parallel",), blk), name="ffn_up_cast")(x, wg, wu)


def _mm_body(a_ref, b_ref, o_ref):
    o_ref[...] = jnp.dot(a_ref[...], b_ref[...], preferred_element_type=F32)


def _mm_res_body(a_ref, b_ref, r_ref, o_ref, *, scale):
    acc = jnp.dot(a_ref[...], b_ref[...], preferred_element_type=F32)
    o_ref[...] = r_ref[...] + scale * acc


_TRANS_B = (((1,), (1,)), ((), ()))


def _mm_t_body(a_ref, bt_ref, o_ref):
    o_ref[...] = lax.dot_general(a_ref[...], bt_ref[...], _TRANS_B, preferred_element_type=F32)


def _mm_cast_body(a_ref, b_ref, *rest, scale, has_res, trans_b):
    r_ref = rest[0] if has_res else None
    o_ref, bo_ref = rest[-2:]
    b = b_ref[...].astype(BF16)
    bo_ref[...] = b
    if trans_b:
        acc = lax.dot_general(a_ref[...], b, _TRANS_B, preferred_element_type=F32)
    else:
        acc = jnp.dot(a_ref[...], b, preferred_element_type=F32)
    o_ref[...] = r_ref[...] + scale * acc if has_res else acc


def _matmul(a, b, l=None, res=None, scale=1.0, tm=1024, trans_b=False):
    m, k = a.shape
    n = b.shape[-2] if trans_b else b.shape[-1]
    tm = min(m, tm)
    tn = MM_TN
    assert m % tm == 0 and n % tn == 0
    blk = _nbytes((tm, k), BF16) + _nbytes((k, tn), BF16) + _nbytes((tm, tn), F32)
    bshape, bidx = ((tn, k), lambda i, j: (j, 0)) if trans_b else ((k, tn), lambda i, j: (0, j))
    bspec = (pl.BlockSpec(bshape, bidx) if b.ndim == 2
             else pl.BlockSpec((None,) + bshape, lambda i, j: (l,) + bidx(i, j)))
    in_specs = [_row_block_spec(tm, k), bspec]
    args = [a, b]
    if trans_b:
        assert res is None
        body = _mm_t_body
    elif res is None:
        body = _mm_body
    else:
        body = functools.partial(_mm_res_body, scale=scale)
        in_specs.append(pl.BlockSpec((tm, tn), lambda i, j: (i, j)))
        args.append(res)
        blk += _nbytes((tm, tn), F32)
    return pl.pallas_call(
        body, grid=(m // tm, n // tn), in_specs=in_specs,
        out_specs=pl.BlockSpec((tm, tn), lambda i, j: (i, j)),
        out_shape=jax.ShapeDtypeStruct((m, n), F32),
        compiler_params=_params(("parallel", "arbitrary"), blk), name="matmul")(*args)


CAST_TILE_BYTES = 6 * 2**20


def _matmul_cast(a, b, l, n, res=None, scale=1.0, trans_b=False):
    m, k = a.shape
    tn = MM_TN
    while _nbytes((k, tn), F32) > CAST_TILE_BYTES:
        tn //= 2
    assert n % tn == 0 and tn % LANES == 0
    blk = _nbytes((m, k), BF16) + _nbytes((k, tn), F32) + _nbytes((k, tn), BF16) + 2 * _nbytes((m, tn), F32)
    if trans_b:
        wspec = pl.BlockSpec((None, tn, k), lambda j: (l, j, 0))
        cspec, cshape = pl.BlockSpec((tn, k), lambda j: (j, 0)), (n, k)
    else:
        wspec = pl.BlockSpec((None, k, tn), lambda j: (l, 0, j))
        cspec, cshape = pl.BlockSpec((k, tn), lambda j: (0, j)), (k, n)
    in_specs = [pl.BlockSpec((m, k), lambda j: (0, 0)), wspec]
    args = [a, b]
    if res is not None:
        in_specs.append(pl.BlockSpec((m, tn), lambda j: (0, j)))
        args.append(res)
    return pl.pallas_call(
        functools.partial(_mm_cast_body, scale=scale, has_res=res is not None, trans_b=trans_b), grid=(n // tn,),
        in_specs=in_specs, out_specs=[pl.BlockSpec((m, tn), lambda j: (0, j)), cspec],
        out_shape=[jax.ShapeDtypeStruct((m, n), F32), jax.ShapeDtypeStruct(cshape, BF16)],
        compiler_params=_params(("parallel",), blk), name="matmul_cast")(*args)


def _rope(x, cos, sin_signed):
    return x * cos + pltpu.roll(x, HEAD_DIM_A // 2, 1) * sin_signed


def _branch_multiplicity(delta):
    c = jnp.zeros(delta.shape, F32)
    for window, dilation in DILATED_BRANCHES:
        assert dilation & (dilation - 1) == 0
        hit = jnp.where(delta <= window, 1.0, 0.0)
        if dilation > 1:
            hit = jnp.where((delta & (dilation - 1)) == 0, hit, 0.0)
        c = c + hit
    return jnp.where(delta >= 0, c, 0.0)


def _attn_prompt_body(q_ref, k_ref, v_ref, cos_ref, sin_ref, o_ref, kout_ref, qs, ks, vs, ctab, *, t_len, tq):
    nq = t_len // tq

    @pl.when((pl.program_id(0) == 0) & (pl.program_id(1) == 0))
    def _():
        rel = (lax.broadcasted_iota(jnp.int32, (tq, tq), 0) - lax.broadcasted_iota(jnp.int32, (tq, tq), 1))
        for d in range(nq):
            ctab[:, d * tq:(d + 1) * tq] = _branch_multiplicity(rel + (nq - 1 - d) * tq)

    cos = cos_ref[...]
    sin = sin_ref[...]
    k = _rope(k_ref[0], cos, sin)
    kout_ref[0] = k
    ks[...] = k.astype(BF16)
    qs[...] = (_rope(q_ref[0], cos, sin) * (HEAD_DIM_A ** -0.5)).astype(BF16)
    vs[...] = v_ref[0].astype(BF16)
    for i in range(nq):
        kw = (i + 1) * tq
        s = lax.dot_general(qs[i * tq:(i + 1) * tq, :], ks[0:kw, :], _TRANS_B, preferred_element_type=F32)
        c = ctab[:, (nq - 1 - i) * tq:]
        sm = jnp.where(c > 0.0, s, NEG)
        p = jnp.exp(sm - jnp.max(sm, axis=-1, keepdims=True)) * c
        l = jnp.sum(p, axis=-1, keepdims=True)
        acc = jnp.dot(p.astype(BF16), vs[0:kw, :], preferred_element_type=F32)
        o_ref[0, i * tq:(i + 1) * tq, :] = acc / l


def _attn_sample_body(q_ref, k_ref, v_ref, kp_ref, vp_ref, cos_ref, sin_ref, o_ref, kout_ref, *, t_len, n_prev):
    cos = cos_ref[...]
    sin = sin_ref[...]
    d1 = (n_prev + lax.broadcasted_iota(jnp.int32, (t_len, n_prev), 0)
          - lax.broadcasted_iota(jnp.int32, (t_len, n_prev), 1))
    c1 = _branch_multiplicity(d1)
    d2 = (lax.broadcasted_iota(jnp.int32, (t_len, LANES), 0) - lax.broadcasted_iota(jnp.int32, (t_len, LANES), 1))
    c2 = _branch_multiplicity(d2)
    pad = jnp.zeros((LANES - t_len, HEAD_DIM_A), F32)
    for h in range(N_HEADS_A):
        sl = slice(h * HEAD_DIM_A, (h + 1) * HEAD_DIM_A)
        k = _rope(k_ref[0, :, sl], cos, sin)
        kout_ref[0, :, sl] = k
        q = (_rope(q_ref[0, :, sl], cos, sin) * (HEAD_DIM_A ** -0.5)).astype(BF16)
        kn = jnp.concatenate([k, pad], axis=0).astype(BF16)
        vn = jnp.concatenate([v_ref[0, :, sl], pad], axis=0).astype(BF16)
        s1 = lax.dot_general(q, kp_ref[:, h, :].astype(BF16), _TRANS_B, preferred_element_type=F32)
        s2 = lax.dot_general(q, kn, _TRANS_B, preferred_element_type=F32)
        sm1 = jnp.where(c1 > 0.0, s1, NEG)
        sm2 = jnp.where(c2 > 0.0, s2, NEG)
        m = jnp.maximum(jnp.max(sm1, axis=-1, keepdims=True), jnp.max(sm2, axis=-1, keepdims=True))
        p1 = jnp.exp(sm1 - m) * c1
        p2 = jnp.exp(sm2 - m) * c2
        l = jnp.sum(p1, axis=-1, keepdims=True) + jnp.sum(p2, axis=-1, keepdims=True)
        acc = (jnp.dot(p1.astype(BF16), vp_ref[:, h, :].astype(BF16), preferred_element_type=F32)
               + jnp.dot(p2.astype(BF16), vn, preferred_element_type=F32))
        o_ref[0, :, sl] = acc / l


def _rope_tables(pos0, t_len):
    half = HEAD_DIM_A // 2
    inv = ROPE_THETA ** (-jnp.arange(half, dtype=F32) / half)
    ang = (pos0 + jnp.arange(t_len)).astype(F32)[:, None] * inv[None, :]
    cos = jnp.cos(ang)
    sin = jnp.sin(ang)
    return jnp.concatenate([cos, cos], axis=-1), jnp.concatenate([-sin, sin], axis=-1)


def _attention(z3, pos0, kv_prefix, l):
    b, t, _ = z3.shape
    cos, sin = _rope_tables(pos0, t)
    hd = HEAD_DIM_A
    out_shape = [jax.ShapeDtypeStruct((b, t, A_W), F32), jax.ShapeDtypeStruct((b, t, A_W), F32)]
    if kv_prefix is None:
        col = lambda base: (lambda bi, hi: (bi, 0, base // hd + hi))
        tab = pl.BlockSpec((t, hd), lambda bi, hi: (0, 0))
        zspecs = [pl.BlockSpec((1, t, hd), col(ZM_QA)), pl.BlockSpec((1, t, hd), col(ZM_KA)),
                  pl.BlockSpec((1, t, hd), col(ZM_VA))]
        out_specs = [pl.BlockSpec((1, t, hd), col(0)), pl.BlockSpec((1, t, hd), col(0))]
        blk = 7 * _nbytes((t, hd), F32)
        tq = min(t, 256)
        body = functools.partial(_attn_prompt_body, t_len=t, tq=tq)
        scratch = 3 * _nbytes((t, hd), BF16) + 5 * _nbytes((tq, t), F32)
        return pl.pallas_call(
            body, grid=(b, N_HEADS_A), in_specs=zspecs + [tab, tab], out_specs=out_specs, out_shape=out_shape,
            scratch_shapes=[pltpu.VMEM((t, hd), BF16)] * 3 + [pltpu.VMEM((tq, t), F32)],
            compiler_params=_params(("arbitrary", "arbitrary"), blk, scratch),
            name="attn_prompt")(z3, z3, z3, cos, sin)
    k_prev, v_prev = kv_prefix
    n_prev = k_prev.shape[2]
    assert t <= LANES and n_prev % LANES == 0 and k_prev.shape[3:] == (N_HEADS_A, hd)
    col = lambda base: (lambda bi: (bi, 0, base // A_W))
    tab = pl.BlockSpec((t, hd), lambda bi: (0, 0))
    zspecs = [pl.BlockSpec((1, t, A_W), col(ZM_QA)), pl.BlockSpec((1, t, A_W), col(ZM_KA)),
              pl.BlockSpec((1, t, A_W), col(ZM_VA))]
    out_specs = [pl.BlockSpec((1, t, A_W), col(0)), pl.BlockSpec((1, t, A_W), col(0))]
    pspec = pl.BlockSpec((None, None, n_prev, N_HEADS_A, hd), lambda bi: (l, bi, 0, 0, 0))
    blk = 5 * _nbytes((t, A_W), F32) + 2 * _nbytes((n_prev, A_W), F32)
    body = functools.partial(_attn_sample_body, t_len=t, n_prev=n_prev)
    return pl.pallas_call(
        body, grid=(b,), in_specs=zspecs + [pspec, pspec, tab, tab], out_specs=out_specs,
        out_shape=out_shape, compiler_params=_params(("parallel",), blk),
        name="attn_sample")(z3, z3, z3, k_prev, v_prev, cos, sin)


def _group_sum(x, ones_blockdiag):
    outs = []
    for j in range(x.shape[-1] // LANES):
        outs.append(jnp.dot(x[:, j * LANES:(j + 1) * LANES], ones_blockdiag, preferred_element_type=F32,
                            precision=lax.Precision.HIGHEST))
    return jnp.concatenate(outs, axis=-1)


def _softplus(y):
    return jnp.maximum(y, 0.0) + jnp.log1p(jnp.exp(-jnp.abs(y)))


def _rwkv_prep_body(x_ref, lo_ref, sx_ref, slo_ref, mux_ref, mulo_ref, w0_ref, a0_ref, kkp_ref, kap_ref, rk_ref,
                    wup_ref, aup_ref, gup_ref, ones_ref,
                    r_o, w_o, k_o, v_o, kk_o, b_o, g_o, bonus_o, last_x, last_lo, *, tt):
    @pl.when(pl.program_id(1) == 0)
    def _():
        last_x[0:1, :] = sx_ref[0]
        last_lo[0:1, :] = slo_ref[0]

    x = x_ref[0]
    lo = lo_ref[0]
    first = lax.broadcasted_iota(jnp.int32, (tt, 1), 0) == 0
    px = jnp.where(first, last_x[0:1, :], pltpu.roll(x, 1, 0))
    plo = jnp.where(first, last_lo[0:1, :], pltpu.roll(lo, 1, 0))
    last_x[0:1, :] = x[tt - 1:tt, :]
    last_lo[0:1, :] = lo[tt - 1:tt, :]
    fx = x + mux_ref[...] * (px - x)
    flo = lo + mulo_ref[...] * (plo - lo)
    r = fx[:, :B_W]
    k = fx[:, B_W:2 * B_W]
    v = fx[:, 2 * B_W:]
    zwa = flo[:, :LANES]
    zg = flo[:, LANES:]
    ones_bd = ones_ref[...]
    wl = w0_ref[...] + jnp.dot(jnp.tanh(zwa).astype(BF16), wup_ref[...], preferred_element_type=F32)
    w_log = -_softplus(-wl) - 0.5
    decay = jnp.exp(-jnp.exp(w_log))
    a = jax.nn.sigmoid(a0_ref[...] + jnp.dot(zwa.astype(BF16), aup_ref[...], preferred_element_type=F32))
    g = jnp.dot(jax.nn.sigmoid(zg).astype(BF16), gup_ref[...], preferred_element_type=F32)
    kk = k * kkp_ref[...]
    kk = kk / jnp.maximum(jnp.sqrt(_group_sum(kk * kk, ones_bd)), 1e-12)
    kmod = k * (1.0 + (a - 1.0) * kap_ref[...])
    r_o[0] = r
    w_o[0] = decay
    k_o[0] = kmod
    v_o[0] = v
    kk_o[0] = kk
    b_o[0] = kk * a
    g_o[0] = g
    bonus_o[0] = _group_sum(r * kmod * rk_ref[...], ones_bd) * v


def _rwkv_prep(zm3, zr3, shift_prev, p):
    b, t, _ = zm3.shape
    tt = min(t, 256)
    sx = shift_prev[:, None, :3 * B_W]
    slo = jnp.pad(shift_prev[:, None, 3 * B_W:], ((0, 0), (0, 0), (0, LORA_PAD - LORA_ALL)))
    row = lambda w: pl.BlockSpec((1, w), lambda bi, ti: (0, 0))
    full = lambda a: pl.BlockSpec(a.shape, lambda bi, ti: (0,) * a.ndim)
    in_specs = [pl.BlockSpec((1, tt, 3 * B_W), lambda bi, ti: (bi, ti, ZM_RKV // (3 * B_W))),
                pl.BlockSpec((1, tt, LORA_PAD), lambda bi, ti: (bi, ti, ZR_LORA // LORA_PAD)),
                pl.BlockSpec((1, 1, 3 * B_W), lambda bi, ti: (bi, 0, 0)),
                pl.BlockSpec((1, 1, LORA_PAD), lambda bi, ti: (bi, 0, 0)),
                row(3 * B_W), row(LORA_PAD), row(B_W), row(B_W), row(B_W), row(B_W), row(B_W),
                full(p['w_up']), full(p['a_up']), full(p['g_up']), full(p['ones_bd'])]
    ospec = pl.BlockSpec((1, tt, B_W), lambda bi, ti: (bi, ti, 0))
    blk = _nbytes((tt, 3 * B_W + LORA_PAD), F32) + 8 * _nbytes((tt, B_W), F32) + 2 * _nbytes((512, B_W), F32)
    return pl.pallas_call(
        functools.partial(_rwkv_prep_body, tt=tt), grid=(b, t // tt), in_specs=in_specs,
        out_specs=[ospec] * 8, out_shape=[jax.ShapeDtypeStruct((b, t, B_W), F32)] * 8,
        scratch_shapes=[pltpu.VMEM((SUBLANES, 3 * B_W), F32), pltpu.VMEM((SUBLANES, LORA_PAD), F32)],
        compiler_params=_params(("parallel", "arbitrary"), blk), name="rwkv_prep")(
            zm3, zr3, sx, slo, p['mu_x'], p['mu_lo'], p['w0'], p['a0'], p['k_k'], p['k_a'], p['r_k'],
            p['w_up'], p['a_up'], p['g_up'], p['ones_bd'])


N_PARTIAL = 4


def _rwkv_steps(r_ref, w_ref, k_ref, kk_ref, b_ref, v_ref, o_ref, state, *, tc, vp, time_major):
    def step(t, row):
        parts = [jnp.zeros((vp, LANES), F32)] * N_PARTIAL
        for k in range(HEAD_DIM_B):
            parts[k % N_PARTIAL] = parts[k % N_PARTIAL] + state[k] * row(kk_ref, k)
        sa = -((parts[0] + parts[1]) + (parts[2] + parts[3]))
        vt = v_ref[t]
        parts = [jnp.zeros((vp, LANES), F32)] * N_PARTIAL
        for k in range(HEAD_DIM_B):
            s = state[k] * row(w_ref, k) + sa * row(b_ref, k) + vt * row(k_ref, k)
            state[k] = s
            parts[k % N_PARTIAL] = parts[k % N_PARTIAL] + s * row(r_ref, k)
        o_ref[t] = (parts[0] + parts[1]) + (parts[2] + parts[3])

    if time_major:
        def one(t, carry):
            step(t, lambda ref, k: ref[t, pl.ds(k, 1), :])
            return carry

        lax.fori_loop(0, tc, one, 0)
    else:
        def eight(tb, carry):
            for s in range(SUBLANES):
                step(tb * SUBLANES + s, lambda ref, k, s=s: ref[k, tb, pl.ds(s, 1), :])
            return carry

        lax.fori_loop(0, tc // SUBLANES, eight, 0)


def _rwkv_scan_body(r_ref, w_ref, k_ref, kk_ref, b_ref, v_ref, s0_ref, o_ref, sout_ref, state, *, tc, vp):
    @pl.when(pl.program_id(0) == 0)
    def _():
        state[...] = s0_ref[...]

    _rwkv_steps(r_ref, w_ref, k_ref, kk_ref, b_ref, v_ref, o_ref, state, tc=tc, vp=vp, time_major=True)

    @pl.when(pl.program_id(0) == pl.num_programs(0) - 1)
    def _():
        sout_ref[...] = state[...]


def _rwkv_scan_rows_body(r_ref, w_ref, k_ref, kk_ref, b_ref, v_ref, s0_ref, o_ref, sout_ref,
                         state, yt, rk, wk, kkey, kkk, bk, vk, ok, *, tc, nb, dup):
    vp = HEAD_DIM_B // dup
    nh = N_HEADS_B

    @pl.when(pl.program_id(0) == 0)
    def _():
        state[...] = s0_ref[...]

    def to_lanes(x_ref, store, n_rows, row_of):
        for b in range(nb):
            yt[b] = x_ref[b].T
        for j in range(n_rows):
            pieces = [yt[b, pl.ds(row_of(j, vh), nh, stride=HEAD_DIM_B), :] for b in range(nb) for vh in range(dup)]
            store(j, jnp.concatenate(pieces, axis=0).T)

    def key_store(dst):
        def store(j, x):
            dst[j] = x.reshape(tc // SUBLANES, SUBLANES, LANES)
        return store

    def val_store(j, x):
        vk[:, j, :] = x

    for x_ref, dst in ((r_ref, rk), (w_ref, wk), (k_ref, kkey), (kk_ref, kkk), (b_ref, bk)):
        to_lanes(x_ref, key_store(dst), HEAD_DIM_B, lambda j, vh: j)
    to_lanes(v_ref, val_store, vp, lambda j, vh: vh * vp + j)

    _rwkv_steps(rk, wk, kkey, kkk, bk, vk, ok, state, tc=tc, vp=vp, time_major=False)

    for j in range(vp):
        m = ok[:, j, :].T
        for b in range(nb):
            for vh in range(dup):
                lane0 = (b * dup + vh) * nh
                yt[b, pl.ds(vh * vp + j, nh, stride=HEAD_DIM_B), :] = m[lane0:lane0 + nh, :]
    for b in range(nb):
        o_ref[b] = yt[b].T

    @pl.when(pl.program_id(0) == pl.num_programs(0) - 1)
    def _():
        sout_ref[...] = state[...]


def _rwkv_scan_rows(r, w, k, kk, bv, v, s0, dup):
    nb, t, _ = r.shape
    vp = HEAD_DIM_B // dup
    tc = LANES
    assert t % tc == 0 and nb * dup * N_HEADS_B == LANES
    xspec = pl.BlockSpec((nb, tc, B_W), lambda i: (0, i, 0))
    sspec = pl.BlockSpec((HEAD_DIM_B, vp, LANES), lambda i: (0, 0, 0))
    key_tile = pltpu.VMEM((HEAD_DIM_B, tc // SUBLANES, SUBLANES, LANES), F32)
    val_tile = pltpu.VMEM((tc, vp, LANES), F32)
    blk = 7 * _nbytes((nb, tc, B_W), F32) + 2 * _nbytes((HEAD_DIM_B, vp, LANES), F32)
    scr = (_nbytes((HEAD_DIM_B, vp, LANES), F32) + _nbytes((nb, B_W, tc), F32)
           + 5 * _nbytes((tc, HEAD_DIM_B, LANES), F32) + 2 * _nbytes((tc, vp, LANES), F32))
    return pl.pallas_call(
        functools.partial(_rwkv_scan_rows_body, tc=tc, nb=nb, dup=dup), grid=(t // tc,),
        in_specs=[xspec] * 6 + [sspec], out_specs=[xspec, sspec],
        out_shape=[jax.ShapeDtypeStruct((nb, t, B_W), F32), jax.ShapeDtypeStruct((HEAD_DIM_B, vp, LANES), F32)],
        scratch_shapes=[pltpu.VMEM((HEAD_DIM_B, vp, LANES), F32), pltpu.VMEM((nb, B_W, tc), F32)]
        + [key_tile] * 5 + [val_tile] * 2,
        compiler_params=_params(("arbitrary",), blk, scr), name="rwkv_scan_rows")(r, w, k, kk, bv, v, s0)


def _rwkv_scan(r, w, k, kk, bv, v, s0):
    t, vp, _ = v.shape
    tc = min(t, 64)
    kspec = pl.BlockSpec((tc, HEAD_DIM_B, LANES), lambda i: (i, 0, 0))
    vspec = pl.BlockSpec((tc, vp, LANES), lambda i: (i, 0, 0))
    sspec = pl.BlockSpec((HEAD_DIM_B, vp, LANES), lambda i: (0, 0, 0))
    blk = 5 * _nbytes((tc, HEAD_DIM_B, LANES), F32) + 2 * _nbytes((tc, vp, LANES), F32) \
        + 2 * _nbytes((HEAD_DIM_B, vp, LANES), F32)
    return pl.pallas_call(
        functools.partial(_rwkv_scan_body, tc=tc, vp=vp), grid=(t // tc,),
        in_specs=[kspec] * 5 + [vspec, sspec], out_specs=[vspec, sspec],
        out_shape=[jax.ShapeDtypeStruct((t, vp, LANES), F32), jax.ShapeDtypeStruct((HEAD_DIM_B, vp, LANES), F32)],
        scratch_shapes=[pltpu.VMEM((HEAD_DIM_B, vp, LANES), F32)],
        compiler_params=_params(("arbitrary",), blk, _nbytes((HEAD_DIM_B, vp, LANES), F32)),
        name="rwkv_scan")(r, w, k, kk, bv, v, s0)


def _rwkv_mix(zm3, zr3, shift_prev, wkv0, p):
    b, t, _ = zm3.shape
    nh, hd = N_HEADS_B, HEAD_DIM_B
    r, w, kmod, v, kk, bv, g, bonus = _rwkv_prep(zm3, zr3, shift_prev, p)
    dup = LANES // (b * nh)
    assert dup * b * nh == LANES and hd % dup == 0
    vp = hd // dup

    def key_layout(x):
        y = x.reshape(b, t, nh, hd).transpose(1, 3, 0, 2)[:, :, :, None]
        return jnp.broadcast_to(y, (t, hd, b, dup, nh)).reshape(t, hd, LANES)

    def val_layout(x):
        return x.reshape(b, t, nh, dup, vp).transpose(1, 4, 0, 3, 2).reshape(t, vp, LANES)

    s0 = wkv0.reshape(b, nh, dup, vp, hd).transpose(4, 3, 0, 2, 1).reshape(hd, vp, LANES)
    if t % LANES == 0:
        o, s = _rwkv_scan_rows(r, w, kmod, kk, bv, v, s0, dup)
    else:
        o, s = _rwkv_scan(key_layout(r), key_layout(w), key_layout(kmod), key_layout(kk), key_layout(bv),
                          val_layout(v), s0)
        o = o.reshape(t, vp, b, dup, nh).transpose(2, 0, 4, 3, 1).reshape(b, t, B_W)
    s = s.reshape(hd, vp, b, dup, nh).transpose(2, 4, 3, 1, 0).reshape(b, nh, hd, hd)
    return o, bonus, g, s


def _gmlp_body(u_ref, v_ref, ws_ref, b_ref, o_ref, *, tc):
    keep = (lax.broadcasted_iota(jnp.int32, (CHUNK, CHUNK), 1) <= lax.broadcasted_iota(jnp.int32, (CHUNK, CHUNK), 0))
    for g in range(N_GROUPS_C):
        sl = slice(g * CHUNK, (g + 1) * CHUNK)
        w = jnp.where(keep, ws_ref[g], 0.0).astype(BF16)
        v = v_ref[0, :, sl]
        if tc < CHUNK:
            v = jnp.concatenate([v, jnp.zeros((CHUNK - tc, CHUNK), F32)], axis=0)
        s = jnp.dot(w, v.astype(BF16), preferred_element_type=F32) + b_ref[g]
        o_ref[0, :, sl] = u_ref[0, :, sl] * s[:tc]


def _gmlp(z3, w_s, b_s):
    b, t, _ = z3.shape
    tc = min(t, CHUNK)
    assert t % tc == 0
    blk = 3 * _nbytes((tc, C_W), F32) + 2 * _nbytes((N_GROUPS_C, CHUNK, CHUNK), F32)
    return pl.pallas_call(
        functools.partial(_gmlp_body, tc=tc), grid=(b, t // tc),
        in_specs=[pl.BlockSpec((1, tc, C_W), lambda bi, ci: (bi, ci, ZR_CU // C_W)),
                  pl.BlockSpec((1, tc, C_W), lambda bi, ci: (bi, ci, ZR_CV // C_W)),
                  pl.BlockSpec((N_GROUPS_C, CHUNK, CHUNK), lambda bi, ci: (0, 0, 0)),
                  pl.BlockSpec((N_GROUPS_C, CHUNK, 1), lambda bi, ci: (0, 0, 0))],
        out_specs=pl.BlockSpec((1, tc, C_W), lambda bi, ci: (bi, ci, 0)),
        out_shape=jax.ShapeDtypeStruct((b, t, C_W), F32),
        compiler_params=_params(("parallel", "parallel"), blk), name="gmlp")(z3, z3, w_s, b_s[:, :, None])


POOL_HALO = 16


def _pool_body(p_ref, pre_ref, w_ref, sc_ref, o_ref, ext, *, t_len, pos0, tc):
    ext[0:POOL_HALO, :] = pre_ref[0]
    ext[POOL_HALO:POOL_HALO + t_len, :] = p_ref[0]
    for c0 in range(0, t_len, tc):
        pos = pos0 + c0 + lax.broadcasted_iota(jnp.int32, (tc, 1), 0)
        for g, win in enumerate(POOL_WINDOWS):
            sl = slice(g * POOL_GROUP, (g + 1) * POOL_GROUP)
            base = POOL_HALO + c0
            acc = ext[base:base + tc, sl]
            for i in range(1, win):
                acc = acc + ext[base - i:base - i + tc, sl]
            cnt = jnp.minimum(win, pos + 1).astype(F32)
            pooled = acc / cnt - p_ref[0, c0:c0 + tc, sl]
            y = jnp.dot(pooled.astype(BF16), w_ref[g], preferred_element_type=F32)
            o_ref[0, c0:c0 + tc, sl] = y * sc_ref[:, sl]


def _pool(z3, prefix, pos0, w_pool, scale):
    b, t, _ = z3.shape
    assert POOL_PREV < POOL_HALO
    pre = jnp.pad(prefix, ((0, 0), (POOL_HALO - POOL_PREV, 0), (0, 0)))
    tc = min(t, 256)
    blk = 2 * _nbytes((t, D_W), F32) + _nbytes((POOL_HALO, D_W), F32) + _nbytes(w_pool.shape, BF16)
    scr = _nbytes((t + POOL_HALO, D_W), F32)
    return pl.pallas_call(
        functools.partial(_pool_body, t_len=t, pos0=pos0, tc=tc), grid=(b,),
        in_specs=[pl.BlockSpec((1, t, D_W), lambda bi: (bi, 0, ZR_D // D_W)),
                  pl.BlockSpec((1, POOL_HALO, D_W), lambda bi: (bi, 0, 0)),
                  pl.BlockSpec(w_pool.shape, lambda bi: (0, 0, 0)),
                  pl.BlockSpec((1, D_W), lambda bi: (0, 0))],
        out_specs=pl.BlockSpec((1, t, D_W), lambda bi: (bi, 0, 0)),
        out_shape=jax.ShapeDtypeStruct((b, t, D_W), F32),
        scratch_shapes=[pltpu.VMEM((t + POOL_HALO, D_W), F32)],
        compiler_params=_params(("parallel",), blk, scr), name="pool")(z3, pre, w_pool, scale.reshape(1, D_W))


def _mix_body(oa_ref, ob_ref, bonus_ref, g_ref, oc_ref, od_ref, ga_ref, lnw_ref, lnb_ref, gc_ref, gd_ref, ones_ref,
              o_ref):
    def rms(x, gain):
        ms = jnp.mean(x * x, axis=-1, keepdims=True)
        return x * lax.rsqrt(ms + RMS_EPS) * gain

    ones_bd = ones_ref[...]
    o = ob_ref[...]
    mean = _group_sum(o, ones_bd) * (1.0 / HEAD_DIM_B)
    d = o - mean
    var = _group_sum(d * d, ones_bd) * (1.0 / HEAD_DIM_B)
    ob = d * lax.rsqrt(var + GN_EPS) * lnw_ref[...] + lnb_ref[...]
    ob = (ob + bonus_ref[...]) * g_ref[...]
    o_ref[:, 0:A_W] = rms(oa_ref[...], ga_ref[...]).astype(BF16)
    o_ref[:, A_W:A_W + B_W] = ob.astype(BF16)
    o_ref[:, A_W + B_W:A_W + B_W + C_W] = rms(oc_ref[...], gc_ref[...]).astype(BF16)
    o_ref[:, A_W + B_W + C_W:] = rms(od_ref[...], gd_ref[...]).astype(BF16)


def _mix(oa, ob, bonus, g, oc, od, ga, lnw, lnb, gc, gd, ones_bd):
    m = oa.shape[0]
    tm = min(m, 256)
    act = pl.BlockSpec((tm, A_W), lambda i: (i, 0))
    row = pl.BlockSpec((1, A_W), lambda i: (0, 0))
    blk = 6 * _nbytes((tm, A_W), F32) + _nbytes((tm, D_MODEL), BF16) + _nbytes((512, A_W), F32)
    r1 = lambda a: a.reshape(1, -1)
    return pl.pallas_call(
        _mix_body, grid=(m // tm,),
        in_specs=[act] * 6 + [row] * 5 + [pl.BlockSpec((LANES, LANES), lambda i: (0, 0))],
        out_specs=pl.BlockSpec((tm, D_MODEL), lambda i: (i, 0)),
        out_shape=jax.ShapeDtypeStruct((m, D_MODEL), BF16),
        compiler_params=_params(("parallel",), blk), name="mix")(
            oa, ob, bonus, g, oc, od, r1(ga), r1(lnw), r1(lnb), r1(gc), r1(gd), ones_bd)


def _prep_rest_weights(w_in_t):
    assert ZM_W == OFF_B + 3 * B_W
    zpad = jnp.zeros((DEPTH, ZR_W - ZR_LORA - LORA_ALL, D_MODEL), w_in_t.dtype)
    return jnp.concatenate([w_in_t[:, OFF_C:], w_in_t[:, ZM_W:OFF_C], zpad], axis=1).astype(BF16)


def _prep_layer_weights(l, w):
    mu = w['mu_b'][l]
    pad_rows = lambda a, before, total: jnp.pad(a, ((before, total - before - a.shape[0]), (0, 0))).astype(BF16)
    ones_bd = jnp.kron(jnp.eye(LANES // HEAD_DIM_B, dtype=F32), jnp.ones((HEAD_DIM_B, HEAD_DIM_B), F32))
    rw = dict(
        mu_x=mu[None, :3 * B_W], mu_lo=jnp.pad(mu[None, 3 * B_W:], ((0, 0), (0, LORA_PAD - LORA_ALL))),
        w0=w['w0'][l][None], a0=w['a0'][l][None], k_k=w['k_k'][l][None], k_a=w['k_a'][l][None],
        r_k=w['r_k'][l].reshape(1, B_W),
        w_up=pad_rows(w['w_up'][l], 0, LANES), a_up=pad_rows(w['a_up'][l], LORA_W, LANES),
        g_up=pad_rows(w['g_up'][l], 0, LORA_PAD - LANES), ones_bd=ones_bd)
    return dict(w_pool=w['w_pool'][l].astype(BF16), rwkv=rw)


def _layer(x, l, w, w_rest, wl, wb, pos0, kv_prefix, shift_prev, wkv0, pool_prefix):
    cast = wb is None
    wb = {} if cast else wb
    b, t, _ = x.shape
    m = b * t

    def up(xn, gate, upw):
        if not cast:
            return _ffn_up(xn, wb[gate], wb[upw])
        hid, wb[gate], wb[upw] = _ffn_up_cast(xn, w[gate], w[upw], l)
        return hid

    def mm(a, name, n, tm, res=None, scale=1.0, trans_b=False):
        if not cast:
            return _matmul(a, wb[name], res=res, scale=scale, tm=tm, trans_b=trans_b)
        out, wb[name] = _matmul_cast(a, w[name], l, n, res=res, scale=scale, trans_b=trans_b)
        return out

    x2 = x.reshape(m, D_MODEL)
    hid = up(_rmsnorm(x2, w['ln_ffn1'][l], BF16), 'w1_gate', 'w1_up')
    h = mm(hid, 'w1_down', D_MODEL, 512, res=x2, scale=0.5)
    hn = _rmsnorm(h, w['ln_mix'][l], BF16)
    zm3 = mm(hn, 'w_in_t', ZM_W, MM_TM_BIG, trans_b=True).reshape(b, t, ZM_W)
    zr3 = _matmul(hn, w_rest, l, tm=MM_TM_BIG, trans_b=True).reshape(b, t, ZR_W)
    out_a, ka = _attention(zm3, pos0, kv_prefix, l)
    o_b, bonus, gate, wkv_new = _rwkv_mix(zm3, zr3, shift_prev, wkv0, wl['rwkv'])
    out_c = _gmlp(zr3, w['w_s'][l], w['b_s'][l])
    out_d = _pool(zr3, pool_prefix, pos0, wl['w_pool'], w['pool_scale'][l])
    r2 = lambda a: a.reshape(m, -1)
    mix = _mix(r2(out_a), r2(o_b), r2(bonus), r2(gate), r2(out_c), r2(out_d), w['g_out_a'][l], w['ln_x_w'][l],
               w['ln_x_b'][l], w['g_out_c'][l], w['g_out_d'][l], wl['rwkv']['ones_bd'])
    h = mm(mix, 'w_out', D_MODEL, MM_TM_BIG, res=h, scale=1.0)
    hid = up(_rmsnorm(h, w['ln_ffn2'][l], BF16), 'w2_gate', 'w2_up')
    y = mm(hid, 'w2_down', D_MODEL, 512, res=h, scale=0.5)
    hs = (b, t, N_HEADS_A, HEAD_DIM_A)
    va = zm3[..., ZM_VA:ZM_VA + A_W]
    shift_new = jnp.concatenate([zm3[:, -1, ZM_RKV:], zr3[:, -1, ZR_LORA:ZR_LORA + LORA_ALL]], axis=-1)
    p = zr3[..., ZR_D:ZR_D + D_W]
    pool_new = jnp.concatenate([pool_prefix, p], axis=1)[:, -POOL_PREV:]
    vc = zr3[..., ZR_CV:ZR_CV + C_W]
    return y.reshape(b, t, D_MODEL), (ka.reshape(hs), va.reshape(hs), wkv_new, shift_new, pool_new, vc), wb


def kernel(x_prompt, x_sample, cache_k_swa, cache_v_swa, state_rwkv_wkv, state_rwkv_shift, state_pool, ln_ffn1, w1_gate, w1_up, w1_down, ln_mix, w_in, g_out_a, mu_b, w0, w_up, a0, a_up, g_up, k_k, k_a, r_k, ln_x_w, ln_x_b, w_s, b_s, g_out_c, w_pool, pool_scale, g_out_d, w_out, ln_ffn2, w2_gate, w2_up, w2_down, ln_final):
    w = dict(ln_ffn1=ln_ffn1, w1_gate=w1_gate, w1_up=w1_up, w1_down=w1_down, ln_mix=ln_mix, w_in=w_in,
             g_out_a=g_out_a, mu_b=mu_b, w0=w0, w_up=w_up, a0=a0, a_up=a_up, g_up=g_up, k_k=k_k, k_a=k_a,
             r_k=r_k, ln_x_w=ln_x_w, ln_x_b=ln_x_b, w_s=w_s, b_s=b_s, g_out_c=g_out_c, w_pool=w_pool,
             pool_scale=pool_scale, g_out_d=g_out_d, w_out=w_out, ln_ffn2=ln_ffn2, w2_gate=w2_gate,
             w2_up=w2_up, w2_down=w2_down)
    nbp, t_prompt, _ = x_prompt.shape
    yp, ys = x_prompt, x_sample
    p_states, s_states = [], []
    w['w_in_t'] = jnp.swapaxes(w_in, 1, 2)
    w_rest = _prep_rest_weights(w['w_in_t'])
    for l in range(DEPTH):
        wl = _prep_layer_weights(l, w)
        ys, ss, wb = _layer(ys, l, w, w_rest, wl, None, PAST_LEN, (cache_k_swa, cache_v_swa),
                            state_rwkv_shift[l], state_rwkv_wkv[l], state_pool[l])
        yp, sp, _ = _layer(yp, l, w, w_rest, wl, wb, 0, None,
                           jnp.zeros((nbp, B_FEAT), F32),
                           jnp.zeros((nbp, N_HEADS_B, HEAD_DIM_B, HEAD_DIM_B), F32),
                           jnp.zeros((nbp, POOL_PREV, D_W), F32))
        p_states.append(sp)
        s_states.append(ss)
    keep = min(WIN_MAX, t_prompt)
    stack = lambda states, i: jnp.stack([s[i] for s in states])
    y_prompt = _rmsnorm(yp.reshape(-1, D_MODEL), ln_final, F32).reshape(yp.shape)
    y_sample = _rmsnorm(ys.reshape(-1, D_MODEL), ln_final, F32).reshape(ys.shape)
    return (y_prompt, y_sample,
            jnp.stack([s[0][:, -keep:] for s in p_states]), jnp.stack([s[1][:, -keep:] for s in p_states]),
            stack(p_states, 2), stack(p_states, 3), stack(p_states, 4),
            stack(s_states, 0), stack(s_states, 1), stack(s_states, 2), stack(s_states, 3), stack(s_states, 4),
            stack(s_states, 5))
```

```python
import functools

import jax
import jax.numpy as jnp
from jax import lax
from jax.experimental import pallas as pl
from jax.experimental.pallas import tpu as pltpu

F32 = jnp.float32
BF16 = jnp.bfloat16

D_MODEL = 4096
DEPTH = 2
PAST_LEN = 16384
A_W = B_W = C_W = D_W = D_MODEL // 4
RMS_EPS = 1e-6
HEAD_DIM_A = 128
N_HEADS_A = A_W // HEAD_DIM_A
DILATED_BRANCHES = ((128, 1), (512, 4), (2048, 16))
WIN_MAX = 2048
ROPE_THETA = 10000.0
HEAD_DIM_B = 64
N_HEADS_B = B_W // HEAD_DIM_B
LORA_W, LORA_A, LORA_G = 64, 64, 160
LORA_ALL = LORA_W + LORA_A + LORA_G
GN_EPS = 64e-5
B_FEAT = 3 * B_W + LORA_ALL
CHUNK = 128
N_GROUPS_C = 8
POOL_WINDOWS = (2, 4, 8, 16)
POOL_PREV = max(POOL_WINDOWS) - 1
POOL_GROUP = D_W // len(POOL_WINDOWS)
OFF_B = 3 * A_W
OFF_C = OFF_B + B_FEAT
OFF_D = OFF_C + 2 * C_W

LANES = 128
SUBLANES = 8
VMEM_BYTES_V7X = 64 * 2**20
VMEM_INTERNAL_RESERVE = 12 * 2**20

MXU_COLS_V7X = 256
MM_TN = 2 * MXU_COLS_V7X

ZM_QA, ZM_KA, ZM_VA = 0, A_W, 2 * A_W
ZM_RKV = 3 * A_W
ZM_W = ZM_RKV + 3 * B_W
ZR_CU = 0
ZR_CV = ZR_CU + C_W
ZR_D = ZR_CV + C_W
ZR_LORA = ZR_D + D_W
LORA_PAD = 3 * LANES
ZR_W = -(-(ZR_LORA + LORA_PAD) // MM_TN) * MM_TN

NEG = -1e30


def _vmem_limit(block_bytes, scratch_bytes=0):
    need = 2 * block_bytes + scratch_bytes + VMEM_INTERNAL_RESERVE
    return int(min(max(need, 16 * 2**20), VMEM_BYTES_V7X - 4 * 2**20))


def _params(sem, block_bytes, scratch_bytes=0):
    return pltpu.CompilerParams(dimension_semantics=sem,
                                vmem_limit_bytes=_vmem_limit(block_bytes, scratch_bytes))


def _nbytes(shape, dtype):
    n = 1
    for s in shape:
        n *= s
    return n * jnp.dtype(dtype).itemsize


def _rmsnorm_body(x_ref, g_ref, o_ref):
    x = x_ref[...]
    ms = jnp.mean(x * x, axis=-1, keepdims=True)
    o_ref[...] = (x * lax.rsqrt(ms + RMS_EPS) * g_ref[...]).astype(o_ref.dtype)


def _rmsnorm(x, g, out_dtype):
    m, d = x.shape
    tm = min(m, 256)
    blk = _nbytes((tm, d), F32) + _nbytes((tm, d), out_dtype)
    return pl.pallas_call(
        _rmsnorm_body, grid=(m // tm,),
        in_specs=[pl.BlockSpec((tm, d), lambda i: (i, 0)), pl.BlockSpec((1, d), lambda i: (0, 0))],
        out_specs=pl.BlockSpec((tm, d), lambda i: (i, 0)),
        out_shape=jax.ShapeDtypeStruct((m, d), out_dtype),
        compiler_params=_params(("parallel",), blk), name="rmsnorm")(x, g.reshape(1, d))


def _ffn_up_body(x_ref, wg_ref, wu_ref, o_ref):
    x = x_ref[...]
    g = jnp.dot(x, wg_ref[...], preferred_element_type=F32)
    u = jnp.dot(x, wu_ref[...], preferred_element_type=F32)
    o_ref[...] = (g * jax.nn.sigmoid(g) * u).astype(o_ref.dtype)


def _ffn_up_cast_body(x_ref, wg_ref, wu_ref, o_ref, wgo_ref, wuo_ref):
    wg = wg_ref[...].astype(BF16)
    wu = wu_ref[...].astype(BF16)
    wgo_ref[...] = wg
    wuo_ref[...] = wu
    x = x_ref[...]
    g = jnp.dot(x, wg, preferred_element_type=F32)
    u = jnp.dot(x, wu, preferred_element_type=F32)
    o_ref[...] = (g * jax.nn.sigmoid(g) * u).astype(o_ref.dtype)


def _ffn_up(x, wg, wu):
    m, k = x.shape
    n = wg.shape[1]
    tm = min(m, 1024)
    tn = MXU_COLS_V7X
    blk = _nbytes((tm, k), BF16) + 2 * _nbytes((k, tn), BF16) + _nbytes((tm, tn), BF16)
    return pl.pallas_call(
        _ffn_up_body, grid=(m // tm, n // tn),
        in_specs=[pl.BlockSpec((tm, k), lambda i, j: (i, 0)),
                  pl.BlockSpec((k, tn), lambda i, j: (0, j)),
                  pl.BlockSpec((k, tn), lambda i, j: (0, j))],
        out_specs=pl.BlockSpec((tm, tn), lambda i, j: (i, j)),
        out_shape=jax.ShapeDtypeStruct((m, n), BF16),
        compiler_params=_params(("parallel", "arbitrary"), blk), name="ffn_up")(x, wg, wu)


def _ffn_up_cast(x, wg, wu, l):
    m, k = x.shape
    n = wg.shape[2]
    tn = MXU_COLS_V7X
    blk = _nbytes((m, k), BF16) + 2 * _nbytes((k, tn), F32) + 2 * _nbytes((k, tn), BF16) + _nbytes((m, tn), BF16)
    wspec = pl.BlockSpec((None, k, tn), lambda j: (l, 0, j))
    ospec = pl.BlockSpec((k, tn), lambda j: (0, j))
    return pl.pallas_call(
        _ffn_up_cast_body, grid=(n // tn,),
        in_specs=[pl.BlockSpec((m, k), lambda j: (0, 0)), wspec, wspec],
        out_specs=[pl.BlockSpec((m, tn), lambda j: (0, j)), ospec, ospec],
        out_shape=[jax.ShapeDtypeStruct((m, n), BF16), jax.ShapeDtypeStruct((k, n), BF16),
                   jax.ShapeDtypeStruct((k, n), BF16)],
        compiler_params=_params(("parallel",), blk), name="ffn_up_cast")(x, wg, wu)


def _mm_body(a_ref, b_ref, o_ref):
    o_ref[...] = jnp.dot(a_ref[...], b_ref[...], preferred_element_type=F32)


def _mm_res_body(a_ref, b_ref, r_ref, o_ref, *, scale):
    acc = jnp.dot(a_ref[...], b_ref[...], preferred_element_type=F32)
    o_ref[...] = r_ref[...] + scale * acc


_TRANS_B = (((1,), (1,)), ((), ()))


def _mm_t_body(a_ref, bt_ref, o_ref):
    o_ref[...] = lax.dot_general(a_ref[...], bt_ref[...], _TRANS_B, preferred_element_type=F32)


def _mm_cast_body(a_ref, b_ref, *rest, scale, has_res, trans_b):
    r_ref = rest[0] if has_res else None
    o_ref, bo_ref = rest[-2:]
    b = b_ref[...].astype(BF16)
    bo_ref[...] = b
    if trans_b:
        acc = lax.dot_general(a_ref[...], b, _TRANS_B, preferred_element_type=F32)
    else:
        acc = jnp.dot(a_ref[...], b, preferred_element_type=F32)
    o_ref[...] = r_ref[...] + scale * acc if has_res else acc


def _matmul(a, b, l=None, res=None, scale=1.0, tm=1024, trans_b=False):
    m, k = a.shape
    n = b.shape[-2] if trans_b else b.shape[-1]
    tm = min(m, tm)
    tn = MM_TN
    assert m % tm == 0 and n % tn == 0
    blk = _nbytes((tm, k), BF16) + _nbytes((k, tn), BF16) + _nbytes((tm, tn), F32)
    bshape, bidx = ((tn, k), lambda i, j: (j, 0)) if trans_b else ((k, tn), lambda i, j: (0, j))
    bspec = (pl.BlockSpec(bshape, bidx) if b.ndim == 2
             else pl.BlockSpec((None,) + bshape, lambda i, j: (l,) + bidx(i, j)))
    in_specs = [pl.BlockSpec((tm, k), lambda i, j: (i, 0)), bspec]
    args = [a, b]
    if trans_b:
        assert res is None
        body = _mm_t_body
    elif res is None:
        body = _mm_body
    else:
        body = functools.partial(_mm_res_body, scale=scale)
        in_specs.append(pl.BlockSpec((tm, tn), lambda i, j: (i, j)))
        args.append(res)
        blk += _nbytes((tm, tn), F32)
    return pl.pallas_call(
        body, grid=(m // tm, n // tn), in_specs=in_specs,
        out_specs=pl.BlockSpec((tm, tn), lambda i, j: (i, j)),
        out_shape=jax.ShapeDtypeStruct((m, n), F32),
        compiler_params=_params(("parallel", "arbitrary"), blk), name="matmul")(*args)


CAST_TILE_BYTES = 6 * 2**20


def _matmul_cast(a, b, l, n, res=None, scale=1.0, trans_b=False):
    m, k = a.shape
    tn = MM_TN
    while _nbytes((k, tn), F32) > CAST_TILE_BYTES:
        tn //= 2
    assert n % tn == 0 and tn % LANES == 0
    blk = _nbytes((m, k), BF16) + _nbytes((k, tn), F32) + _nbytes((k, tn), BF16) + 2 * _nbytes((m, tn), F32)
    if trans_b:
        wspec = pl.BlockSpec((None, tn, k), lambda j: (l, j, 0))
        cspec, cshape = pl.BlockSpec((tn, k), lambda j: (j, 0)), (n, k)
    else:
        wspec = pl.BlockSpec((None, k, tn), lambda j: (l, 0, j))
        cspec, cshape = pl.BlockSpec((k, tn), lambda j: (0, j)), (k, n)
    in_specs = [pl.BlockSpec((m, k), lambda j: (0, 0)), wspec]
    args = [a, b]
    if res is not None:
        in_specs.append(pl.BlockSpec((m, tn), lambda j: (0, j)))
        args.append(res)
    return pl.pallas_call(
        functools.partial(_mm_cast_body, scale=scale, has_res=res is not None, trans_b=trans_b), grid=(n // tn,),
        in_specs=in_specs, out_specs=[pl.BlockSpec((m, tn), lambda j: (0, j)), cspec],
        out_shape=[jax.ShapeDtypeStruct((m, n), F32), jax.ShapeDtypeStruct(cshape, BF16)],
        compiler_params=_params(("parallel",), blk), name="matmul_cast")(*args)


def _rope(x, cos, sin_signed):
    return x * cos + pltpu.roll(x, HEAD_DIM_A // 2, 1) * sin_signed


def _branch_multiplicity(delta):
    c = jnp.zeros(delta.shape, F32)
    for window, dilation in DILATED_BRANCHES:
        assert dilation & (dilation - 1) == 0
        hit = jnp.where(delta <= window, 1.0, 0.0)
        if dilation > 1:
            hit = jnp.where((delta & (dilation - 1)) == 0, hit, 0.0)
        c = c + hit
    return jnp.where(delta >= 0, c, 0.0)


def _attn_prompt_body(q_ref, k_ref, v_ref, cos_ref, sin_ref, o_ref, kout_ref, qs, ks, vs, ctab, *, t_len, tq):
    nq = t_len // tq

    @pl.when((pl.program_id(0) == 0) & (pl.program_id(1) == 0))
    def _():
        rel = (lax.broadcasted_iota(jnp.int32, (tq, tq), 0) - lax.broadcasted_iota(jnp.int32, (tq, tq), 1))
        for d in range(nq):
            ctab[:, d * tq:(d + 1) * tq] = _branch_multiplicity(rel + (nq - 1 - d) * tq)

    cos = cos_ref[...]
    sin = sin_ref[...]
    k = _rope(k_ref[0], cos, sin)
    kout_ref[0] = k
    ks[...] = k.astype(BF16)
    qs[...] = (_rope(q_ref[0], cos, sin) * (HEAD_DIM_A ** -0.5)).astype(BF16)
    vs[...] = v_ref[0].astype(BF16)
    for i in range(nq):
        kw = (i + 1) * tq
        s = lax.dot_general(qs[i * tq:(i + 1) * tq, :], ks[0:kw, :], _TRANS_B, preferred_element_type=F32)
        c = ctab[:, (nq - 1 - i) * tq:]
        sm = jnp.where(c > 0.0, s, NEG)
        p = jnp.exp(sm - jnp.max(sm, axis=-1, keepdims=True)) * c
        l = jnp.sum(p, axis=-1, keepdims=True)
        acc = jnp.dot(p.astype(BF16), vs[0:kw, :], preferred_element_type=F32)
        o_ref[0, i * tq:(i + 1) * tq, :] = acc / l


def _attn_sample_body(q_ref, k_ref, v_ref, kp_ref, vp_ref, cos_ref, sin_ref, o_ref, kout_ref, *, t_len, n_prev):
    cos = cos_ref[...]
    sin = sin_ref[...]
    d1 = (n_prev + lax.broadcasted_iota(jnp.int32, (t_len, n_prev), 0)
          - lax.broadcasted_iota(jnp.int32, (t_len, n_prev), 1))
    c1 = _branch_multiplicity(d1)
    d2 = (lax.broadcasted_iota(jnp.int32, (t_len, LANES), 0) - lax.broadcasted_iota(jnp.int32, (t_len, LANES), 1))
    c2 = _branch_multiplicity(d2)
    pad = jnp.zeros((LANES - t_len, HEAD_DIM_A), F32)
    for h in range(N_HEADS_A):
        sl = slice(h * HEAD_DIM_A, (h + 1) * HEAD_DIM_A)
        k = _rope(k_ref[0, :, sl], cos, sin)
        kout_ref[0, :, sl] = k
        q = (_rope(q_ref[0, :, sl], cos, sin) * (HEAD_DIM_A ** -0.5)).astype(BF16)
        kn = jnp.concatenate([k, pad], axis=0).astype(BF16)
        vn = jnp.concatenate([v_ref[0, :, sl], pad], axis=0).astype(BF16)
        s1 = lax.dot_general(q, kp_ref[:, h, :].astype(BF16), _TRANS_B, preferred_element_type=F32)
        s2 = lax.dot_general(q, kn, _TRANS_B, preferred_element_type=F32)
        sm1 = jnp.where(c1 > 0.0, s1, NEG)
        sm2 = jnp.where(c2 > 0.0, s2, NEG)
        m = jnp.maximum(jnp.max(sm1, axis=-1, keepdims=True), jnp.max(sm2, axis=-1, keepdims=True))
        p1 = jnp.exp(sm1 - m) * c1
        p2 = jnp.exp(sm2 - m) * c2
        l = jnp.sum(p1, axis=-1, keepdims=True) + jnp.sum(p2, axis=-1, keepdims=True)
        acc = (jnp.dot(p1.astype(BF16), vp_ref[:, h, :].astype(BF16), preferred_element_type=F32)
               + jnp.dot(p2.astype(BF16), vn, preferred_element_type=F32))
        o_ref[0, :, sl] = acc / l


def _rope_tables(pos0, t_len):
    half = HEAD_DIM_A // 2
    inv = ROPE_THETA ** (-jnp.arange(half, dtype=F32) / half)
    ang = (pos0 + jnp.arange(t_len)).astype(F32)[:, None] * inv[None, :]
    cos = jnp.cos(ang)
    sin = jnp.sin(ang)
    return jnp.concatenate([cos, cos], axis=-1), jnp.concatenate([-sin, sin], axis=-1)


def _attention(z3, pos0, kv_prefix, l):
    b, t, _ = z3.shape
    cos, sin = _rope_tables(pos0, t)
    hd = HEAD_DIM_A
    out_shape = [jax.ShapeDtypeStruct((b, t, A_W), F32), jax.ShapeDtypeStruct((b, t, A_W), F32)]
    if kv_prefix is None:
        col = lambda base: (lambda bi, hi: (bi, 0, base // hd + hi))
        tab = pl.BlockSpec((t, hd), lambda bi, hi: (0, 0))
        zspecs = [pl.BlockSpec((1, t, hd), col(ZM_QA)), pl.BlockSpec((1, t, hd), col(ZM_KA)),
                  pl.BlockSpec((1, t, hd), col(ZM_VA))]
        out_specs = [pl.BlockSpec((1, t, hd), col(0)), pl.BlockSpec((1, t, hd), col(0))]
        blk = 7 * _nbytes((t, hd), F32)
        tq = min(t, 256)
        body = functools.partial(_attn_prompt_body, t_len=t, tq=tq)
        scratch = 3 * _nbytes((t, hd), BF16) + 5 * _nbytes((tq, t), F32)
        return pl.pallas_call(
            body, grid=(b, N_HEADS_A), in_specs=zspecs + [tab, tab], out_specs=out_specs, out_shape=out_shape,
            scratch_shapes=[pltpu.VMEM((t, hd), BF16)] * 3 + [pltpu.VMEM((tq, t), F32)],
            compiler_params=_params(("arbitrary", "arbitrary"), blk, scratch),
            name="attn_prompt")(z3, z3, z3, cos, sin)
    k_prev, v_prev = kv_prefix
    n_prev = k_prev.shape[2]
    assert t <= LANES and n_prev % LANES == 0 and k_prev.shape[3:] == (N_HEADS_A, hd)
    col = lambda base: (lambda bi: (bi, 0, base // A_W))
    tab = pl.BlockSpec((t, hd), lambda bi: (0, 0))
    zspecs = [pl.BlockSpec((1, t, A_W), col(ZM_QA)), pl.BlockSpec((1, t, A_W), col(ZM_KA)),
              pl.BlockSpec((1, t, A_W), col(ZM_VA))]
    out_specs = [pl.BlockSpec((1, t, A_W), col(0)), pl.BlockSpec((1, t, A_W), col(0))]
    pspec = pl.BlockSpec((None, None, n_prev, N_HEADS_A, hd), lambda bi: (l, bi, 0, 0, 0))
    blk = 5 * _nbytes((t, A_W), F32) + 2 * _nbytes((n_prev, A_W), F32)
    body = functools.partial(_attn_sample_body, t_len=t, n_prev=n_prev)
    return pl.pallas_call(
        body, grid=(b,), in_specs=zspecs + [pspec, pspec, tab, tab], out_specs=out_specs,
        out_shape=out_shape, compiler_params=_params(("parallel",), blk),
        name="attn_sample")(z3, z3, z3, k_prev, v_prev, cos, sin)


def _group_sum(x, ones_blockdiag):
    outs = []
    for j in range(x.shape[-1] // LANES):
        outs.append(jnp.dot(x[:, j * LANES:(j + 1) * LANES], ones_blockdiag, preferred_element_type=F32,
                            precision=lax.Precision.HIGHEST))
    return jnp.concatenate(outs, axis=-1)


def _softplus(y):
    return jnp.maximum(y, 0.0) + jnp.log1p(jnp.exp(-jnp.abs(y)))


def _rwkv_prep_body(x_ref, lo_ref, sx_ref, slo_ref, mux_ref, mulo_ref, w0_ref, a0_ref, kkp_ref, kap_ref, rk_ref,
                    wup_ref, aup_ref, gup_ref, ones_ref,
                    r_o, w_o, k_o, v_o, kk_o, b_o, g_o, bonus_o, last_x, last_lo, *, tt, channel_major):
    @pl.when(pl.program_id(1) == 0)
    def _():
        last_x[0:1, :] = sx_ref[0]
        last_lo[0:1, :] = slo_ref[0]

    x = x_ref[0]
    lo = lo_ref[0]
    first = lax.broadcasted_iota(jnp.int32, (tt, 1), 0) == 0
    px = jnp.where(first, last_x[0:1, :], pltpu.roll(x, 1, 0))
    plo = jnp.where(first, last_lo[0:1, :], pltpu.roll(lo, 1, 0))
    last_x[0:1, :] = x[tt - 1:tt, :]
    last_lo[0:1, :] = lo[tt - 1:tt, :]
    fx = x + mux_ref[...] * (px - x)
    flo = lo + mulo_ref[...] * (plo - lo)
    r = fx[:, :B_W]
    k = fx[:, B_W:2 * B_W]
    v = fx[:, 2 * B_W:]
    zwa = flo[:, :LANES]
    zg = flo[:, LANES:]
    ones_bd = ones_ref[...]
    wl = w0_ref[...] + jnp.dot(jnp.tanh(zwa).astype(BF16), wup_ref[...], preferred_element_type=F32)
    w_log = -_softplus(-wl) - 0.5
    decay = jnp.exp(-jnp.exp(w_log))
    a = jax.nn.sigmoid(a0_ref[...] + jnp.dot(zwa.astype(BF16), aup_ref[...], preferred_element_type=F32))
    g = jnp.dot(jax.nn.sigmoid(zg).astype(BF16), gup_ref[...], preferred_element_type=F32)
    kk = k * kkp_ref[...]
    kk = kk / jnp.maximum(jnp.sqrt(_group_sum(kk * kk, ones_bd)), 1e-12)
    kmod = k * (1.0 + (a - 1.0) * kap_ref[...])
    lay = (lambda y: y.T) if channel_major else (lambda y: y)
    r_o[0] = lay(r)
    w_o[0] = lay(decay)
    k_o[0] = lay(kmod)
    v_o[0] = lay(v)
    kk_o[0] = lay(kk)
    b_o[0] = lay(kk * a)
    g_o[0] = g
    bonus_o[0] = _group_sum(r * kmod * rk_ref[...], ones_bd) * v


def _rwkv_prep(zm3, zr3, shift_prev, p, channel_major):
    b, t, _ = zm3.shape
    tt = min(t, 256)
    sx = shift_prev[:, None, :3 * B_W]
    slo = jnp.pad(shift_prev[:, None, 3 * B_W:], ((0, 0), (0, 0), (0, LORA_PAD - LORA_ALL)))
    row = lambda w: pl.BlockSpec((1, w), lambda bi, ti: (0, 0))
    full = lambda a: pl.BlockSpec(a.shape, lambda bi, ti: (0,) * a.ndim)
    in_specs = [pl.BlockSpec((1, tt, 3 * B_W), lambda bi, ti: (bi, ti, ZM_RKV // (3 * B_W))),
                pl.BlockSpec((1, tt, LORA_PAD), lambda bi, ti: (bi, ti, ZR_LORA // LORA_PAD)),
                pl.BlockSpec((1, 1, 3 * B_W), lambda bi, ti: (bi, 0, 0)),
                pl.BlockSpec((1, 1, LORA_PAD), lambda bi, ti: (bi, 0, 0)),
                row(3 * B_W), row(LORA_PAD), row(B_W), row(B_W), row(B_W), row(B_W), row(B_W),
                full(p['w_up']), full(p['a_up']), full(p['g_up']), full(p['ones_bd'])]
    ospec = pl.BlockSpec((1, tt, B_W), lambda bi, ti: (bi, ti, 0))
    oshape = jax.ShapeDtypeStruct((b, t, B_W), F32)
    if channel_major:
        rspec, rshape = pl.BlockSpec((1, B_W, tt), lambda bi, ti: (bi, 0, ti)), jax.ShapeDtypeStruct((b, B_W, t), F32)
    else:
        rspec, rshape = ospec, oshape
    blk = _nbytes((tt, 3 * B_W + LORA_PAD), F32) + 8 * _nbytes((tt, B_W), F32) + 2 * _nbytes((512, B_W), F32)
    return pl.pallas_call(
        functools.partial(_rwkv_prep_body, tt=tt, channel_major=channel_major), grid=(b, t // tt),
        in_specs=in_specs, out_specs=[rspec] * 6 + [ospec] * 2, out_shape=[rshape] * 6 + [oshape] * 2,
        scratch_shapes=[pltpu.VMEM((SUBLANES, 3 * B_W), F32), pltpu.VMEM((SUBLANES, LORA_PAD), F32)],
        compiler_params=_params(("parallel", "arbitrary"), blk), name="rwkv_prep")(
            zm3, zr3, sx, slo, p['mu_x'], p['mu_lo'], p['w0'], p['a0'], p['k_k'], p['k_a'], p['r_k'],
            p['w_up'], p['a_up'], p['g_up'], p['ones_bd'])


N_PARTIAL = 4


def _rwkv_steps(r_ref, w_ref, k_ref, kk_ref, b_ref, v_ref, o_ref, state, *, tc, vp, time_major):
    def step(t, row):
        parts = [jnp.zeros((vp, LANES), F32)] * N_PARTIAL
        for k in range(HEAD_DIM_B):
            parts[k % N_PARTIAL] = parts[k % N_PARTIAL] + state[k] * row(kk_ref, k)
        sa = -((parts[0] + parts[1]) + (parts[2] + parts[3]))
        vt = v_ref[t]
        parts = [jnp.zeros((vp, LANES), F32)] * N_PARTIAL
        for k in range(HEAD_DIM_B):
            s = state[k] * row(w_ref, k) + sa * row(b_ref, k) + vt * row(k_ref, k)
            state[k] = s
            parts[k % N_PARTIAL] = parts[k % N_PARTIAL] + s * row(r_ref, k)
        o_ref[t] = (parts[0] + parts[1]) + (parts[2] + parts[3])

    if time_major:
        def one(t, carry):
            step(t, lambda ref, k: ref[t, pl.ds(k, 1), :])
            return carry

        lax.fori_loop(0, tc, one, 0)
    else:
        def eight(tb, carry):
            for s in range(SUBLANES):
                step(tb * SUBLANES + s, lambda ref, k, s=s: ref[k, tb, pl.ds(s, 1), :])
            return carry

        lax.fori_loop(0, tc // SUBLANES, eight, 0)


def _rwkv_scan_body(r_ref, w_ref, k_ref, kk_ref, b_ref, v_ref, s0_ref, o_ref, sout_ref, state, *, tc, vp):
    @pl.when(pl.program_id(0) == 0)
    def _():
        state[...] = s0_ref[...]

    _rwkv_steps(r_ref, w_ref, k_ref, kk_ref, b_ref, v_ref, o_ref, state, tc=tc, vp=vp, time_major=True)

    @pl.when(pl.program_id(0) == pl.num_programs(0) - 1)
    def _():
        sout_ref[...] = state[...]


def _rwkv_scan_rows_body(r_ref, w_ref, k_ref, kk_ref, b_ref, v_ref, s0_ref, o_ref, sout_ref,
                         state, yt, rk, wk, kkey, kkk, bk, vk, ok, *, tc, nb, dup):
    vp = HEAD_DIM_B // dup
    nh = N_HEADS_B

    @pl.when(pl.program_id(0) == 0)
    def _():
        state[...] = s0_ref[...]

    def to_lanes(x_ref, store, n_rows, row_of):
        for j in range(n_rows):
            pieces = [x_ref[b, pl.ds(row_of(j, vh), nh, stride=HEAD_DIM_B), :] for b in range(nb) for vh in range(dup)]
            store(j, jnp.concatenate(pieces, axis=0).T)

    def key_store(dst):
        def store(j, x):
            dst[j] = x.reshape(tc // SUBLANES, SUBLANES, LANES)
        return store

    def val_store(j, x):
        vk[:, j, :] = x

    for x_ref, dst in ((r_ref, rk), (w_ref, wk), (k_ref, kkey), (kk_ref, kkk), (b_ref, bk)):
        to_lanes(x_ref, key_store(dst), HEAD_DIM_B, lambda j, vh: j)
    to_lanes(v_ref, val_store, vp, lambda j, vh: vh * vp + j)

    _rwkv_steps(rk, wk, kkey, kkk, bk, vk, ok, state, tc=tc, vp=vp, time_major=False)

    for j in range(vp):
        m = ok[:, j, :].T
        for b in range(nb):
            for vh in range(dup):
                lane0 = (b * dup + vh) * nh
                yt[b, pl.ds(vh * vp + j, nh, stride=HEAD_DIM_B), :] = m[lane0:lane0 + nh, :]
    for b in range(nb):
        o_ref[b] = yt[b].T

    @pl.when(pl.program_id(0) == pl.num_programs(0) - 1)
    def _():
        sout_ref[...] = state[...]


def _rwkv_scan_rows(r, w, k, kk, bv, v, s0, dup):
    nb, _, t = r.shape
    vp = HEAD_DIM_B // dup
    tc = LANES
    assert t % tc == 0 and nb * dup * N_HEADS_B == LANES
    cspec = pl.BlockSpec((nb, B_W, tc), lambda i: (0, 0, i))
    xspec = pl.BlockSpec((nb, tc, B_W), lambda i: (0, i, 0))
    sspec = pl.BlockSpec((HEAD_DIM_B, vp, LANES), lambda i: (0, 0, 0))
    key_tile = pltpu.VMEM((HEAD_DIM_B, tc // SUBLANES, SUBLANES, LANES), F32)
    val_tile = pltpu.VMEM((tc, vp, LANES), F32)
    blk = 7 * _nbytes((nb, tc, B_W), F32) + 2 * _nbytes((HEAD_DIM_B, vp, LANES), F32)
    scr = (_nbytes((HEAD_DIM_B, vp, LANES), F32) + _nbytes((nb, B_W, tc), F32)
           + 5 * _nbytes((tc, HEAD_DIM_B, LANES), F32) + 2 * _nbytes((tc, vp, LANES), F32))
    return pl.pallas_call(
        functools.partial(_rwkv_scan_rows_body, tc=tc, nb=nb, dup=dup), grid=(t // tc,),
        in_specs=[cspec] * 6 + [sspec], out_specs=[xspec, sspec],
        out_shape=[jax.ShapeDtypeStruct((nb, t, B_W), F32), jax.ShapeDtypeStruct((HEAD_DIM_B, vp, LANES), F32)],
        scratch_shapes=[pltpu.VMEM((HEAD_DIM_B, vp, LANES), F32), pltpu.VMEM((nb, B_W, tc), F32)]
        + [key_tile] * 5 + [val_tile] * 2,
        compiler_params=_params(("arbitrary",), blk, scr), name="rwkv_scan_rows")(r, w, k, kk, bv, v, s0)


def _rwkv_scan(r, w, k, kk, bv, v, s0):
    t, vp, _ = v.shape
    tc = min(t, 64)
    kspec = pl.BlockSpec((tc, HEAD_DIM_B, LANES), lambda i: (i, 0, 0))
    vspec = pl.BlockSpec((tc, vp, LANES), lambda i: (i, 0, 0))
    sspec = pl.BlockSpec((HEAD_DIM_B, vp, LANES), lambda i: (0, 0, 0))
    blk = 5 * _nbytes((tc, HEAD_DIM_B, LANES), F32) + 2 * _nbytes((tc, vp, LANES), F32) \
        + 2 * _nbytes((HEAD_DIM_B, vp, LANES), F32)
    return pl.pallas_call(
        functools.partial(_rwkv_scan_body, tc=tc, vp=vp), grid=(t // tc,),
        in_specs=[kspec] * 5 + [vspec, sspec], out_specs=[vspec, sspec],
        out_shape=[jax.ShapeDtypeStruct((t, vp, LANES), F32), jax.ShapeDtypeStruct((HEAD_DIM_B, vp, LANES), F32)],
        scratch_shapes=[pltpu.VMEM((HEAD_DIM_B, vp, LANES), F32)],
        compiler_params=_params(("arbitrary",), blk, _nbytes((HEAD_DIM_B, vp, LANES), F32)),
        name="rwkv_scan")(r, w, k, kk, bv, v, s0)


def _rwkv_mix(zm3, zr3, shift_prev, wkv0, p):
    b, t, _ = zm3.shape
    nh, hd = N_HEADS_B, HEAD_DIM_B
    in_vmem_relayout = t % LANES == 0
    r, w, kmod, v, kk, bv, g, bonus = _rwkv_prep(zm3, zr3, shift_prev, p, channel_major=in_vmem_relayout)
    dup = LANES // (b * nh)
    assert dup * b * nh == LANES and hd % dup == 0
    vp = hd // dup

    def key_layout(x):
        y = x.reshape(b, t, nh, hd).transpose(1, 3, 0, 2)[:, :, :, None]
        return jnp.broadcast_to(y, (t, hd, b, dup, nh)).reshape(t, hd, LANES)

    def val_layout(x):
        return x.reshape(b, t, nh, dup, vp).transpose(1, 4, 0, 3, 2).reshape(t, vp, LANES)

    s0 = wkv0.reshape(b, nh, dup, vp, hd).transpose(4, 3, 0, 2, 1).reshape(hd, vp, LANES)
    if in_vmem_relayout:
        o, s = _rwkv_scan_rows(r, w, kmod, kk, bv, v, s0, dup)
    else:
        o, s = _rwkv_scan(key_layout(r), key_layout(w), key_layout(kmod), key_layout(kk), key_layout(bv),
                          val_layout(v), s0)
        o = o.reshape(t, vp, b, dup, nh).transpose(2, 0, 4, 3, 1).reshape(b, t, B_W)
    s = s.reshape(hd, vp, b, dup, nh).transpose(2, 4, 3, 1, 0).reshape(b, nh, hd, hd)
    return o, bonus, g, s


def _gmlp_body(u_ref, v_ref, ws_ref, b_ref, o_ref, *, tc):
    keep = (lax.broadcasted_iota(jnp.int32, (CHUNK, CHUNK), 1) <= lax.broadcasted_iota(jnp.int32, (CHUNK, CHUNK), 0))
    for g in range(N_GROUPS_C):
        sl = slice(g * CHUNK, (g + 1) * CHUNK)
        w = jnp.where(keep, ws_ref[g], 0.0).astype(BF16)
        v = v_ref[0, :, sl]
        if tc < CHUNK:
            v = jnp.concatenate([v, jnp.zeros((CHUNK - tc, CHUNK), F32)], axis=0)
        s = jnp.dot(w, v.astype(BF16), preferred_element_type=F32) + b_ref[g]
        o_ref[0, :, sl] = u_ref[0, :, sl] * s[:tc]


def _gmlp(z3, w_s, b_s):
    b, t, _ = z3.shape
    tc = min(t, CHUNK)
    assert t % tc == 0
    blk = 3 * _nbytes((tc, C_W), F32) + 2 * _nbytes((N_GROUPS_C, CHUNK, CHUNK), F32)
    return pl.pallas_call(
        functools.partial(_gmlp_body, tc=tc), grid=(b, t // tc),
        in_specs=[pl.BlockSpec((1, tc, C_W), lambda bi, ci: (bi, ci, ZR_CU // C_W)),
                  pl.BlockSpec((1, tc, C_W), lambda bi, ci: (bi, ci, ZR_CV // C_W)),
                  pl.BlockSpec((N_GROUPS_C, CHUNK, CHUNK), lambda bi, ci: (0, 0, 0)),
                  pl.BlockSpec((N_GROUPS_C, CHUNK, 1), lambda bi, ci: (0, 0, 0))],
        out_specs=pl.BlockSpec((1, tc, C_W), lambda bi, ci: (bi, ci, 0)),
        out_shape=jax.ShapeDtypeStruct((b, t, C_W), F32),
        compiler_params=_params(("parallel", "parallel"), blk), name="gmlp")(z3, z3, w_s, b_s[:, :, None])


POOL_HALO = 16


def _pool_body(p_ref, pre_ref, w_ref, sc_ref, o_ref, ext, *, t_len, pos0, tc):
    ext[0:POOL_HALO, :] = pre_ref[0]
    ext[POOL_HALO:POOL_HALO + t_len, :] = p_ref[0]
    for c0 in range(0, t_len, tc):
        pos = pos0 + c0 + lax.broadcasted_iota(jnp.int32, (tc, 1), 0)
        for g, win in enumerate(POOL_WINDOWS):
            sl = slice(g * POOL_GROUP, (g + 1) * POOL_GROUP)
            base = POOL_HALO + c0
            acc = ext[base:base + tc, sl]
            for i in range(1, win):
                acc = acc + ext[base - i:base - i + tc, sl]
            cnt = jnp.minimum(win, pos + 1).astype(F32)
            pooled = acc / cnt - p_ref[0, c0:c0 + tc, sl]
            y = jnp.dot(pooled.astype(BF16), w_ref[g], preferred_element_type=F32)
            o_ref[0, c0:c0 + tc, sl] = y * sc_ref[:, sl]


def _pool(z3, prefix, pos0, w_pool, scale):
    b, t, _ = z3.shape
    assert POOL_PREV < POOL_HALO
    pre = jnp.pad(prefix, ((0, 0), (POOL_HALO - POOL_PREV, 0), (0, 0)))
    tc = min(t, 256)
    blk = 2 * _nbytes((t, D_W), F32) + _nbytes((POOL_HALO, D_W), F32) + _nbytes(w_pool.shape, BF16)
    scr = _nbytes((t + POOL_HALO, D_W), F32)
    return pl.pallas_call(
        functools.partial(_pool_body, t_len=t, pos0=pos0, tc=tc), grid=(b,),
        in_specs=[pl.BlockSpec((1, t, D_W), lambda bi: (bi, 0, ZR_D // D_W)),
                  pl.BlockSpec((1, POOL_HALO, D_W), lambda bi: (bi, 0, 0)),
                  pl.BlockSpec(w_pool.shape, lambda bi: (0, 0, 0)),
                  pl.BlockSpec((1, D_W), lambda bi: (0, 0))],
        out_specs=pl.BlockSpec((1, t, D_W), lambda bi: (bi, 0, 0)),
        out_shape=jax.ShapeDtypeStruct((b, t, D_W), F32),
        scratch_shapes=[pltpu.VMEM((t + POOL_HALO, D_W), F32)],
        compiler_params=_params(("parallel",), blk, scr), name="pool")(z3, pre, w_pool, scale.reshape(1, D_W))


def _mix_body(oa_ref, ob_ref, bonus_ref, g_ref, oc_ref, od_ref, ga_ref, lnw_ref, lnb_ref, gc_ref, gd_ref, ones_ref,
              o_ref):
    def rms(x, gain):
        ms = jnp.mean(x * x, axis=-1, keepdims=True)
        return x * lax.rsqrt(ms + RMS_EPS) * gain

    ones_bd = ones_ref[...]
    o = ob_ref[...]
    mean = _group_sum(o, ones_bd) * (1.0 / HEAD_DIM_B)
    d = o - mean
    var = _group_sum(d * d, ones_bd) * (1.0 / HEAD_DIM_B)
    ob = d * lax.rsqrt(var + GN_EPS) * lnw_ref[...] + lnb_ref[...]
    ob = (ob + bonus_ref[...]) * g_ref[...]
    o_ref[:, 0:A_W] = rms(oa_ref[...], ga_ref[...]).astype(BF16)
    o_ref[:, A_W:A_W + B_W] = ob.astype(BF16)
    o_ref[:, A_W + B_W:A_W + B_W + C_W] = rms(oc_ref[...], gc_ref[...]).astype(BF16)
    o_ref[:, A_W + B_W + C_W:] = rms(od_ref[...], gd_ref[...]).astype(BF16)


def _mix(oa, ob, bonus, g, oc, od, ga, lnw, lnb, gc, gd, ones_bd):
    m = oa.shape[0]
    tm = min(m, 256)
    act = pl.BlockSpec((tm, A_W), lambda i: (i, 0))
    row = pl.BlockSpec((1, A_W), lambda i: (0, 0))
    blk = 6 * _nbytes((tm, A_W), F32) + _nbytes((tm, D_MODEL), BF16) + _nbytes((512, A_W), F32)
    r1 = lambda a: a.reshape(1, -1)
    return pl.pallas_call(
        _mix_body, grid=(m // tm,),
        in_specs=[act] * 6 + [row] * 5 + [pl.BlockSpec((LANES, LANES), lambda i: (0, 0))],
        out_specs=pl.BlockSpec((tm, D_MODEL), lambda i: (i, 0)),
        out_shape=jax.ShapeDtypeStruct((m, D_MODEL), BF16),
        compiler_params=_params(("parallel",), blk), name="mix")(
            oa, ob, bonus, g, oc, od, r1(ga), r1(lnw), r1(lnb), r1(gc), r1(gd), ones_bd)


def _prep_rest_weights(w_in_t):
    assert ZM_W == OFF_B + 3 * B_W
    zpad = jnp.zeros((DEPTH, ZR_W - ZR_LORA - LORA_ALL, D_MODEL), w_in_t.dtype)
    return jnp.concatenate([w_in_t[:, OFF_C:], w_in_t[:, ZM_W:OFF_C], zpad], axis=1).astype(BF16)


def _prep_layer_weights(l, w):
    mu = w['mu_b'][l]
    pad_rows = lambda a, before, total: jnp.pad(a, ((before, total - before - a.shape[0]), (0, 0))).astype(BF16)
    ones_bd = jnp.kron(jnp.eye(LANES // HEAD_DIM_B, dtype=F32), jnp.ones((HEAD_DIM_B, HEAD_DIM_B), F32))
    rw = dict(
        mu_x=mu[None, :3 * B_W], mu_lo=jnp.pad(mu[None, 3 * B_W:], ((0, 0), (0, LORA_PAD - LORA_ALL))),
        w0=w['w0'][l][None], a0=w['a0'][l][None], k_k=w['k_k'][l][None], k_a=w['k_a'][l][None],
        r_k=w['r_k'][l].reshape(1, B_W),
        w_up=pad_rows(w['w_up'][l], 0, LANES), a_up=pad_rows(w['a_up'][l], LORA_W, LANES),
        g_up=pad_rows(w['g_up'][l], 0, LORA_PAD - LANES), ones_bd=ones_bd)
    return dict(w_pool=w['w_pool'][l].astype(BF16), rwkv=rw)


def _layer(x, l, w, w_rest, wl, wb, pos0, kv_prefix, shift_prev, wkv0, pool_prefix):
    cast = wb is None
    wb = {} if cast else wb
    b, t, _ = x.shape
    m = b * t

    def up(xn, gate, upw):
        if not cast:
            return _ffn_up(xn, wb[gate], wb[upw])
        hid, wb[gate], wb[upw] = _ffn_up_cast(xn, w[gate], w[upw], l)
        return hid

    def mm(a, name, n, tm, res=None, scale=1.0, trans_b=False):
        if not cast:
            return _matmul(a, wb[name], res=res, scale=scale, tm=tm, trans_b=trans_b)
        out, wb[name] = _matmul_cast(a, w[name], l, n, res=res, scale=scale, trans_b=trans_b)
        return out

    x2 = x.reshape(m, D_MODEL)
    hid = up(_rmsnorm(x2, w['ln_ffn1'][l], BF16), 'w1_gate', 'w1_up')
    h = mm(hid, 'w1_down', D_MODEL, 512, res=x2, scale=0.5)
    hn = _rmsnorm(h, w['ln_mix'][l], BF16)
    zm3 = mm(hn, 'w_in_t', ZM_W, 1024, trans_b=True).reshape(b, t, ZM_W)
    zr3 = _matmul(hn, w_rest, l, trans_b=True).reshape(b, t, ZR_W)
    out_a, ka = _attention(zm3, pos0, kv_prefix, l)
    o_b, bonus, gate, wkv_new = _rwkv_mix(zm3, zr3, shift_prev, wkv0, wl['rwkv'])
    out_c = _gmlp(zr3, w['w_s'][l], w['b_s'][l])
    out_d = _pool(zr3, pool_prefix, pos0, wl['w_pool'], w['pool_scale'][l])
    r2 = lambda a: a.reshape(m, -1)
    mix = _mix(r2(out_a), r2(o_b), r2(bonus), r2(gate), r2(out_c), r2(out_d), w['g_out_a'][l], w['ln_x_w'][l],
               w['ln_x_b'][l], w['g_out_c'][l], w['g_out_d'][l], wl['rwkv']['ones_bd'])
    h = mm(mix, 'w_out', D_MODEL, 1024, res=h, scale=1.0)
    hid = up(_rmsnorm(h, w['ln_ffn2'][l], BF16), 'w2_gate', 'w2_up')
    y = mm(hid, 'w2_down', D_MODEL, 512, res=h, scale=0.5)
    hs = (b, t, N_HEADS_A, HEAD_DIM_A)
    va = zm3[..., ZM_VA:ZM_VA + A_W]
    shift_new = jnp.concatenate([zm3[:, -1, ZM_RKV:], zr3[:, -1, ZR_LORA:ZR_LORA + LORA_ALL]], axis=-1)
    p = zr3[..., ZR_D:ZR_D + D_W]
    pool_new = jnp.concatenate([pool_prefix, p], axis=1)[:, -POOL_PREV:]
    vc = zr3[..., ZR_CV:ZR_CV + C_W]
    return y.reshape(b, t, D_MODEL), (ka.reshape(hs), va.reshape(hs), wkv_new, shift_new, pool_new, vc), wb


def kernel(x_prompt, x_sample, cache_k_swa, cache_v_swa, state_rwkv_wkv, state_rwkv_shift, state_pool, ln_ffn1, w1_gate, w1_up, w1_down, ln_mix, w_in, g_out_a, mu_b, w0, w_up, a0, a_up, g_up, k_k, k_a, r_k, ln_x_w, ln_x_b, w_s, b_s, g_out_c, w_pool, pool_scale, g_out_d, w_out, ln_ffn2, w2_gate, w2_up, w2_down, ln_final):
    w = dict(ln_ffn1=ln_ffn1, w1_gate=w1_gate, w1_up=w1_up, w1_down=w1_down, ln_mix=ln_mix, w_in=w_in,
             g_out_a=g_out_a, mu_b=mu_b, w0=w0, w_up=w_up, a0=a0, a_up=a_up, g_up=g_up, k_k=k_k, k_a=k_a,
             r_k=r_k, ln_x_w=ln_x_w, ln_x_b=ln_x_b, w_s=w_s, b_s=b_s, g_out_c=g_out_c, w_pool=w_pool,
             pool_scale=pool_scale, g_out_d=g_out_d, w_out=w_out, ln_ffn2=ln_ffn2, w2_gate=w2_gate,
             w2_up=w2_up, w2_down=w2_down)
    nbp, t_prompt, _ = x_prompt.shape
    yp, ys = x_prompt, x_sample
    p_states, s_states = [], []
    w['w_in_t'] = jnp.swapaxes(w_in, 1, 2)
    w_rest = _prep_rest_weights(w['w_in_t'])
    for l in range(DEPTH):
        wl = _prep_layer_weights(l, w)
        ys, ss, wb = _layer(ys, l, w, w_rest, wl, None, PAST_LEN, (cache_k_swa, cache_v_swa),
                            state_rwkv_shift[l], state_rwkv_wkv[l], state_pool[l])
        yp, sp, _ = _layer(yp, l, w, w_rest, wl, wb, 0, None,
                           jnp.zeros((nbp, B_FEAT), F32),
                           jnp.zeros((nbp, N_HEADS_B, HEAD_DIM_B, HEAD_DIM_B), F32),
                           jnp.zeros((nbp, POOL_PREV, D_W), F32))
        p_states.append(sp)
        s_states.append(ss)
    keep = min(WIN_MAX, t_prompt)
    stack = lambda states, i: jnp.stack([s[i] for s in states])
    y_prompt = _rmsnorm(yp.reshape(-1, D_MODEL), ln_final, F32).reshape(yp.shape)
    y_sample = _rmsnorm(ys.reshape(-1, D_MODEL), ln_final, F32).reshape(ys.shape)
    return (y_prompt, y_sample,
            jnp.stack([s[0][:, -keep:] for s in p_states]), jnp.stack([s[1][:, -keep:] for s in p_states]),
            stack(p_states, 2), stack(p_states, 3), stack(p_states, 4),
            stack(s_states, 0), stack(s_states, 1), stack(s_states, 2), stack(s_states, 3), stack(s_states, 4),
            stack(s_states, 5))
```

```python
import functools

import jax
import jax.numpy as jnp
from jax import lax
from jax.experimental import pallas as pl
from jax.experimental.pallas import tpu as pltpu

F32 = jnp.float32
BF16 = jnp.bfloat16

D_MODEL = 4096
DEPTH = 2
PAST_LEN = 16384
A_W = B_W = C_W = D_W = D_MODEL // 4
RMS_EPS = 1e-6
HEAD_DIM_A = 128
N_HEADS_A = A_W // HEAD_DIM_A
DILATED_BRANCHES = ((128, 1), (512, 4), (2048, 16))
WIN_MAX = 2048
ROPE_THETA = 10000.0
HEAD_DIM_B = 64
N_HEADS_B = B_W // HEAD_DIM_B
LORA_W, LORA_A, LORA_G = 64, 64, 160
LORA_ALL = LORA_W + LORA_A + LORA_G
GN_EPS = 64e-5
B_FEAT = 3 * B_W + LORA_ALL
CHUNK = 128
N_GROUPS_C = 8
POOL_WINDOWS = (2, 4, 8, 16)
POOL_PREV = max(POOL_WINDOWS) - 1
POOL_GROUP = D_W // len(POOL_WINDOWS)
OFF_B = 3 * A_W
OFF_C = OFF_B + B_FEAT
OFF_D = OFF_C + 2 * C_W

LANES = 128
SUBLANES = 8
VMEM_BYTES_V7X = 64 * 2**20
VMEM_INTERNAL_RESERVE = 12 * 2**20

MXU_COLS_V7X = 256
MM_TN = 2 * MXU_COLS_V7X

ZM_QA, ZM_KA, ZM_VA = 0, A_W, 2 * A_W
ZM_RKV = 3 * A_W
ZM_W = ZM_RKV + 3 * B_W
ZR_CU = 0
ZR_CV = ZR_CU + C_W
ZR_D = ZR_CV + C_W
ZR_LORA = ZR_D + D_W
LORA_PAD = 3 * LANES
ZR_W = -(-(ZR_LORA + LORA_PAD) // MM_TN) * MM_TN

NEG = -1e30


def _vmem_limit(block_bytes, scratch_bytes=0):
    need = 2 * block_bytes + scratch_bytes + VMEM_INTERNAL_RESERVE
    return int(min(max(need, 16 * 2**20), VMEM_BYTES_V7X - 4 * 2**20))


def _params(sem, block_bytes, scratch_bytes=0):
    return pltpu.CompilerParams(dimension_semantics=sem,
                                vmem_limit_bytes=_vmem_limit(block_bytes, scratch_bytes))


def _nbytes(shape, dtype):
    n = 1
    for s in shape:
        n *= s
    return n * jnp.dtype(dtype).itemsize


def _rmsnorm_body(x_ref, g_ref, o_ref):
    x = x_ref[...]
    ms = jnp.mean(x * x, axis=-1, keepdims=True)
    o_ref[...] = (x * lax.rsqrt(ms + RMS_EPS) * g_ref[...]).astype(o_ref.dtype)


def _rmsnorm(x, g, out_dtype):
    m, d = x.shape
    tm = min(m, 256)
    blk = _nbytes((tm, d), F32) + _nbytes((tm, d), out_dtype)
    return pl.pallas_call(
        _rmsnorm_body, grid=(m // tm,),
        in_specs=[pl.BlockSpec((tm, d), lambda i: (i, 0)), pl.BlockSpec((1, d), lambda i: (0, 0))],
        out_specs=pl.BlockSpec((tm, d), lambda i: (i, 0)),
        out_shape=jax.ShapeDtypeStruct((m, d), out_dtype),
        compiler_params=_params(("parallel",), blk), name="rmsnorm")(x, g.reshape(1, d))


def _ffn_up_body(x_ref, wg_ref, wu_ref, o_ref):
    x = x_ref[...]
    g = jnp.dot(x, wg_ref[...], preferred_element_type=F32)
    u = jnp.dot(x, wu_ref[...], preferred_element_type=F32)
    o_ref[...] = (g * jax.nn.sigmoid(g) * u).astype(o_ref.dtype)


def _ffn_up_cast_body(x_ref, wg_ref, wu_ref, o_ref, wgo_ref, wuo_ref):
    wg = wg_ref[...].astype(BF16)
    wu = wu_ref[...].astype(BF16)
    wgo_ref[...] = wg
    wuo_ref[...] = wu
    x = x_ref[...]
    g = jnp.dot(x, wg, preferred_element_type=F32)
    u = jnp.dot(x, wu, preferred_element_type=F32)
    o_ref[...] = (g * jax.nn.sigmoid(g) * u).astype(o_ref.dtype)


def _ffn_up(x, wg, wu):
    m, k = x.shape
    n = wg.shape[1]
    tm = min(m, 1024)
    tn = MXU_COLS_V7X
    blk = _nbytes((tm, k), BF16) + 2 * _nbytes((k, tn), BF16) + _nbytes((tm, tn), BF16)
    return pl.pallas_call(
        _ffn_up_body, grid=(m // tm, n // tn),
        in_specs=[pl.BlockSpec((tm, k), lambda i, j: (i, 0)),
                  pl.BlockSpec((k, tn), lambda i, j: (0, j)),
                  pl.BlockSpec((k, tn), lambda i, j: (0, j))],
        out_specs=pl.BlockSpec((tm, tn), lambda i, j: (i, j)),
        out_shape=jax.ShapeDtypeStruct((m, n), BF16),
        compiler_params=_params(("parallel", "arbitrary"), blk), name="ffn_up")(x, wg, wu)


def _ffn_up_cast(x, wg, wu, l):
    m, k = x.shape
    n = wg.shape[2]
    tn = MXU_COLS_V7X
    blk = _nbytes((m, k), BF16) + 2 * _nbytes((k, tn), F32) + 2 * _nbytes((k, tn), BF16) + _nbytes((m, tn), BF16)
    wspec = pl.BlockSpec((None, k, tn), lambda j: (l, 0, j))
    ospec = pl.BlockSpec((k, tn), lambda j: (0, j))
    return pl.pallas_call(
        _ffn_up_cast_body, grid=(n // tn,),
        in_specs=[pl.BlockSpec((m, k), lambda j: (0, 0)), wspec, wspec],
        out_specs=[pl.BlockSpec((m, tn), lambda j: (0, j)), ospec, ospec],
        out_shape=[jax.ShapeDtypeStruct((m, n), BF16), jax.ShapeDtypeStruct((k, n), BF16),
                   jax.ShapeDtypeStruct((k, n), BF16)],
        compiler_params=_params(("parallel",), blk), name="ffn_up_cast")(x, wg, wu)


def _mm_body(a_ref, b_ref, o_ref):
    o_ref[...] = jnp.dot(a_ref[...], b_ref[...], preferred_element_type=F32)


def _mm_res_body(a_ref, b_ref, r_ref, o_ref, *, scale):
    acc = jnp.dot(a_ref[...], b_ref[...], preferred_element_type=F32)
    o_ref[...] = r_ref[...] + scale * acc


_TRANS_B = (((1,), (1,)), ((), ()))


def _mm_t_body(a_ref, bt_ref, o_ref):
    o_ref[...] = lax.dot_general(a_ref[...], bt_ref[...], _TRANS_B, preferred_element_type=F32)


def _mm_cast_body(a_ref, b_ref, *rest, scale, has_res, trans_b):
    r_ref = rest[0] if has_res else None
    o_ref, bo_ref = rest[-2:]
    b = (b_ref[0] if len(b_ref.shape) == 3 else b_ref[...]).astype(BF16)
    bo_ref[...] = b
    if trans_b:
        acc = lax.dot_general(a_ref[...], b, _TRANS_B, preferred_element_type=F32)
    else:
        acc = jnp.dot(a_ref[...], b, preferred_element_type=F32)
    o_ref[...] = r_ref[...] + scale * acc if has_res else acc


def _matmul(a, b, l=None, res=None, scale=1.0, tm=1024, trans_b=False):
    m, k = a.shape
    n = b.shape[-2] if trans_b else b.shape[-1]
    tm = min(m, tm)
    tn = MM_TN
    assert m % tm == 0 and n % tn == 0
    blk = _nbytes((tm, k), BF16) + _nbytes((k, tn), BF16) + _nbytes((tm, tn), F32)
    bshape, bidx = ((tn, k), lambda i, j: (j, 0)) if trans_b else ((k, tn), lambda i, j: (0, j))
    bspec = (pl.BlockSpec(bshape, bidx) if b.ndim == 2
             else pl.BlockSpec((None,) + bshape, lambda i, j: (l,) + bidx(i, j)))
    in_specs = [pl.BlockSpec((tm, k), lambda i, j: (i, 0)), bspec]
    args = [a, b]
    if trans_b:
        assert res is None
        body = _mm_t_body
    elif res is None:
        body = _mm_body
    else:
        body = functools.partial(_mm_res_body, scale=scale)
        in_specs.append(pl.BlockSpec((tm, tn), lambda i, j: (i, j)))
        args.append(res)
        blk += _nbytes((tm, tn), F32)
    return pl.pallas_call(
        body, grid=(m // tm, n // tn), in_specs=in_specs,
        out_specs=pl.BlockSpec((tm, tn), lambda i, j: (i, j)),
        out_shape=jax.ShapeDtypeStruct((m, n), F32),
        compiler_params=_params(("parallel", "arbitrary"), blk), name="matmul")(*args)


CAST_TILE_BYTES = 6 * 2**20


def _matmul_cast(a, b, l, n, res=None, scale=1.0, trans_b=False, src_rows=None):
    m, k = a.shape
    tn = MM_TN
    while _nbytes((k, tn), F32) > CAST_TILE_BYTES:
        tn //= 2
    assert n % tn == 0 and tn % LANES == 0
    blk = _nbytes((m, k), BF16) + _nbytes((k, tn), F32) + _nbytes((k, tn), BF16) + 2 * _nbytes((m, tn), F32)
    if trans_b:
        if src_rows is None:
            wspec = pl.BlockSpec((None, tn, k), lambda j: (l, j, 0))
        else:
            wspec = pl.BlockSpec((pl.Element(1), pl.Element(tn), pl.Element(k)), lambda j: (l, src_rows(j, tn), 0))
        cspec, cshape = pl.BlockSpec((tn, k), lambda j: (j, 0)), (n, k)
    else:
        wspec = pl.BlockSpec((None, k, tn), lambda j: (l, 0, j))
        cspec, cshape = pl.BlockSpec((k, tn), lambda j: (0, j)), (k, n)
    in_specs = [pl.BlockSpec((m, k), lambda j: (0, 0)), wspec]
    args = [a, b]
    if res is not None:
        in_specs.append(pl.BlockSpec((m, tn), lambda j: (0, j)))
        args.append(res)
    return pl.pallas_call(
        functools.partial(_mm_cast_body, scale=scale, has_res=res is not None, trans_b=trans_b), grid=(n // tn,),
        in_specs=in_specs, out_specs=[pl.BlockSpec((m, tn), lambda j: (0, j)), cspec],
        out_shape=[jax.ShapeDtypeStruct((m, n), F32), jax.ShapeDtypeStruct(cshape, BF16)],
        compiler_params=_params(("parallel",), blk), name="matmul_cast")(*args)


def _rope(x, cos, sin_signed):
    return x * cos + pltpu.roll(x, HEAD_DIM_A // 2, 1) * sin_signed


def _branch_multiplicity(delta):
    c = jnp.zeros(delta.shape, F32)
    for window, dilation in DILATED_BRANCHES:
        assert dilation & (dilation - 1) == 0
        hit = jnp.where(delta <= window, 1.0, 0.0)
        if dilation > 1:
            hit = jnp.where((delta & (dilation - 1)) == 0, hit, 0.0)
        c = c + hit
    return jnp.where(delta >= 0, c, 0.0)


def _attn_prompt_body(q_ref, k_ref, v_ref, cos_ref, sin_ref, o_ref, kout_ref, qs, ks, vs, ctab, *, t_len, tq):
    nq = t_len // tq

    @pl.when((pl.program_id(0) == 0) & (pl.program_id(1) == 0))
    def _():
        rel = (lax.broadcasted_iota(jnp.int32, (tq, tq), 0) - lax.broadcasted_iota(jnp.int32, (tq, tq), 1))
        for d in range(nq):
            ctab[:, d * tq:(d + 1) * tq] = _branch_multiplicity(rel + (nq - 1 - d) * tq)

    cos = cos_ref[...]
    sin = sin_ref[...]
    k = _rope(k_ref[0], cos, sin)
    kout_ref[0] = k
    ks[...] = k.astype(BF16)
    qs[...] = (_rope(q_ref[0], cos, sin) * (HEAD_DIM_A ** -0.5)).astype(BF16)
    vs[...] = v_ref[0].astype(BF16)
    for i in range(nq):
        kw = (i + 1) * tq
        s = lax.dot_general(qs[i * tq:(i + 1) * tq, :], ks[0:kw, :], _TRANS_B, preferred_element_type=F32)
        c = ctab[:, (nq - 1 - i) * tq:]
        sm = jnp.where(c > 0.0, s, NEG)
        p = jnp.exp(sm - jnp.max(sm, axis=-1, keepdims=True)) * c
        l = jnp.sum(p, axis=-1, keepdims=True)
        acc = jnp.dot(p.astype(BF16), vs[0:kw, :], preferred_element_type=F32)
        o_ref[0, i * tq:(i + 1) * tq, :] = acc / l


def _attn_sample_body(q_ref, k_ref, v_ref, kp_ref, vp_ref, cos_ref, sin_ref, o_ref, kout_ref, *, t_len, n_prev):
    cos = cos_ref[...]
    sin = sin_ref[...]
    d1 = (n_prev + lax.broadcasted_iota(jnp.int32, (t_len, n_prev), 0)
          - lax.broadcasted_iota(jnp.int32, (t_len, n_prev), 1))
    c1 = _branch_multiplicity(d1)
    d2 = (lax.broadcasted_iota(jnp.int32, (t_len, LANES), 0) - lax.broadcasted_iota(jnp.int32, (t_len, LANES), 1))
    c2 = _branch_multiplicity(d2)
    pad = jnp.zeros((LANES - t_len, HEAD_DIM_A), F32)
    for h in range(N_HEADS_A):
        sl = slice(h * HEAD_DIM_A, (h + 1) * HEAD_DIM_A)
        k = _rope(k_ref[0, :, sl], cos, sin)
        kout_ref[0, :, sl] = k
        q = (_rope(q_ref[0, :, sl], cos, sin) * (HEAD_DIM_A ** -0.5)).astype(BF16)
        kn = jnp.concatenate([k, pad], axis=0).astype(BF16)
        vn = jnp.concatenate([v_ref[0, :, sl], pad], axis=0).astype(BF16)
        s1 = lax.dot_general(q, kp_ref[:, h, :].astype(BF16), _TRANS_B, preferred_element_type=F32)
        s2 = lax.dot_general(q, kn, _TRANS_B, preferred_element_type=F32)
        sm1 = jnp.where(c1 > 0.0, s1, NEG)
        sm2 = jnp.where(c2 > 0.0, s2, NEG)
        m = jnp.maximum(jnp.max(sm1, axis=-1, keepdims=True), jnp.max(sm2, axis=-1, keepdims=True))
        p1 = jnp.exp(sm1 - m) * c1
        p2 = jnp.exp(sm2 - m) * c2
        l = jnp.sum(p1, axis=-1, keepdims=True) + jnp.sum(p2, axis=-1, keepdims=True)
        acc = (jnp.dot(p1.astype(BF16), vp_ref[:, h, :].astype(BF16), preferred_element_type=F32)
               + jnp.dot(p2.astype(BF16), vn, preferred_element_type=F32))
        o_ref[0, :, sl] = acc / l


def _rope_tables(pos0, t_len):
    half = HEAD_DIM_A // 2
    inv = ROPE_THETA ** (-jnp.arange(half, dtype=F32) / half)
    ang = (pos0 + jnp.arange(t_len)).astype(F32)[:, None] * inv[None, :]
    cos = jnp.cos(ang)
    sin = jnp.sin(ang)
    return jnp.concatenate([cos, cos], axis=-1), jnp.concatenate([-sin, sin], axis=-1)


def _attention(z3, pos0, kv_prefix, l):
    b, t, _ = z3.shape
    cos, sin = _rope_tables(pos0, t)
    hd = HEAD_DIM_A
    out_shape = [jax.ShapeDtypeStruct((b, t, A_W), F32), jax.ShapeDtypeStruct((b, t, A_W), F32)]
    if kv_prefix is None:
        col = lambda base: (lambda bi, hi: (bi, 0, base // hd + hi))
        tab = pl.BlockSpec((t, hd), lambda bi, hi: (0, 0))
        zspecs = [pl.BlockSpec((1, t, hd), col(ZM_QA)), pl.BlockSpec((1, t, hd), col(ZM_KA)),
                  pl.BlockSpec((1, t, hd), col(ZM_VA))]
        out_specs = [pl.BlockSpec((1, t, hd), col(0)), pl.BlockSpec((1, t, hd), col(0))]
        blk = 7 * _nbytes((t, hd), F32)
        tq = min(t, 256)
        body = functools.partial(_attn_prompt_body, t_len=t, tq=tq)
        scratch = 3 * _nbytes((t, hd), BF16) + 5 * _nbytes((tq, t), F32)
        return pl.pallas_call(
            body, grid=(b, N_HEADS_A), in_specs=zspecs + [tab, tab], out_specs=out_specs, out_shape=out_shape,
            scratch_shapes=[pltpu.VMEM((t, hd), BF16)] * 3 + [pltpu.VMEM((tq, t), F32)],
            compiler_params=_params(("arbitrary", "arbitrary"), blk, scratch),
            name="attn_prompt")(z3, z3, z3, cos, sin)
    k_prev, v_prev = kv_prefix
    n_prev = k_prev.shape[2]
    assert t <= LANES and n_prev % LANES == 0 and k_prev.shape[3:] == (N_HEADS_A, hd)
    col = lambda base: (lambda bi: (bi, 0, base // A_W))
    tab = pl.BlockSpec((t, hd), lambda bi: (0, 0))
    zspecs = [pl.BlockSpec((1, t, A_W), col(ZM_QA)), pl.BlockSpec((1, t, A_W), col(ZM_KA)),
              pl.BlockSpec((1, t, A_W), col(ZM_VA))]
    out_specs = [pl.BlockSpec((1, t, A_W), col(0)), pl.BlockSpec((1, t, A_W), col(0))]
    pspec = pl.BlockSpec((None, None, n_prev, N_HEADS_A, hd), lambda bi: (l, bi, 0, 0, 0))
    blk = 5 * _nbytes((t, A_W), F32) + 2 * _nbytes((n_prev, A_W), F32)
    body = functools.partial(_attn_sample_body, t_len=t, n_prev=n_prev)
    return pl.pallas_call(
        body, grid=(b,), in_specs=zspecs + [pspec, pspec, tab, tab], out_specs=out_specs,
        out_shape=out_shape, compiler_params=_params(("parallel",), blk),
        name="attn_sample")(z3, z3, z3, k_prev, v_prev, cos, sin)


def _group_sum(x, ones_blockdiag):
    outs = []
    for j in range(x.shape[-1] // LANES):
        outs.append(jnp.dot(x[:, j * LANES:(j + 1) * LANES], ones_blockdiag, preferred_element_type=F32,
                            precision=lax.Precision.HIGHEST))
    return jnp.concatenate(outs, axis=-1)


def _softplus(y):
    return jnp.maximum(y, 0.0) + jnp.log1p(jnp.exp(-jnp.abs(y)))


def _rwkv_prep_body(x_ref, lo_ref, sx_ref, slo_ref, mux_ref, mulo_ref, w0_ref, a0_ref, kkp_ref, kap_ref, rk_ref,
                    wup_ref, aup_ref, gup_ref, ones_ref,
                    r_o, w_o, k_o, v_o, kk_o, b_o, g_o, bonus_o, last_x, last_lo, *, tt, channel_major):
    @pl.when(pl.program_id(1) == 0)
    def _():
        last_x[0:1, :] = sx_ref[0]
        last_lo[0:1, :] = slo_ref[0]

    x = x_ref[0]
    lo = lo_ref[0]
    first = lax.broadcasted_iota(jnp.int32, (tt, 1), 0) == 0
    px = jnp.where(first, last_x[0:1, :], pltpu.roll(x, 1, 0))
    plo = jnp.where(first, last_lo[0:1, :], pltpu.roll(lo, 1, 0))
    last_x[0:1, :] = x[tt - 1:tt, :]
    last_lo[0:1, :] = lo[tt - 1:tt, :]
    fx = x + mux_ref[...] * (px - x)
    flo = lo + mulo_ref[...] * (plo - lo)
    r = fx[:, :B_W]
    k = fx[:, B_W:2 * B_W]
    v = fx[:, 2 * B_W:]
    zwa = flo[:, :LANES]
    zg = flo[:, LANES:]
    ones_bd = ones_ref[...]
    wl = w0_ref[...] + jnp.dot(jnp.tanh(zwa).astype(BF16), wup_ref[...], preferred_element_type=F32)
    w_log = -_softplus(-wl) - 0.5
    decay = jnp.exp(-jnp.exp(w_log))
    a = jax.nn.sigmoid(a0_ref[...] + jnp.dot(zwa.astype(BF16), aup_ref[...], preferred_element_type=F32))
    g = jnp.dot(jax.nn.sigmoid(zg).astype(BF16), gup_ref[...], preferred_element_type=F32)
    kk = k * kkp_ref[...]
    kk = kk / jnp.maximum(jnp.sqrt(_group_sum(kk * kk, ones_bd)), 1e-12)
    kmod = k * (1.0 + (a - 1.0) * kap_ref[...])
    lay = (lambda y: y.T) if channel_major else (lambda y: y)
    r_o[0] = lay(r)
    w_o[0] = lay(decay)
    k_o[0] = lay(kmod)
    v_o[0] = lay(v)
    kk_o[0] = lay(kk)
    b_o[0] = lay(kk * a)
    g_o[0] = g
    bonus_o[0] = _group_sum(r * kmod * rk_ref[...], ones_bd) * v


def _rwkv_prep(zm3, zr3, shift_prev, p, channel_major):
    b, t, _ = zm3.shape
    tt = min(t, 256)
    sx = shift_prev[:, None, :3 * B_W]
    slo = jnp.pad(shift_prev[:, None, 3 * B_W:], ((0, 0), (0, 0), (0, LORA_PAD - LORA_ALL)))
    row = lambda w: pl.BlockSpec((1, w), lambda bi, ti: (0, 0))
    full = lambda a: pl.BlockSpec(a.shape, lambda bi, ti: (0,) * a.ndim)
    in_specs = [pl.BlockSpec((1, tt, 3 * B_W), lambda bi, ti: (bi, ti, ZM_RKV // (3 * B_W))),
                pl.BlockSpec((1, tt, LORA_PAD), lambda bi, ti: (bi, ti, ZR_LORA // LORA_PAD)),
                pl.BlockSpec((1, 1, 3 * B_W), lambda bi, ti: (bi, 0, 0)),
                pl.BlockSpec((1, 1, LORA_PAD), lambda bi, ti: (bi, 0, 0)),
                row(3 * B_W), row(LORA_PAD), row(B_W), row(B_W), row(B_W), row(B_W), row(B_W),
                full(p['w_up']), full(p['a_up']), full(p['g_up']), full(p['ones_bd'])]
    ospec = pl.BlockSpec((1, tt, B_W), lambda bi, ti: (bi, ti, 0))
    oshape = jax.ShapeDtypeStruct((b, t, B_W), F32)
    if channel_major:
        rspec, rshape = pl.BlockSpec((1, B_W, tt), lambda bi, ti: (bi, 0, ti)), jax.ShapeDtypeStruct((b, B_W, t), F32)
    else:
        rspec, rshape = ospec, oshape
    blk = _nbytes((tt, 3 * B_W + LORA_PAD), F32) + 8 * _nbytes((tt, B_W), F32) + 2 * _nbytes((512, B_W), F32)
    return pl.pallas_call(
        functools.partial(_rwkv_prep_body, tt=tt, channel_major=channel_major), grid=(b, t // tt),
        in_specs=in_specs, out_specs=[rspec] * 6 + [ospec] * 2, out_shape=[rshape] * 6 + [oshape] * 2,
        scratch_shapes=[pltpu.VMEM((SUBLANES, 3 * B_W), F32), pltpu.VMEM((SUBLANES, LORA_PAD), F32)],
        compiler_params=_params(("parallel", "arbitrary"), blk), name="rwkv_prep")(
            zm3, zr3, sx, slo, p['mu_x'], p['mu_lo'], p['w0'], p['a0'], p['k_k'], p['k_a'], p['r_k'],
            p['w_up'], p['a_up'], p['g_up'], p['ones_bd'])


N_PARTIAL = 4


def _rwkv_steps(r_ref, w_ref, k_ref, kk_ref, b_ref, v_ref, o_ref, state, *, tc, vp, time_major):
    def step(t, row):
        parts = [jnp.zeros((vp, LANES), F32)] * N_PARTIAL
        for k in range(HEAD_DIM_B):
            parts[k % N_PARTIAL] = parts[k % N_PARTIAL] + state[k] * row(kk_ref, k)
        sa = -((parts[0] + parts[1]) + (parts[2] + parts[3]))
        vt = v_ref[t]
        parts = [jnp.zeros((vp, LANES), F32)] * N_PARTIAL
        for k in range(HEAD_DIM_B):
            s = state[k] * row(w_ref, k) + sa * row(b_ref, k) + vt * row(k_ref, k)
            state[k] = s
            parts[k % N_PARTIAL] = parts[k % N_PARTIAL] + s * row(r_ref, k)
        o_ref[t] = (parts[0] + parts[1]) + (parts[2] + parts[3])

    if time_major:
        def one(t, carry):
            step(t, lambda ref, k: ref[t, pl.ds(k, 1), :])
            return carry

        lax.fori_loop(0, tc, one, 0)
    else:
        def eight(tb, carry):
            for s in range(SUBLANES):
                step(tb * SUBLANES + s, lambda ref, k, s=s: ref[k, tb, pl.ds(s, 1), :])
            return carry

        lax.fori_loop(0, tc // SUBLANES, eight, 0)


def _rwkv_scan_body(r_ref, w_ref, k_ref, kk_ref, b_ref, v_ref, s0_ref, o_ref, sout_ref, state, *, tc, vp):
    @pl.when(pl.program_id(0) == 0)
    def _():
        state[...] = s0_ref[...]

    _rwkv_steps(r_ref, w_ref, k_ref, kk_ref, b_ref, v_ref, o_ref, state, tc=tc, vp=vp, time_major=True)

    @pl.when(pl.program_id(0) == pl.num_programs(0) - 1)
    def _():
        sout_ref[...] = state[...]


def _rwkv_scan_rows_body(r_ref, w_ref, k_ref, kk_ref, b_ref, v_ref, s0_ref, o_ref, sout_ref,
                         state, yt, rk, wk, kkey, kkk, bk, vk, ok, *, tc, nb, dup):
    vp = HEAD_DIM_B // dup
    nh = N_HEADS_B

    @pl.when(pl.program_id(0) == 0)
    def _():
        state[...] = s0_ref[...]

    def to_lanes(x_ref, store, n_rows, row_of):
        for j in range(n_rows):
            pieces = [x_ref[b, pl.ds(row_of(j, vh), nh, stride=HEAD_DIM_B), :] for b in range(nb) for vh in range(dup)]
            store(j, jnp.concatenate(pieces, axis=0).T)

    def key_store(dst):
        def store(j, x):
            dst[j] = x.reshape(tc // SUBLANES, SUBLANES, LANES)
        return store

    def val_store(j, x):
        vk[:, j, :] = x

    for x_ref, dst in ((r_ref, rk), (w_ref, wk), (k_ref, kkey), (kk_ref, kkk), (b_ref, bk)):
        to_lanes(x_ref, key_store(dst), HEAD_DIM_B, lambda j, vh: j)
    to_lanes(v_ref, val_store, vp, lambda j, vh: vh * vp + j)

    _rwkv_steps(rk, wk, kkey, kkk, bk, vk, ok, state, tc=tc, vp=vp, time_major=False)

    for j in range(vp):
        m = ok[:, j, :].T
        for b in range(nb):
            for vh in range(dup):
                lane0 = (b * dup + vh) * nh
                yt[b, pl.ds(vh * vp + j, nh, stride=HEAD_DIM_B), :] = m[lane0:lane0 + nh, :]
    for b in range(nb):
        o_ref[b] = yt[b].T

    @pl.when(pl.program_id(0) == pl.num_programs(0) - 1)
    def _():
        sout_ref[...] = state[...]


def _rwkv_scan_rows(r, w, k, kk, bv, v, s0, dup):
    nb, _, t = r.shape
    vp = HEAD_DIM_B // dup
    tc = LANES
    assert t % tc == 0 and nb * dup * N_HEADS_B == LANES
    cspec = pl.BlockSpec((nb, B_W, tc), lambda i: (0, 0, i))
    xspec = pl.BlockSpec((nb, tc, B_W), lambda i: (0, i, 0))
    sspec = pl.BlockSpec((HEAD_DIM_B, vp, LANES), lambda i: (0, 0, 0))
    key_tile = pltpu.VMEM((HEAD_DIM_B, tc // SUBLANES, SUBLANES, LANES), F32)
    val_tile = pltpu.VMEM((tc, vp, LANES), F32)
    blk = 7 * _nbytes((nb, tc, B_W), F32) + 2 * _nbytes((HEAD_DIM_B, vp, LANES), F32)
    scr = (_nbytes((HEAD_DIM_B, vp, LANES), F32) + _nbytes((nb, B_W, tc), F32)
           + 5 * _nbytes((tc, HEAD_DIM_B, LANES), F32) + 2 * _nbytes((tc, vp, LANES), F32))
    return pl.pallas_call(
        functools.partial(_rwkv_scan_rows_body, tc=tc, nb=nb, dup=dup), grid=(t // tc,),
        in_specs=[cspec] * 6 + [sspec], out_specs=[xspec, sspec],
        out_shape=[jax.ShapeDtypeStruct((nb, t, B_W), F32), jax.ShapeDtypeStruct((HEAD_DIM_B, vp, LANES), F32)],
        scratch_shapes=[pltpu.VMEM((HEAD_DIM_B, vp, LANES), F32), pltpu.VMEM((nb, B_W, tc), F32)]
        + [key_tile] * 5 + [val_tile] * 2,
        compiler_params=_params(("arbitrary",), blk, scr), name="rwkv_scan_rows")(r, w, k, kk, bv, v, s0)


def _rwkv_scan(r, w, k, kk, bv, v, s0):
    t, vp, _ = v.shape
    tc = min(t, 64)
    kspec = pl.BlockSpec((tc, HEAD_DIM_B, LANES), lambda i: (i, 0, 0))
    vspec = pl.BlockSpec((tc, vp, LANES), lambda i: (i, 0, 0))
    sspec = pl.BlockSpec((HEAD_DIM_B, vp, LANES), lambda i: (0, 0, 0))
    blk = 5 * _nbytes((tc, HEAD_DIM_B, LANES), F32) + 2 * _nbytes((tc, vp, LANES), F32) \
        + 2 * _nbytes((HEAD_DIM_B, vp, LANES), F32)
    return pl.pallas_call(
        functools.partial(_rwkv_scan_body, tc=tc, vp=vp), grid=(t // tc,),
        in_specs=[kspec] * 5 + [vspec, sspec], out_specs=[vspec, sspec],
        out_shape=[jax.ShapeDtypeStruct((t, vp, LANES), F32), jax.ShapeDtypeStruct((HEAD_DIM_B, vp, LANES), F32)],
        scratch_shapes=[pltpu.VMEM((HEAD_DIM_B, vp, LANES), F32)],
        compiler_params=_params(("arbitrary",), blk, _nbytes((HEAD_DIM_B, vp, LANES), F32)),
        name="rwkv_scan")(r, w, k, kk, bv, v, s0)


def _rwkv_mix(zm3, zr3, shift_prev, wkv0, p):
    b, t, _ = zm3.shape
    nh, hd = N_HEADS_B, HEAD_DIM_B
    in_vmem_relayout = t % LANES == 0
    r, w, kmod, v, kk, bv, g, bonus = _rwkv_prep(zm3, zr3, shift_prev, p, channel_major=in_vmem_relayout)
    dup = LANES // (b * nh)
    assert dup * b * nh == LANES and hd % dup == 0
    vp = hd // dup

    def key_layout(x):
        y = x.reshape(b, t, nh, hd).transpose(1, 3, 0, 2)[:, :, :, None]
        return jnp.broadcast_to(y, (t, hd, b, dup, nh)).reshape(t, hd, LANES)

    def val_layout(x):
        return x.reshape(b, t, nh, dup, vp).transpose(1, 4, 0, 3, 2).reshape(t, vp, LANES)

    s0 = wkv0.reshape(b, nh, dup, vp, hd).transpose(4, 3, 0, 2, 1).reshape(hd, vp, LANES)
    if in_vmem_relayout:
        o, s = _rwkv_scan_rows(r, w, kmod, kk, bv, v, s0, dup)
    else:
        o, s = _rwkv_scan(key_layout(r), key_layout(w), key_layout(kmod), key_layout(kk), key_layout(bv),
                          val_layout(v), s0)
        o = o.reshape(t, vp, b, dup, nh).transpose(2, 0, 4, 3, 1).reshape(b, t, B_W)
    s = s.reshape(hd, vp, b, dup, nh).transpose(2, 4, 3, 1, 0).reshape(b, nh, hd, hd)
    return o, bonus, g, s


def _gmlp_body(u_ref, v_ref, ws_ref, b_ref, o_ref, *, tc):
    keep = (lax.broadcasted_iota(jnp.int32, (CHUNK, CHUNK), 1) <= lax.broadcasted_iota(jnp.int32, (CHUNK, CHUNK), 0))
    for g in range(N_GROUPS_C):
        sl = slice(g * CHUNK, (g + 1) * CHUNK)
        w = jnp.where(keep, ws_ref[g], 0.0).astype(BF16)
        v = v_ref[0, :, sl]
        if tc < CHUNK:
            v = jnp.concatenate([v, jnp.zeros((CHUNK - tc, CHUNK), F32)], axis=0)
        s = jnp.dot(w, v.astype(BF16), preferred_element_type=F32) + b_ref[g]
        o_ref[0, :, sl] = u_ref[0, :, sl] * s[:tc]


def _gmlp(z3, w_s, b_s):
    b, t, _ = z3.shape
    tc = min(t, CHUNK)
    assert t % tc == 0
    blk = 3 * _nbytes((tc, C_W), F32) + 2 * _nbytes((N_GROUPS_C, CHUNK, CHUNK), F32)
    return pl.pallas_call(
        functools.partial(_gmlp_body, tc=tc), grid=(b, t // tc),
        in_specs=[pl.BlockSpec((1, tc, C_W), lambda bi, ci: (bi, ci, ZR_CU // C_W)),
                  pl.BlockSpec((1, tc, C_W), lambda bi, ci: (bi, ci, ZR_CV // C_W)),
                  pl.BlockSpec((N_GROUPS_C, CHUNK, CHUNK), lambda bi, ci: (0, 0, 0)),
                  pl.BlockSpec((N_GROUPS_C, CHUNK, 1), lambda bi, ci: (0, 0, 0))],
        out_specs=pl.BlockSpec((1, tc, C_W), lambda bi, ci: (bi, ci, 0)),
        out_shape=jax.ShapeDtypeStruct((b, t, C_W), F32),
        compiler_params=_params(("parallel", "parallel"), blk), name="gmlp")(z3, z3, w_s, b_s[:, :, None])


POOL_HALO = 16


def _pool_body(p_ref, pre_ref, w_ref, sc_ref, o_ref, ext, *, t_len, pos0, tc):
    ext[0:POOL_HALO, :] = pre_ref[0]
    ext[POOL_HALO:POOL_HALO + t_len, :] = p_ref[0]
    for c0 in range(0, t_len, tc):
        pos = pos0 + c0 + lax.broadcasted_iota(jnp.int32, (tc, 1), 0)
        for g, win in enumerate(POOL_WINDOWS):
            sl = slice(g * POOL_GROUP, (g + 1) * POOL_GROUP)
            base = POOL_HALO + c0
            acc = ext[base:base + tc, sl]
            for i in range(1, win):
                acc = acc + ext[base - i:base - i + tc, sl]
            cnt = jnp.minimum(win, pos + 1).astype(F32)
            pooled = acc / cnt - p_ref[0, c0:c0 + tc, sl]
            y = jnp.dot(pooled.astype(BF16), w_ref[g], preferred_element_type=F32)
            o_ref[0, c0:c0 + tc, sl] = y * sc_ref[:, sl]


def _pool(z3, prefix, pos0, w_pool, scale):
    b, t, _ = z3.shape
    assert POOL_PREV < POOL_HALO
    pre = jnp.pad(prefix, ((0, 0), (POOL_HALO - POOL_PREV, 0), (0, 0)))
    tc = min(t, 256)
    blk = 2 * _nbytes((t, D_W), F32) + _nbytes((POOL_HALO, D_W), F32) + _nbytes(w_pool.shape, BF16)
    scr = _nbytes((t + POOL_HALO, D_W), F32)
    return pl.pallas_call(
        functools.partial(_pool_body, t_len=t, pos0=pos0, tc=tc), grid=(b,),
        in_specs=[pl.BlockSpec((1, t, D_W), lambda bi: (bi, 0, ZR_D // D_W)),
                  pl.BlockSpec((1, POOL_HALO, D_W), lambda bi: (bi, 0, 0)),
                  pl.BlockSpec(w_pool.shape, lambda bi: (0, 0, 0)),
                  pl.BlockSpec((1, D_W), lambda bi: (0, 0))],
        out_specs=pl.BlockSpec((1, t, D_W), lambda bi: (bi, 0, 0)),
        out_shape=jax.ShapeDtypeStruct((b, t, D_W), F32),
        scratch_shapes=[pltpu.VMEM((t + POOL_HALO, D_W), F32)],
        compiler_params=_params(("parallel",), blk, scr), name="pool")(z3, pre, w_pool, scale.reshape(1, D_W))


def _mix_body(oa_ref, ob_ref, bonus_ref, g_ref, oc_ref, od_ref, ga_ref, lnw_ref, lnb_ref, gc_ref, gd_ref, ones_ref,
              o_ref):
    def rms(x, gain):
        ms = jnp.mean(x * x, axis=-1, keepdims=True)
        return x * lax.rsqrt(ms + RMS_EPS) * gain

    ones_bd = ones_ref[...]
    o = ob_ref[...]
    mean = _group_sum(o, ones_bd) * (1.0 / HEAD_DIM_B)
    d = o - mean
    var = _group_sum(d * d, ones_bd) * (1.0 / HEAD_DIM_B)
    ob = d * lax.rsqrt(var + GN_EPS) * lnw_ref[...] + lnb_ref[...]
    ob = (ob + bonus_ref[...]) * g_ref[...]
    o_ref[:, 0:A_W] = rms(oa_ref[...], ga_ref[...]).astype(BF16)
    o_ref[:, A_W:A_W + B_W] = ob.astype(BF16)
    o_ref[:, A_W + B_W:A_W + B_W + C_W] = rms(oc_ref[...], gc_ref[...]).astype(BF16)
    o_ref[:, A_W + B_W + C_W:] = rms(od_ref[...], gd_ref[...]).astype(BF16)


def _mix(oa, ob, bonus, g, oc, od, ga, lnw, lnb, gc, gd, ones_bd):
    m = oa.shape[0]
    tm = min(m, 256)
    act = pl.BlockSpec((tm, A_W), lambda i: (i, 0))
    row = pl.BlockSpec((1, A_W), lambda i: (0, 0))
    blk = 6 * _nbytes((tm, A_W), F32) + _nbytes((tm, D_MODEL), BF16) + _nbytes((512, A_W), F32)
    r1 = lambda a: a.reshape(1, -1)
    return pl.pallas_call(
        _mix_body, grid=(m // tm,),
        in_specs=[act] * 6 + [row] * 5 + [pl.BlockSpec((LANES, LANES), lambda i: (0, 0))],
        out_specs=pl.BlockSpec((tm, D_MODEL), lambda i: (i, 0)),
        out_shape=jax.ShapeDtypeStruct((m, D_MODEL), BF16),
        compiler_params=_params(("parallel",), blk), name="mix")(
            oa, ob, bonus, g, oc, od, r1(ga), r1(lnw), r1(lnb), r1(gc), r1(gd), ones_bd)


def _zr_src_rows(j, tn):
    assert ZM_W == OFF_B + 3 * B_W and (ZR_LORA - ZR_CU) % tn == 0 and OFF_C % 32 == 0 and ZM_W % 32 == 0
    n_cd = (ZR_LORA - ZR_CU) // tn
    return pl.multiple_of(jnp.where(j < n_cd, OFF_C + j * tn, ZM_W + (j - n_cd) * tn), 32)


def _prep_layer_weights(l, w):
    mu = w['mu_b'][l]
    pad_rows = lambda a, before, total: jnp.pad(a, ((before, total - before - a.shape[0]), (0, 0))).astype(BF16)
    ones_bd = jnp.kron(jnp.eye(LANES // HEAD_DIM_B, dtype=F32), jnp.ones((HEAD_DIM_B, HEAD_DIM_B), F32))
    rw = dict(
        mu_x=mu[None, :3 * B_W], mu_lo=jnp.pad(mu[None, 3 * B_W:], ((0, 0), (0, LORA_PAD - LORA_ALL))),
        w0=w['w0'][l][None], a0=w['a0'][l][None], k_k=w['k_k'][l][None], k_a=w['k_a'][l][None],
        r_k=w['r_k'][l].reshape(1, B_W),
        w_up=pad_rows(w['w_up'][l], 0, LANES), a_up=pad_rows(w['a_up'][l], LORA_W, LANES),
        g_up=pad_rows(w['g_up'][l], 0, LORA_PAD - LANES), ones_bd=ones_bd)
    return dict(w_pool=w['w_pool'][l].astype(BF16), rwkv=rw)


def _layer(x, l, w, wl, wb, pos0, kv_prefix, shift_prev, wkv0, pool_prefix):
    cast = wb is None
    wb = {} if cast else wb
    b, t, _ = x.shape
    m = b * t

    def up(xn, gate, upw):
        if not cast:
            return _ffn_up(xn, wb[gate], wb[upw])
        hid, wb[gate], wb[upw] = _ffn_up_cast(xn, w[gate], w[upw], l)
        return hid

    def mm(a, name, n, tm, res=None, scale=1.0, trans_b=False, copy=None, src_rows=None):
        copy = copy or name
        if not cast:
            return _matmul(a, wb[copy], res=res, scale=scale, tm=tm, trans_b=trans_b)
        out, wb[copy] = _matmul_cast(a, w[name], l, n, res=res, scale=scale, trans_b=trans_b, src_rows=src_rows)
        return out

    x2 = x.reshape(m, D_MODEL)
    hid = up(_rmsnorm(x2, w['ln_ffn1'][l], BF16), 'w1_gate', 'w1_up')
    h = mm(hid, 'w1_down', D_MODEL, 512, res=x2, scale=0.5)
    hn = _rmsnorm(h, w['ln_mix'][l], BF16)
    zm3 = mm(hn, 'w_in_t', ZM_W, 1024, trans_b=True).reshape(b, t, ZM_W)
    zr3 = mm(hn, 'w_in_t', ZR_W, 1024, trans_b=True, copy='w_rest_t', src_rows=_zr_src_rows).reshape(b, t, ZR_W)
    out_a, ka = _attention(zm3, pos0, kv_prefix, l)
    o_b, bonus, gate, wkv_new = _rwkv_mix(zm3, zr3, shift_prev, wkv0, wl['rwkv'])
    out_c = _gmlp(zr3, w['w_s'][l], w['b_s'][l])
    out_d = _pool(zr3, pool_prefix, pos0, wl['w_pool'], w['pool_scale'][l])
    r2 = lambda a: a.reshape(m, -1)
    mix = _mix(r2(out_a), r2(o_b), r2(bonus), r2(gate), r2(out_c), r2(out_d), w['g_out_a'][l], w['ln_x_w'][l],
               w['ln_x_b'][l], w['g_out_c'][l], w['g_out_d'][l], wl['rwkv']['ones_bd'])
    h = mm(mix, 'w_out', D_MODEL, 1024, res=h, scale=1.0)
    hid = up(_rmsnorm(h, w['ln_ffn2'][l], BF16), 'w2_gate', 'w2_up')
    y = mm(hid, 'w2_down', D_MODEL, 512, res=h, scale=0.5)
    hs = (b, t, N_HEADS_A, HEAD_DIM_A)
    va = zm3[..., ZM_VA:ZM_VA + A_W]
    shift_new = jnp.concatenate([zm3[:, -1, ZM_RKV:], zr3[:, -1, ZR_LORA:ZR_LORA + LORA_ALL]], axis=-1)
    p = zr3[..., ZR_D:ZR_D + D_W]
    pool_new = jnp.concatenate([pool_prefix, p], axis=1)[:, -POOL_PREV:]
    vc = zr3[..., ZR_CV:ZR_CV + C_W]
    return y.reshape(b, t, D_MODEL), (ka.reshape(hs), va.reshape(hs), wkv_new, shift_new, pool_new, vc), wb


def kernel(x_prompt, x_sample, cache_k_swa, cache_v_swa, state_rwkv_wkv, state_rwkv_shift, state_pool, ln_ffn1, w1_gate, w1_up, w1_down, ln_mix, w_in, g_out_a, mu_b, w0, w_up, a0, a_up, g_up, k_k, k_a, r_k, ln_x_w, ln_x_b, w_s, b_s, g_out_c, w_pool, pool_scale, g_out_d, w_out, ln_ffn2, w2_gate, w2_up, w2_down, ln_final):
    w = dict(ln_ffn1=ln_ffn1, w1_gate=w1_gate, w1_up=w1_up, w1_down=w1_down, ln_mix=ln_mix, w_in=w_in,
             g_out_a=g_out_a, mu_b=mu_b, w0=w0, w_up=w_up, a0=a0, a_up=a_up, g_up=g_up, k_k=k_k, k_a=k_a,
             r_k=r_k, ln_x_w=ln_x_w, ln_x_b=ln_x_b, w_s=w_s, b_s=b_s, g_out_c=g_out_c, w_pool=w_pool,
             pool_scale=pool_scale, g_out_d=g_out_d, w_out=w_out, ln_ffn2=ln_ffn2, w2_gate=w2_gate,
             w2_up=w2_up, w2_down=w2_down)
    nbp, t_prompt, _ = x_prompt.shape
    yp, ys = x_prompt, x_sample
    p_states, s_states = [], []
    w['w_in_t'] = jnp.swapaxes(w_in, 1, 2)
    for l in range(DEPTH):
        wl = _prep_layer_weights(l, w)
        ys, ss, wb = _layer(ys, l, w, wl, None, PAST_LEN, (cache_k_swa, cache_v_swa),
                            state_rwkv_shift[l], state_rwkv_wkv[l], state_pool[l])
        yp, sp, _ = _layer(yp, l, w, wl, wb, 0, None,
                           jnp.zeros((nbp, B_FEAT), F32),
                           jnp.zeros((nbp, N_HEADS_B, HEAD_DIM_B, HEAD_DIM_B), F32),
                           jnp.zeros((nbp, POOL_PREV, D_W), F32))
        p_states.append(sp)
        s_states.append(ss)
    keep = min(WIN_MAX, t_prompt)
    stack = lambda states, i: jnp.stack([s[i] for s in states])
    y_prompt = _rmsnorm(yp.reshape(-1, D_MODEL), ln_final, F32).reshape(yp.shape)
    y_sample = _rmsnorm(ys.reshape(-1, D_MODEL), ln_final, F32).reshape(ys.shape)
    return (y_prompt, y_sample,
            jnp.stack([s[0][:, -keep:] for s in p_states]), jnp.stack([s[1][:, -keep:] for s in p_states]),
            stack(p_states, 2), stack(p_states, 3), stack(p_states, 4),
            stack(s_states, 0), stack(s_states, 1), stack(s_states, 2), stack(s_states, 3), stack(s_states, 4),
            stack(s_states, 5))
```

```python
import functools

import jax
import jax.numpy as jnp
from jax import lax
from jax.experimental import pallas as pl
from jax.experimental.pallas import tpu as pltpu

F32 = jnp.float32
BF16 = jnp.bfloat16

D_MODEL = 4096
DEPTH = 2
PAST_LEN = 16384
A_W = B_W = C_W = D_W = D_MODEL // 4
RMS_EPS = 1e-6
HEAD_DIM_A = 128
N_HEADS_A = A_W // HEAD_DIM_A
DILATED_BRANCHES = ((128, 1), (512, 4), (2048, 16))
WIN_MAX = 2048
ROPE_THETA = 10000.0
HEAD_DIM_B = 64
N_HEADS_B = B_W // HEAD_DIM_B
LORA_W, LORA_A, LORA_G = 64, 64, 160
LORA_ALL = LORA_W + LORA_A + LORA_G
GN_EPS = 64e-5
B_FEAT = 3 * B_W + LORA_ALL
CHUNK = 128
N_GROUPS_C = 8
POOL_WINDOWS = (2, 4, 8, 16)
POOL_PREV = max(POOL_WINDOWS) - 1
POOL_GROUP = D_W // len(POOL_WINDOWS)
OFF_B = 3 * A_W
OFF_C = OFF_B + B_FEAT
OFF_D = OFF_C + 2 * C_W

LANES = 128
SUBLANES = 8
VMEM_BYTES_V7X = 64 * 2**20
VMEM_INTERNAL_RESERVE = 12 * 2**20

MXU_COLS_V7X = 256
MM_TN = 2 * MXU_COLS_V7X

ZM_QA, ZM_KA, ZM_VA = 0, A_W, 2 * A_W
ZM_RKV = 3 * A_W
ZM_W = ZM_RKV + 3 * B_W
ZR_CU = 0
ZR_CV = ZR_CU + C_W
ZR_D = ZR_CV + C_W
ZR_LORA = ZR_D + D_W
LORA_PAD = 3 * LANES
ZR_W = -(-(ZR_LORA + LORA_PAD) // MM_TN) * MM_TN

NEG = -1e30


def _vmem_limit(block_bytes, scratch_bytes=0):
    need = 2 * block_bytes + scratch_bytes + VMEM_INTERNAL_RESERVE
    return int(min(max(need, 16 * 2**20), VMEM_BYTES_V7X - 4 * 2**20))


def _params(sem, block_bytes, scratch_bytes=0):
    return pltpu.CompilerParams(dimension_semantics=sem,
                                vmem_limit_bytes=_vmem_limit(block_bytes, scratch_bytes))


def _nbytes(shape, dtype):
    n = 1
    for s in shape:
        n *= s
    return n * jnp.dtype(dtype).itemsize


def _rmsnorm_body(x_ref, g_ref, o_ref):
    x = x_ref[...]
    ms = jnp.mean(x * x, axis=-1, keepdims=True)
    o_ref[...] = (x * lax.rsqrt(ms + RMS_EPS) * g_ref[...]).astype(o_ref.dtype)


def _rmsnorm(x, g, out_dtype):
    m, d = x.shape
    tm = min(m, 256)
    blk = _nbytes((tm, d), F32) + _nbytes((tm, d), out_dtype)
    return pl.pallas_call(
        _rmsnorm_body, grid=(m // tm,),
        in_specs=[pl.BlockSpec((tm, d), lambda i: (i, 0)), pl.BlockSpec((1, d), lambda i: (0, 0))],
        out_specs=pl.BlockSpec((tm, d), lambda i: (i, 0)),
        out_shape=jax.ShapeDtypeStruct((m, d), out_dtype),
        compiler_params=_params(("parallel",), blk), name="rmsnorm")(x, g.reshape(1, d))


def _ffn_up_body(x_ref, wg_ref, wu_ref, o_ref):
    x = x_ref[...]
    g = jnp.dot(x, wg_ref[...].astype(BF16), preferred_element_type=F32)
    u = jnp.dot(x, wu_ref[...].astype(BF16), preferred_element_type=F32)
    o_ref[...] = (g * jax.nn.sigmoid(g) * u).astype(o_ref.dtype)


def _ffn_up(x, wg, wu, l):
    m, k = x.shape
    n = wg.shape[2]
    tm = min(m, 1024)
    tn = MXU_COLS_V7X
    blk = _nbytes((tm, k), BF16) + 2 * _nbytes((k, tn), F32) + _nbytes((k, tn), BF16) + _nbytes((tm, tn), BF16)
    wspec = pl.BlockSpec((None, k, tn), lambda i, j: (l, 0, j))
    return pl.pallas_call(
        _ffn_up_body, grid=(m // tm, n // tn),
        in_specs=[pl.BlockSpec((tm, k), lambda i, j: (i, 0)), wspec, wspec],
        out_specs=pl.BlockSpec((tm, tn), lambda i, j: (i, j)),
        out_shape=jax.ShapeDtypeStruct((m, n), BF16),
        compiler_params=_params(("parallel", "arbitrary"), blk), name="ffn_up")(x, wg, wu)


def _mm_body(a_ref, b_ref, o_ref):
    o_ref[...] = jnp.dot(a_ref[...], b_ref[...], preferred_element_type=F32)


def _mm_res_body(a_ref, b_ref, r_ref, o_ref, *, scale):
    acc = jnp.dot(a_ref[...], b_ref[...], preferred_element_type=F32)
    o_ref[...] = r_ref[...] + scale * acc


_TRANS_B = (((1,), (1,)), ((), ()))


def _mm_t_body(a_ref, bt_ref, o_ref):
    o_ref[...] = lax.dot_general(a_ref[...], bt_ref[...], _TRANS_B, preferred_element_type=F32)


def _mm_cast_body(a_ref, b_ref, *rest, scale, has_res, trans_b):
    r_ref = rest[0] if has_res else None
    o_ref, bo_ref = rest[-2:]
    b = (b_ref[0] if len(b_ref.shape) == 3 else b_ref[...]).astype(BF16)
    bo_ref[...] = b
    if trans_b:
        acc = lax.dot_general(a_ref[...], b, _TRANS_B, preferred_element_type=F32)
    else:
        acc = jnp.dot(a_ref[...], b, preferred_element_type=F32)
    o_ref[...] = r_ref[...] + scale * acc if has_res else acc


def _matmul(a, b, l=None, res=None, scale=1.0, tm=1024, trans_b=False):
    m, k = a.shape
    n = b.shape[-2] if trans_b else b.shape[-1]
    tm = min(m, tm)
    tn = MM_TN
    assert m % tm == 0 and n % tn == 0
    blk = _nbytes((tm, k), BF16) + _nbytes((k, tn), BF16) + _nbytes((tm, tn), F32)
    bshape, bidx = ((tn, k), lambda i, j: (j, 0)) if trans_b else ((k, tn), lambda i, j: (0, j))
    bspec = (pl.BlockSpec(bshape, bidx) if b.ndim == 2
             else pl.BlockSpec((None,) + bshape, lambda i, j: (l,) + bidx(i, j)))
    in_specs = [pl.BlockSpec((tm, k), lambda i, j: (i, 0)), bspec]
    args = [a, b]
    if trans_b:
        assert res is None
        body = _mm_t_body
    elif res is None:
        body = _mm_body
    else:
        body = functools.partial(_mm_res_body, scale=scale)
        in_specs.append(pl.BlockSpec((tm, tn), lambda i, j: (i, j)))
        args.append(res)
        blk += _nbytes((tm, tn), F32)
    return pl.pallas_call(
        body, grid=(m // tm, n // tn), in_specs=in_specs,
        out_specs=pl.BlockSpec((tm, tn), lambda i, j: (i, j)),
        out_shape=jax.ShapeDtypeStruct((m, n), F32),
        compiler_params=_params(("parallel", "arbitrary"), blk), name="matmul")(*args)


CAST_TILE_BYTES = 6 * 2**20


def _matmul_cast(a, b, l, n, res=None, scale=1.0, trans_b=False, src_rows=None):
    m, k = a.shape
    tn = MM_TN
    while _nbytes((k, tn), F32) > CAST_TILE_BYTES:
        tn //= 2
    assert n % tn == 0 and tn % LANES == 0
    blk = _nbytes((m, k), BF16) + _nbytes((k, tn), F32) + _nbytes((k, tn), BF16) + 2 * _nbytes((m, tn), F32)
    if trans_b:
        if src_rows is None:
            wspec = pl.BlockSpec((None, tn, k), lambda j: (l, j, 0))
        else:
            wspec = pl.BlockSpec((pl.Element(1), pl.Element(tn), pl.Element(k)), lambda j: (l, src_rows(j, tn), 0))
        cspec, cshape = pl.BlockSpec((tn, k), lambda j: (j, 0)), (n, k)
    else:
        wspec = pl.BlockSpec((None, k, tn), lambda j: (l, 0, j))
        cspec, cshape = pl.BlockSpec((k, tn), lambda j: (0, j)), (k, n)
    in_specs = [pl.BlockSpec((m, k), lambda j: (0, 0)), wspec]
    args = [a, b]
    if res is not None:
        in_specs.append(pl.BlockSpec((m, tn), lambda j: (0, j)))
        args.append(res)
    return pl.pallas_call(
        functools.partial(_mm_cast_body, scale=scale, has_res=res is not None, trans_b=trans_b), grid=(n // tn,),
        in_specs=in_specs, out_specs=[pl.BlockSpec((m, tn), lambda j: (0, j)), cspec],
        out_shape=[jax.ShapeDtypeStruct((m, n), F32), jax.ShapeDtypeStruct(cshape, BF16)],
        compiler_params=_params(("parallel",), blk), name="matmul_cast")(*args)


def _rope(x, cos, sin_signed):
    return x * cos + pltpu.roll(x, HEAD_DIM_A // 2, 1) * sin_signed


def _branch_multiplicity(delta):
    c = jnp.zeros(delta.shape, F32)
    for window, dilation in DILATED_BRANCHES:
        assert dilation & (dilation - 1) == 0
        hit = jnp.where(delta <= window, 1.0, 0.0)
        if dilation > 1:
            hit = jnp.where((delta & (dilation - 1)) == 0, hit, 0.0)
        c = c + hit
    return jnp.where(delta >= 0, c, 0.0)


def _attn_prompt_body(q_ref, k_ref, v_ref, cos_ref, sin_ref, o_ref, kout_ref, qs, ks, vs, ctab, *, t_len, tq):
    nq = t_len // tq

    @pl.when((pl.program_id(0) == 0) & (pl.program_id(1) == 0))
    def _():
        rel = (lax.broadcasted_iota(jnp.int32, (tq, tq), 0) - lax.broadcasted_iota(jnp.int32, (tq, tq), 1))
        for d in range(nq):
            ctab[:, d * tq:(d + 1) * tq] = _branch_multiplicity(rel + (nq - 1 - d) * tq)

    cos = cos_ref[...]
    sin = sin_ref[...]
    k = _rope(k_ref[0], cos, sin)
    kout_ref[0] = k
    ks[...] = k.astype(BF16)
    qs[...] = (_rope(q_ref[0], cos, sin) * (HEAD_DIM_A ** -0.5)).astype(BF16)
    vs[...] = v_ref[0].astype(BF16)
    for i in range(nq):
        kw = (i + 1) * tq
        s = lax.dot_general(qs[i * tq:(i + 1) * tq, :], ks[0:kw, :], _TRANS_B, preferred_element_type=F32)
        c = ctab[:, (nq - 1 - i) * tq:]
        sm = jnp.where(c > 0.0, s, NEG)
        p = jnp.exp(sm - jnp.max(sm, axis=-1, keepdims=True)) * c
        l = jnp.sum(p, axis=-1, keepdims=True)
        acc = jnp.dot(p.astype(BF16), vs[0:kw, :], preferred_element_type=F32)
        o_ref[0, i * tq:(i + 1) * tq, :] = acc / l


def _attn_sample_body(q_ref, k_ref, v_ref, kp_ref, vp_ref, cos_ref, sin_ref, o_ref, kout_ref, *, t_len, n_prev):
    cos = cos_ref[...]
    sin = sin_ref[...]
    d1 = (n_prev + lax.broadcasted_iota(jnp.int32, (t_len, n_prev), 0)
          - lax.broadcasted_iota(jnp.int32, (t_len, n_prev), 1))
    c1 = _branch_multiplicity(d1)
    d2 = (lax.broadcasted_iota(jnp.int32, (t_len, LANES), 0) - lax.broadcasted_iota(jnp.int32, (t_len, LANES), 1))
    c2 = _branch_multiplicity(d2)
    pad = jnp.zeros((LANES - t_len, HEAD_DIM_A), F32)
    for h in range(N_HEADS_A):
        sl = slice(h * HEAD_DIM_A, (h + 1) * HEAD_DIM_A)
        k = _rope(k_ref[0, :, sl], cos, sin)
        kout_ref[0, :, sl] = k
        q = (_rope(q_ref[0, :, sl], cos, sin) * (HEAD_DIM_A ** -0.5)).astype(BF16)
        kn = jnp.concatenate([k, pad], axis=0).astype(BF16)
        vn = jnp.concatenate([v_ref[0, :, sl], pad], axis=0).astype(BF16)
        s1 = lax.dot_general(q, kp_ref[:, h, :].astype(BF16), _TRANS_B, preferred_element_type=F32)
        s2 = lax.dot_general(q, kn, _TRANS_B, preferred_element_type=F32)
        sm1 = jnp.where(c1 > 0.0, s1, NEG)
        sm2 = jnp.where(c2 > 0.0, s2, NEG)
        m = jnp.maximum(jnp.max(sm1, axis=-1, keepdims=True), jnp.max(sm2, axis=-1, keepdims=True))
        p1 = jnp.exp(sm1 - m) * c1
        p2 = jnp.exp(sm2 - m) * c2
        l = jnp.sum(p1, axis=-1, keepdims=True) + jnp.sum(p2, axis=-1, keepdims=True)
        acc = (jnp.dot(p1.astype(BF16), vp_ref[:, h, :].astype(BF16), preferred_element_type=F32)
               + jnp.dot(p2.astype(BF16), vn, preferred_element_type=F32))
        o_ref[0, :, sl] = acc / l


def _rope_tables(pos0, t_len):
    half = HEAD_DIM_A // 2
    inv = ROPE_THETA ** (-jnp.arange(half, dtype=F32) / half)
    ang = (pos0 + jnp.arange(t_len)).astype(F32)[:, None] * inv[None, :]
    cos = jnp.cos(ang)
    sin = jnp.sin(ang)
    return jnp.concatenate([cos, cos], axis=-1), jnp.concatenate([-sin, sin], axis=-1)


def _attention(z3, pos0, kv_prefix, l):
    b, t, _ = z3.shape
    cos, sin = _rope_tables(pos0, t)
    hd = HEAD_DIM_A
    out_shape = [jax.ShapeDtypeStruct((b, t, A_W), F32), jax.ShapeDtypeStruct((b, t, A_W), F32)]
    if kv_prefix is None:
        col = lambda base: (lambda bi, hi: (bi, 0, base // hd + hi))
        tab = pl.BlockSpec((t, hd), lambda bi, hi: (0, 0))
        zspecs = [pl.BlockSpec((1, t, hd), col(ZM_QA)), pl.BlockSpec((1, t, hd), col(ZM_KA)),
                  pl.BlockSpec((1, t, hd), col(ZM_VA))]
        out_specs = [pl.BlockSpec((1, t, hd), col(0)), pl.BlockSpec((1, t, hd), col(0))]
        blk = 7 * _nbytes((t, hd), F32)
        tq = min(t, 256)
        body = functools.partial(_attn_prompt_body, t_len=t, tq=tq)
        scratch = 3 * _nbytes((t, hd), BF16) + 5 * _nbytes((tq, t), F32)
        return pl.pallas_call(
            body, grid=(b, N_HEADS_A), in_specs=zspecs + [tab, tab], out_specs=out_specs, out_shape=out_shape,
            scratch_shapes=[pltpu.VMEM((t, hd), BF16)] * 3 + [pltpu.VMEM((tq, t), F32)],
            compiler_params=_params(("arbitrary", "arbitrary"), blk, scratch),
            name="attn_prompt")(z3, z3, z3, cos, sin)
    k_prev, v_prev = kv_prefix
    n_prev = k_prev.shape[2]
    assert t <= LANES and n_prev % LANES == 0 and k_prev.shape[3:] == (N_HEADS_A, hd)
    col = lambda base: (lambda bi: (bi, 0, base // A_W))
    tab = pl.BlockSpec((t, hd), lambda bi: (0, 0))
    zspecs = [pl.BlockSpec((1, t, A_W), col(ZM_QA)), pl.BlockSpec((1, t, A_W), col(ZM_KA)),
              pl.BlockSpec((1, t, A_W), col(ZM_VA))]
    out_specs = [pl.BlockSpec((1, t, A_W), col(0)), pl.BlockSpec((1, t, A_W), col(0))]
    pspec = pl.BlockSpec((None, None, n_prev, N_HEADS_A, hd), lambda bi: (l, bi, 0, 0, 0))
    blk = 5 * _nbytes((t, A_W), F32) + 2 * _nbytes((n_prev, A_W), F32)
    body = functools.partial(_attn_sample_body, t_len=t, n_prev=n_prev)
    return pl.pallas_call(
        body, grid=(b,), in_specs=zspecs + [pspec, pspec, tab, tab], out_specs=out_specs,
        out_shape=out_shape, compiler_params=_params(("parallel",), blk),
        name="attn_sample")(z3, z3, z3, k_prev, v_prev, cos, sin)


def _group_sum(x, ones_blockdiag):
    outs = []
    for j in range(x.shape[-1] // LANES):
        outs.append(jnp.dot(x[:, j * LANES:(j + 1) * LANES], ones_blockdiag, preferred_element_type=F32,
                            precision=lax.Precision.HIGHEST))
    return jnp.concatenate(outs, axis=-1)


def _softplus(y):
    return jnp.maximum(y, 0.0) + jnp.log1p(jnp.exp(-jnp.abs(y)))


def _rwkv_prep_body(x_ref, lo_ref, sx_ref, slo_ref, mux_ref, mulo_ref, w0_ref, a0_ref, kkp_ref, kap_ref, rk_ref,
                    wup_ref, aup_ref, gup_ref, ones_ref,
                    r_o, w_o, k_o, v_o, kk_o, b_o, g_o, bonus_o, last_x, last_lo, *, tt, channel_major):
    @pl.when(pl.program_id(1) == 0)
    def _():
        last_x[0:1, :] = sx_ref[0]
        last_lo[0:1, :] = slo_ref[0]

    x = x_ref[0]
    lo = lo_ref[0]
    first = lax.broadcasted_iota(jnp.int32, (tt, 1), 0) == 0
    px = jnp.where(first, last_x[0:1, :], pltpu.roll(x, 1, 0))
    plo = jnp.where(first, last_lo[0:1, :], pltpu.roll(lo, 1, 0))
    last_x[0:1, :] = x[tt - 1:tt, :]
    last_lo[0:1, :] = lo[tt - 1:tt, :]
    fx = x + mux_ref[...] * (px - x)
    flo = lo + mulo_ref[...] * (plo - lo)
    r = fx[:, :B_W]
    k = fx[:, B_W:2 * B_W]
    v = fx[:, 2 * B_W:]
    zwa = flo[:, :LANES]
    zg = flo[:, LANES:]
    ones_bd = ones_ref[...]
    wl = w0_ref[...] + jnp.dot(jnp.tanh(zwa).astype(BF16), wup_ref[...], preferred_element_type=F32)
    w_log = -_softplus(-wl) - 0.5
    decay = jnp.exp(-jnp.exp(w_log))
    a = jax.nn.sigmoid(a0_ref[...] + jnp.dot(zwa.astype(BF16), aup_ref[...], preferred_element_type=F32))
    g = jnp.dot(jax.nn.sigmoid(zg).astype(BF16), gup_ref[...], preferred_element_type=F32)
    kk = k * kkp_ref[...]
    kk = kk / jnp.maximum(jnp.sqrt(_group_sum(kk * kk, ones_bd)), 1e-12)
    kmod = k * (1.0 + (a - 1.0) * kap_ref[...])
    lay = (lambda y: y.T) if channel_major else (lambda y: y)
    r_o[0] = lay(r)
    w_o[0] = lay(decay)
    k_o[0] = lay(kmod)
    v_o[0] = lay(v)
    kk_o[0] = lay(kk)
    b_o[0] = lay(kk * a)
    g_o[0] = g
    bonus_o[0] = _group_sum(r * kmod * rk_ref[...], ones_bd) * v


def _rwkv_prep(zm3, zr3, shift_prev, p, channel_major):
    b, t, _ = zm3.shape
    tt = min(t, 256)
    sx = shift_prev[:, None, :3 * B_W]
    slo = jnp.pad(shift_prev[:, None, 3 * B_W:], ((0, 0), (0, 0), (0, LORA_PAD - LORA_ALL)))
    row = lambda w: pl.BlockSpec((1, w), lambda bi, ti: (0, 0))
    full = lambda a: pl.BlockSpec(a.shape, lambda bi, ti: (0,) * a.ndim)
    in_specs = [pl.BlockSpec((1, tt, 3 * B_W), lambda bi, ti: (bi, ti, ZM_RKV // (3 * B_W))),
                pl.BlockSpec((1, tt, LORA_PAD), lambda bi, ti: (bi, ti, ZR_LORA // LORA_PAD)),
                pl.BlockSpec((1, 1, 3 * B_W), lambda bi, ti: (bi, 0, 0)),
                pl.BlockSpec((1, 1, LORA_PAD), lambda bi, ti: (bi, 0, 0)),
                row(3 * B_W), row(LORA_PAD), row(B_W), row(B_W), row(B_W), row(B_W), row(B_W),
                full(p['w_up']), full(p['a_up']), full(p['g_up']), full(p['ones_bd'])]
    ospec = pl.BlockSpec((1, tt, B_W), lambda bi, ti: (bi, ti, 0))
    oshape = jax.ShapeDtypeStruct((b, t, B_W), F32)
    if channel_major:
        rspec, rshape = pl.BlockSpec((1, B_W, tt), lambda bi, ti: (bi, 0, ti)), jax.ShapeDtypeStruct((b, B_W, t), F32)
    else:
        rspec, rshape = ospec, oshape
    blk = _nbytes((tt, 3 * B_W + LORA_PAD), F32) + 8 * _nbytes((tt, B_W), F32) + 2 * _nbytes((512, B_W), F32)
    return pl.pallas_call(
        functools.partial(_rwkv_prep_body, tt=tt, channel_major=channel_major), grid=(b, t // tt),
        in_specs=in_specs, out_specs=[rspec] * 6 + [ospec] * 2, out_shape=[rshape] * 6 + [oshape] * 2,
        scratch_shapes=[pltpu.VMEM((SUBLANES, 3 * B_W), F32), pltpu.VMEM((SUBLANES, LORA_PAD), F32)],
        compiler_params=_params(("parallel", "arbitrary"), blk), name="rwkv_prep")(
            zm3, zr3, sx, slo, p['mu_x'], p['mu_lo'], p['w0'], p['a0'], p['k_k'], p['k_a'], p['r_k'],
            p['w_up'], p['a_up'], p['g_up'], p['ones_bd'])


N_PARTIAL = 4


def _rwkv_steps(r_ref, w_ref, k_ref, kk_ref, b_ref, v_ref, o_ref, state, *, tc, vp, time_major):
    def step(t, row):
        parts = [jnp.zeros((vp, LANES), F32)] * N_PARTIAL
        for k in range(HEAD_DIM_B):
            parts[k % N_PARTIAL] = parts[k % N_PARTIAL] + state[k] * row(kk_ref, k)
        sa = -((parts[0] + parts[1]) + (parts[2] + parts[3]))
        vt = v_ref[t]
        parts = [jnp.zeros((vp, LANES), F32)] * N_PARTIAL
        for k in range(HEAD_DIM_B):
            s = state[k] * row(w_ref, k) + sa * row(b_ref, k) + vt * row(k_ref, k)
            state[k] = s
            parts[k % N_PARTIAL] = parts[k % N_PARTIAL] + s * row(r_ref, k)
        o_ref[t] = (parts[0] + parts[1]) + (parts[2] + parts[3])

    if time_major:
        def one(t, carry):
            step(t, lambda ref, k: ref[t, pl.ds(k, 1), :])
            return carry

        lax.fori_loop(0, tc, one, 0)
    else:
        def eight(tb, carry):
            for s in range(SUBLANES):
                step(tb * SUBLANES + s, lambda ref, k, s=s: ref[k, tb, pl.ds(s, 1), :])
            return carry

        lax.fori_loop(0, tc // SUBLANES, eight, 0)


def _rwkv_scan_body(r_ref, w_ref, k_ref, kk_ref, b_ref, v_ref, s0_ref, o_ref, sout_ref, state, *, tc, vp):
    @pl.when(pl.program_id(0) == 0)
    def _():
        state[...] = s0_ref[...]

    _rwkv_steps(r_ref, w_ref, k_ref, kk_ref, b_ref, v_ref, o_ref, state, tc=tc, vp=vp, time_major=True)

    @pl.when(pl.program_id(0) == pl.num_programs(0) - 1)
    def _():
        sout_ref[...] = state[...]


def _rwkv_scan_rows_body(r_ref, w_ref, k_ref, kk_ref, b_ref, v_ref, s0_ref, o_ref, sout_ref,
                         state, yt, rk, wk, kkey, kkk, bk, vk, ok, *, tc, nb, dup):
    vp = HEAD_DIM_B // dup
    nh = N_HEADS_B

    @pl.when(pl.program_id(0) == 0)
    def _():
        state[...] = s0_ref[...]

    def to_lanes(x_ref, store, n_rows, row_of):
        for j in range(n_rows):
            pieces = [x_ref[b, pl.ds(row_of(j, vh), nh, stride=HEAD_DIM_B), :] for b in range(nb) for vh in range(dup)]
            store(j, jnp.concatenate(pieces, axis=0).T)

    def key_store(dst):
        def store(j, x):
            dst[j] = x.reshape(tc // SUBLANES, SUBLANES, LANES)
        return store

    def val_store(j, x):
        vk[:, j, :] = x

    for x_ref, dst in ((r_ref, rk), (w_ref, wk), (k_ref, kkey), (kk_ref, kkk), (b_ref, bk)):
        to_lanes(x_ref, key_store(dst), HEAD_DIM_B, lambda j, vh: j)
    to_lanes(v_ref, val_store, vp, lambda j, vh: vh * vp + j)

    _rwkv_steps(rk, wk, kkey, kkk, bk, vk, ok, state, tc=tc, vp=vp, time_major=False)

    for j in range(vp):
        m = ok[:, j, :].T
        for b in range(nb):
            for vh in range(dup):
                lane0 = (b * dup + vh) * nh
                yt[b, pl.ds(vh * vp + j, nh, stride=HEAD_DIM_B), :] = m[lane0:lane0 + nh, :]
    for b in range(nb):
        o_ref[b] = yt[b].T

    @pl.when(pl.program_id(0) == pl.num_programs(0) - 1)
    def _():
        sout_ref[...] = state[...]


def _rwkv_scan_rows(r, w, k, kk, bv, v, s0, dup):
    nb, _, t = r.shape
    vp = HEAD_DIM_B // dup
    tc = LANES
    assert t % tc == 0 and nb * dup * N_HEADS_B == LANES
    cspec = pl.BlockSpec((nb, B_W, tc), lambda i: (0, 0, i))
    xspec = pl.BlockSpec((nb, tc, B_W), lambda i: (0, i, 0))
    sspec = pl.BlockSpec((HEAD_DIM_B, vp, LANES), lambda i: (0, 0, 0))
    key_tile = pltpu.VMEM((HEAD_DIM_B, tc // SUBLANES, SUBLANES, LANES), F32)
    val_tile = pltpu.VMEM((tc, vp, LANES), F32)
    blk = 7 * _nbytes((nb, tc, B_W), F32) + 2 * _nbytes((HEAD_DIM_B, vp, LANES), F32)
    scr = (_nbytes((HEAD_DIM_B, vp, LANES), F32) + _nbytes((nb, B_W, tc), F32)
           + 5 * _nbytes((tc, HEAD_DIM_B, LANES), F32) + 2 * _nbytes((tc, vp, LANES), F32))
    return pl.pallas_call(
        functools.partial(_rwkv_scan_rows_body, tc=tc, nb=nb, dup=dup), grid=(t // tc,),
        in_specs=[cspec] * 6 + [sspec], out_specs=[xspec, sspec],
        out_shape=[jax.ShapeDtypeStruct((nb, t, B_W), F32), jax.ShapeDtypeStruct((HEAD_DIM_B, vp, LANES), F32)],
        scratch_shapes=[pltpu.VMEM((HEAD_DIM_B, vp, LANES), F32), pltpu.VMEM((nb, B_W, tc), F32)]
        + [key_tile] * 5 + [val_tile] * 2,
        compiler_params=_params(("arbitrary",), blk, scr), name="rwkv_scan_rows")(r, w, k, kk, bv, v, s0)


def _rwkv_scan(r, w, k, kk, bv, v, s0):
    t, vp, _ = v.shape
    tc = min(t, 64)
    kspec = pl.BlockSpec((tc, HEAD_DIM_B, LANES), lambda i: (i, 0, 0))
    vspec = pl.BlockSpec((tc, vp, LANES), lambda i: (i, 0, 0))
    sspec = pl.BlockSpec((HEAD_DIM_B, vp, LANES), lambda i: (0, 0, 0))
    blk = 5 * _nbytes((tc, HEAD_DIM_B, LANES), F32) + 2 * _nbytes((tc, vp, LANES), F32) \
        + 2 * _nbytes((HEAD_DIM_B, vp, LANES), F32)
    return pl.pallas_call(
        functools.partial(_rwkv_scan_body, tc=tc, vp=vp), grid=(t // tc,),
        in_specs=[kspec] * 5 + [vspec, sspec], out_specs=[vspec, sspec],
        out_shape=[jax.ShapeDtypeStruct((t, vp, LANES), F32), jax.ShapeDtypeStruct((HEAD_DIM_B, vp, LANES), F32)],
        scratch_shapes=[pltpu.VMEM((HEAD_DIM_B, vp, LANES), F32)],
        compiler_params=_params(("arbitrary",), blk, _nbytes((HEAD_DIM_B, vp, LANES), F32)),
        name="rwkv_scan")(r, w, k, kk, bv, v, s0)


def _rwkv_mix(zm3, zr3, shift_prev, wkv0, p):
    b, t, _ = zm3.shape
    nh, hd = N_HEADS_B, HEAD_DIM_B
    in_vmem_relayout = t % LANES == 0
    r, w, kmod, v, kk, bv, g, bonus = _rwkv_prep(zm3, zr3, shift_prev, p, channel_major=in_vmem_relayout)
    dup = LANES // (b * nh)
    assert dup * b * nh == LANES and hd % dup == 0
    vp = hd // dup

    def key_layout(x):
        y = x.reshape(b, t, nh, hd).transpose(1, 3, 0, 2)[:, :, :, None]
        return jnp.broadcast_to(y, (t, hd, b, dup, nh)).reshape(t, hd, LANES)

    def val_layout(x):
        return x.reshape(b, t, nh, dup, vp).transpose(1, 4, 0, 3, 2).reshape(t, vp, LANES)

    s0 = wkv0.reshape(b, nh, dup, vp, hd).transpose(4, 3, 0, 2, 1).reshape(hd, vp, LANES)
    if in_vmem_relayout:
        o, s = _rwkv_scan_rows(r, w, kmod, kk, bv, v, s0, dup)
    else:
        o, s = _rwkv_scan(key_layout(r), key_layout(w), key_layout(kmod), key_layout(kk), key_layout(bv),
                          val_layout(v), s0)
        o = o.reshape(t, vp, b, dup, nh).transpose(2, 0, 4, 3, 1).reshape(b, t, B_W)
    s = s.reshape(hd, vp, b, dup, nh).transpose(2, 4, 3, 1, 0).reshape(b, nh, hd, hd)
    return o, bonus, g, s


def _gmlp_body(u_ref, v_ref, ws_ref, b_ref, o_ref, *, tc):
    keep = (lax.broadcasted_iota(jnp.int32, (CHUNK, CHUNK), 1) <= lax.broadcasted_iota(jnp.int32, (CHUNK, CHUNK), 0))
    for g in range(N_GROUPS_C):
        sl = slice(g * CHUNK, (g + 1) * CHUNK)
        w = jnp.where(keep, ws_ref[g], 0.0).astype(BF16)
        v = v_ref[0, :, sl]
        if tc < CHUNK:
            v = jnp.concatenate([v, jnp.zeros((CHUNK - tc, CHUNK), F32)], axis=0)
        s = jnp.dot(w, v.astype(BF16), preferred_element_type=F32) + b_ref[g]
        o_ref[0, :, sl] = u_ref[0, :, sl] * s[:tc]


def _gmlp(z3, w_s, b_s):
    b, t, _ = z3.shape
    tc = min(t, CHUNK)
    assert t % tc == 0
    blk = 3 * _nbytes((tc, C_W), F32) + 2 * _nbytes((N_GROUPS_C, CHUNK, CHUNK), F32)
    return pl.pallas_call(
        functools.partial(_gmlp_body, tc=tc), grid=(b, t // tc),
        in_specs=[pl.BlockSpec((1, tc, C_W), lambda bi, ci: (bi, ci, ZR_CU // C_W)),
                  pl.BlockSpec((1, tc, C_W), lambda bi, ci: (bi, ci, ZR_CV // C_W)),
                  pl.BlockSpec((N_GROUPS_C, CHUNK, CHUNK), lambda bi, ci: (0, 0, 0)),
                  pl.BlockSpec((N_GROUPS_C, CHUNK, 1), lambda bi, ci: (0, 0, 0))],
        out_specs=pl.BlockSpec((1, tc, C_W), lambda bi, ci: (bi, ci, 0)),
        out_shape=jax.ShapeDtypeStruct((b, t, C_W), F32),
        compiler_params=_params(("parallel", "parallel"), blk), name="gmlp")(z3, z3, w_s, b_s[:, :, None])


POOL_HALO = 16


def _pool_body(p_ref, pre_ref, w_ref, sc_ref, o_ref, ext, *, t_len, pos0, tc):
    ext[0:POOL_HALO, :] = pre_ref[0]
    ext[POOL_HALO:POOL_HALO + t_len, :] = p_ref[0]
    for c0 in range(0, t_len, tc):
        pos = pos0 + c0 + lax.broadcasted_iota(jnp.int32, (tc, 1), 0)
        for g, win in enumerate(POOL_WINDOWS):
            sl = slice(g * POOL_GROUP, (g + 1) * POOL_GROUP)
            base = POOL_HALO + c0
            acc = ext[base:base + tc, sl]
            for i in range(1, win):
                acc = acc + ext[base - i:base - i + tc, sl]
            cnt = jnp.minimum(win, pos + 1).astype(F32)
            pooled = acc / cnt - p_ref[0, c0:c0 + tc, sl]
            y = jnp.dot(pooled.astype(BF16), w_ref[g], preferred_element_type=F32)
            o_ref[0, c0:c0 + tc, sl] = y * sc_ref[:, sl]


def _pool(z3, prefix, pos0, w_pool, scale):
    b, t, _ = z3.shape
    assert POOL_PREV < POOL_HALO
    pre = jnp.pad(prefix, ((0, 0), (POOL_HALO - POOL_PREV, 0), (0, 0)))
    tc = min(t, 256)
    blk = 2 * _nbytes((t, D_W), F32) + _nbytes((POOL_HALO, D_W), F32) + _nbytes(w_pool.shape, BF16)
    scr = _nbytes((t + POOL_HALO, D_W), F32)
    return pl.pallas_call(
        functools.partial(_pool_body, t_len=t, pos0=pos0, tc=tc), grid=(b,),
        in_specs=[pl.BlockSpec((1, t, D_W), lambda bi: (bi, 0, ZR_D // D_W)),
                  pl.BlockSpec((1, POOL_HALO, D_W), lambda bi: (bi, 0, 0)),
                  pl.BlockSpec(w_pool.shape, lambda bi: (0, 0, 0)),
                  pl.BlockSpec((1, D_W), lambda bi: (0, 0))],
        out_specs=pl.BlockSpec((1, t, D_W), lambda bi: (bi, 0, 0)),
        out_shape=jax.ShapeDtypeStruct((b, t, D_W), F32),
        scratch_shapes=[pltpu.VMEM((t + POOL_HALO, D_W), F32)],
        compiler_params=_params(("parallel",), blk, scr), name="pool")(z3, pre, w_pool, scale.reshape(1, D_W))


def _mix_body(oa_ref, ob_ref, bonus_ref, g_ref, oc_ref, od_ref, ga_ref, lnw_ref, lnb_ref, gc_ref, gd_ref, ones_ref,
              o_ref):
    def rms(x, gain):
        ms = jnp.mean(x * x, axis=-1, keepdims=True)
        return x * lax.rsqrt(ms + RMS_EPS) * gain

    ones_bd = ones_ref[...]
    o = ob_ref[...]
    mean = _group_sum(o, ones_bd) * (1.0 / HEAD_DIM_B)
    d = o - mean
    var = _group_sum(d * d, ones_bd) * (1.0 / HEAD_DIM_B)
    ob = d * lax.rsqrt(var + GN_EPS) * lnw_ref[...] + lnb_ref[...]
    ob = (ob + bonus_ref[...]) * g_ref[...]
    o_ref[:, 0:A_W] = rms(oa_ref[...], ga_ref[...]).astype(BF16)
    o_ref[:, A_W:A_W + B_W] = ob.astype(BF16)
    o_ref[:, A_W + B_W:A_W + B_W + C_W] = rms(oc_ref[...], gc_ref[...]).astype(BF16)
    o_ref[:, A_W + B_W + C_W:] = rms(od_ref[...], gd_ref[...]).astype(BF16)


def _mix(oa, ob, bonus, g, oc, od, ga, lnw, lnb, gc, gd, ones_bd):
    m = oa.shape[0]
    tm = min(m, 256)
    act = pl.BlockSpec((tm, A_W), lambda i: (i, 0))
    row = pl.BlockSpec((1, A_W), lambda i: (0, 0))
    blk = 6 * _nbytes((tm, A_W), F32) + _nbytes((tm, D_MODEL), BF16) + _nbytes((512, A_W), F32)
    r1 = lambda a: a.reshape(1, -1)
    return pl.pallas_call(
        _mix_body, grid=(m // tm,),
        in_specs=[act] * 6 + [row] * 5 + [pl.BlockSpec((LANES, LANES), lambda i: (0, 0))],
        out_specs=pl.BlockSpec((tm, D_MODEL), lambda i: (i, 0)),
        out_shape=jax.ShapeDtypeStruct((m, D_MODEL), BF16),
        compiler_params=_params(("parallel",), blk), name="mix")(
            oa, ob, bonus, g, oc, od, r1(ga), r1(lnw), r1(lnb), r1(gc), r1(gd), ones_bd)


def _zr_src_rows(j, tn):
    assert ZM_W == OFF_B + 3 * B_W and (ZR_LORA - ZR_CU) % tn == 0 and OFF_C % 32 == 0 and ZM_W % 32 == 0
    n_cd = (ZR_LORA - ZR_CU) // tn
    return pl.multiple_of(jnp.where(j < n_cd, OFF_C + j * tn, ZM_W + (j - n_cd) * tn), 32)


def _prep_layer_weights(l, w):
    mu = w['mu_b'][l]
    pad_rows = lambda a, before, total: jnp.pad(a, ((before, total - before - a.shape[0]), (0, 0))).astype(BF16)
    ones_bd = jnp.kron(jnp.eye(LANES // HEAD_DIM_B, dtype=F32), jnp.ones((HEAD_DIM_B, HEAD_DIM_B), F32))
    rw = dict(
        mu_x=mu[None, :3 * B_W], mu_lo=jnp.pad(mu[None, 3 * B_W:], ((0, 0), (0, LORA_PAD - LORA_ALL))),
        w0=w['w0'][l][None], a0=w['a0'][l][None], k_k=w['k_k'][l][None], k_a=w['k_a'][l][None],
        r_k=w['r_k'][l].reshape(1, B_W),
        w_up=pad_rows(w['w_up'][l], 0, LANES), a_up=pad_rows(w['a_up'][l], LORA_W, LANES),
        g_up=pad_rows(w['g_up'][l], 0, LORA_PAD - LANES), ones_bd=ones_bd)
    return dict(w_pool=w['w_pool'][l].astype(BF16), rwkv=rw)


def _layer(x, l, w, wl, wb, pos0, kv_prefix, shift_prev, wkv0, pool_prefix):
    cast = wb is None
    wb = {} if cast else wb
    b, t, _ = x.shape
    m = b * t

    def up(xn, gate, upw):
        return _ffn_up(xn, w[gate], w[upw], l)

    def mm(a, name, n, tm, res=None, scale=1.0, trans_b=False, copy=None, src_rows=None):
        copy = copy or name
        if not cast:
            return _matmul(a, wb[copy], res=res, scale=scale, tm=tm, trans_b=trans_b)
        out, wb[copy] = _matmul_cast(a, w[name], l, n, res=res, scale=scale, trans_b=trans_b, src_rows=src_rows)
        return out

    x2 = x.reshape(m, D_MODEL)
    hid = up(_rmsnorm(x2, w['ln_ffn1'][l], BF16), 'w1_gate', 'w1_up')
    h = mm(hid, 'w1_down', D_MODEL, 512, res=x2, scale=0.5)
    hn = _rmsnorm(h, w['ln_mix'][l], BF16)
    zm3 = mm(hn, 'w_in_t', ZM_W, 1024, trans_b=True).reshape(b, t, ZM_W)
    zr3 = mm(hn, 'w_in_t', ZR_W, 1024, trans_b=True, copy='w_rest_t', src_rows=_zr_src_rows).reshape(b, t, ZR_W)
    out_a, ka = _attention(zm3, pos0, kv_prefix, l)
    o_b, bonus, gate, wkv_new = _rwkv_mix(zm3, zr3, shift_prev, wkv0, wl['rwkv'])
    out_c = _gmlp(zr3, w['w_s'][l], w['b_s'][l])
    out_d = _pool(zr3, pool_prefix, pos0, wl['w_pool'], w['pool_scale'][l])
    r2 = lambda a: a.reshape(m, -1)
    mix = _mix(r2(out_a), r2(o_b), r2(bonus), r2(gate), r2(out_c), r2(out_d), w['g_out_a'][l], w['ln_x_w'][l],
               w['ln_x_b'][l], w['g_out_c'][l], w['g_out_d'][l], wl['rwkv']['ones_bd'])
    h = mm(mix, 'w_out', D_MODEL, 1024, res=h, scale=1.0)
    hid = up(_rmsnorm(h, w['ln_ffn2'][l], BF16), 'w2_gate', 'w2_up')
    y = mm(hid, 'w2_down', D_MODEL, 512, res=h, scale=0.5)
    hs = (b, t, N_HEADS_A, HEAD_DIM_A)
    va = zm3[..., ZM_VA:ZM_VA + A_W]
    shift_new = jnp.concatenate([zm3[:, -1, ZM_RKV:], zr3[:, -1, ZR_LORA:ZR_LORA + LORA_ALL]], axis=-1)
    p = zr3[..., ZR_D:ZR_D + D_W]
    pool_new = jnp.concatenate([pool_prefix, p], axis=1)[:, -POOL_PREV:]
    vc = zr3[..., ZR_CV:ZR_CV + C_W]
    return y.reshape(b, t, D_MODEL), (ka.reshape(hs), va.reshape(hs), wkv_new, shift_new, pool_new, vc), wb


def kernel(x_prompt, x_sample, cache_k_swa, cache_v_swa, state_rwkv_wkv, state_rwkv_shift, state_pool, ln_ffn1, w1_gate, w1_up, w1_down, ln_mix, w_in, g_out_a, mu_b, w0, w_up, a0, a_up, g_up, k_k, k_a, r_k, ln_x_w, ln_x_b, w_s, b_s, g_out_c, w_pool, pool_scale, g_out_d, w_out, ln_ffn2, w2_gate, w2_up, w2_down, ln_final):
    w = dict(ln_ffn1=ln_ffn1, w1_gate=w1_gate, w1_up=w1_up, w1_down=w1_down, ln_mix=ln_mix, w_in=w_in,
             g_out_a=g_out_a, mu_b=mu_b, w0=w0, w_up=w_up, a0=a0, a_up=a_up, g_up=g_up, k_k=k_k, k_a=k_a,
             r_k=r_k, ln_x_w=ln_x_w, ln_x_b=ln_x_b, w_s=w_s, b_s=b_s, g_out_c=g_out_c, w_pool=w_pool,
             pool_scale=pool_scale, g_out_d=g_out_d, w_out=w_out, ln_ffn2=ln_ffn2, w2_gate=w2_gate,
             w2_up=w2_up, w2_down=w2_down)
    nbp, t_prompt, _ = x_prompt.shape
    yp, ys = x_prompt, x_sample
    p_states, s_states = [], []
    w['w_in_t'] = jnp.swapaxes(w_in, 1, 2)
    for l in range(DEPTH):
        wl = _prep_layer_weights(l, w)
        ys, ss, wb = _layer(ys, l, w, wl, None, PAST_LEN, (cache_k_swa, cache_v_swa),
                            state_rwkv_shift[l], state_rwkv_wkv[l], state_pool[l])
        yp, sp, _ = _layer(yp, l, w, wl, wb, 0, None,
                           jnp.zeros((nbp, B_FEAT), F32),
                           jnp.zeros((nbp, N_HEADS_B, HEAD_DIM_B, HEAD_DIM_B), F32),
                           jnp.zeros((nbp, POOL_PREV, D_W), F32))
        p_states.append(sp)
        s_states.append(ss)
    keep = min(WIN_MAX, t_prompt)
    stack = lambda states, i: jnp.stack([s[i] for s in states])
    y_prompt = _rmsnorm(yp.reshape(-1, D_MODEL), ln_final, F32).reshape(yp.shape)
    y_sample = _rmsnorm(ys.reshape(-1, D_MODEL), ln_final, F32).reshape(ys.shape)
    return (y_prompt, y_sample,
            jnp.stack([s[0][:, -keep:] for s in p_states]), jnp.stack([s[1][:, -keep:] for s in p_states]),
            stack(p_states, 2), stack(p_states, 3), stack(p_states, 4),
            stack(s_states, 0), stack(s_states, 1), stack(s_states, 2), stack(s_states, 3), stack(s_states, 4),
            stack(s_states, 5))
```

```python
import functools

import jax
import jax.numpy as jnp
from jax import lax
from jax.experimental import pallas as pl
from jax.experimental.pallas import tpu as pltpu

F32 = jnp.float32
BF16 = jnp.bfloat16

D_MODEL = 4096
DEPTH = 2
PAST_LEN = 16384
A_W = B_W = C_W = D_W = D_MODEL // 4
RMS_EPS = 1e-6
HEAD_DIM_A = 128
N_HEADS_A = A_W // HEAD_DIM_A
DILATED_BRANCHES = ((128, 1), (512, 4), (2048, 16))
WIN_MAX = 2048
ROPE_THETA = 10000.0
HEAD_DIM_B = 64
N_HEADS_B = B_W // HEAD_DIM_B
LORA_W, LORA_A, LORA_G = 64, 64, 160
LORA_ALL = LORA_W + LORA_A + LORA_G
GN_EPS = 64e-5
B_FEAT = 3 * B_W + LORA_ALL
CHUNK = 128
N_GROUPS_C = 8
POOL_WINDOWS = (2, 4, 8, 16)
POOL_PREV = max(POOL_WINDOWS) - 1
POOL_GROUP = D_W // len(POOL_WINDOWS)
OFF_B = 3 * A_W
OFF_C = OFF_B + B_FEAT
OFF_D = OFF_C + 2 * C_W

LANES = 128
SUBLANES = 8
VMEM_BYTES_V7X = 64 * 2**20
VMEM_INTERNAL_RESERVE = 12 * 2**20

MXU_COLS_V7X = 256
MM_TN = 2 * MXU_COLS_V7X

ZM_QA, ZM_KA, ZM_VA = 0, A_W, 2 * A_W
ZM_RKV = 3 * A_W
ZM_W = ZM_RKV + 3 * B_W
ZR_CU = 0
ZR_CV = ZR_CU + C_W
ZR_D = ZR_CV + C_W
ZR_LORA = ZR_D + D_W
LORA_PAD = 3 * LANES
ZR_W = -(-(ZR_LORA + LORA_PAD) // MM_TN) * MM_TN

NEG = -1e30


def _vmem_limit(block_bytes, scratch_bytes=0):
    need = 2 * block_bytes + scratch_bytes + VMEM_INTERNAL_RESERVE
    return int(min(max(need, 16 * 2**20), VMEM_BYTES_V7X - 4 * 2**20))


def _params(sem, block_bytes, scratch_bytes=0):
    return pltpu.CompilerParams(dimension_semantics=sem,
                                vmem_limit_bytes=_vmem_limit(block_bytes, scratch_bytes))


def _nbytes(shape, dtype):
    n = 1
    for s in shape:
        n *= s
    return n * jnp.dtype(dtype).itemsize


def _rmsnorm_body(x_ref, g_ref, o_ref):
    x = x_ref[...]
    ms = jnp.mean(x * x, axis=-1, keepdims=True)
    o_ref[...] = (x * lax.rsqrt(ms + RMS_EPS) * g_ref[...]).astype(o_ref.dtype)


def _rmsnorm(x, g, out_dtype):
    m, d = x.shape
    tm = min(m, 256)
    blk = _nbytes((tm, d), F32) + _nbytes((tm, d), out_dtype)
    return pl.pallas_call(
        _rmsnorm_body, grid=(m // tm,),
        in_specs=[pl.BlockSpec((tm, d), lambda i: (i, 0)), pl.BlockSpec((1, d), lambda i: (0, 0))],
        out_specs=pl.BlockSpec((tm, d), lambda i: (i, 0)),
        out_shape=jax.ShapeDtypeStruct((m, d), out_dtype),
        compiler_params=_params(("parallel",), blk), name="rmsnorm")(x, g.reshape(1, d))


def _row_scale(ssq_ref, width):
    return lax.rsqrt(jnp.sum(ssq_ref[...], axis=-1, keepdims=True) * (1.0 / width) + RMS_EPS)


def _ffn_up_body(x_ref, wg_ref, wu_ref, *rest, norm_width):
    o_ref = rest[-1]
    x = x_ref[...]
    g = jnp.dot(x, wg_ref[...].astype(BF16), preferred_element_type=F32)
    u = jnp.dot(x, wu_ref[...].astype(BF16), preferred_element_type=F32)
    if norm_width:
        s = _row_scale(rest[0], norm_width)
        g = g * s
        u = u * s
    o_ref[...] = (g * jax.nn.sigmoid(g) * u).astype(o_ref.dtype)


def _ffn_up(x, wg, wu, l, ssq=None):
    m, k = x.shape
    n = wg.shape[2]
    tm = min(m, 1024)
    tn = MXU_COLS_V7X
    blk = _nbytes((tm, k), BF16) + 2 * _nbytes((k, tn), F32) + _nbytes((k, tn), BF16) + _nbytes((tm, tn), BF16)
    wspec = pl.BlockSpec((None, k, tn), lambda i, j: (l, 0, j))
    in_specs = [pl.BlockSpec((tm, k), lambda i, j: (i, 0)), wspec, wspec]
    args = [x, wg, wu]
    if ssq is not None:
        in_specs.append(pl.BlockSpec((tm, LANES), lambda i, j: (i, 0)))
        args.append(ssq)
    return pl.pallas_call(
        functools.partial(_ffn_up_body, norm_width=k if ssq is not None else 0), grid=(m // tm, n // tn),
        in_specs=in_specs, out_specs=pl.BlockSpec((tm, tn), lambda i, j: (i, j)),
        out_shape=jax.ShapeDtypeStruct((m, n), BF16),
        compiler_params=_params(("parallel", "arbitrary"), blk), name="ffn_up")(*args)


def _mm_body(a_ref, b_ref, o_ref):
    o_ref[...] = jnp.dot(a_ref[...], b_ref[...], preferred_element_type=F32)


def _mm_res_body(a_ref, b_ref, r_ref, *rest, scale, fuse_norm):
    acc = jnp.dot(a_ref[...], b_ref[...], preferred_element_type=F32)
    h = r_ref[...] + scale * acc
    if not fuse_norm:
        rest[0][...] = h
        return
    gain_ref, o_ref, hb_ref, ssq_ref = rest
    o_ref[...] = h
    hb_ref[...] = (h * gain_ref[...]).astype(BF16)
    hh = h * h
    part = hh[:, :LANES]
    for c in range(1, hh.shape[1] // LANES):
        part = part + hh[:, c * LANES:(c + 1) * LANES]

    @pl.when(pl.program_id(1) == 0)
    def _():
        ssq_ref[...] = part

    @pl.when(pl.program_id(1) > 0)
    def _():
        ssq_ref[...] += part


_TRANS_B = (((1,), (1,)), ((), ()))


def _mm_t_body(a_ref, bt_ref, *rest, norm_width):
    acc = lax.dot_general(a_ref[...], bt_ref[...], _TRANS_B, preferred_element_type=F32)
    if norm_width:
        acc = acc * _row_scale(rest[0], norm_width)
    rest[-1][...] = acc


def _mm_cast_body(a_ref, b_ref, *rest, scale, has_res, trans_b):
    r_ref = rest[0] if has_res else None
    o_ref, bo_ref = rest[-2:]
    b = (b_ref[0] if len(b_ref.shape) == 3 else b_ref[...]).astype(BF16)
    bo_ref[...] = b
    if trans_b:
        acc = lax.dot_general(a_ref[...], b, _TRANS_B, preferred_element_type=F32)
    else:
        acc = jnp.dot(a_ref[...], b, preferred_element_type=F32)
    o_ref[...] = r_ref[...] + scale * acc if has_res else acc


def _matmul(a, b, l=None, res=None, scale=1.0, tm=1024, trans_b=False, ssq=None, next_gain=None):
    m, k = a.shape
    n = b.shape[-2] if trans_b else b.shape[-1]
    tm = min(m, tm)
    tn = MM_TN
    assert m % tm == 0 and n % tn == 0
    blk = _nbytes((tm, k), BF16) + _nbytes((k, tn), BF16) + _nbytes((tm, tn), F32)
    bshape, bidx = ((tn, k), lambda i, j: (j, 0)) if trans_b else ((k, tn), lambda i, j: (0, j))
    bspec = (pl.BlockSpec(bshape, bidx) if b.ndim == 2
             else pl.BlockSpec((None,) + bshape, lambda i, j: (l,) + bidx(i, j)))
    in_specs = [pl.BlockSpec((tm, k), lambda i, j: (i, 0)), bspec]
    args = [a, b]
    tile = pl.BlockSpec((tm, tn), lambda i, j: (i, j))
    out_specs, out_shape = tile, jax.ShapeDtypeStruct((m, n), F32)
    if trans_b:
        assert res is None and next_gain is None
        body = functools.partial(_mm_t_body, norm_width=k if ssq is not None else 0)
        if ssq is not None:
            in_specs.append(pl.BlockSpec((tm, LANES), lambda i, j: (i, 0)))
            args.append(ssq)
    elif res is None:
        assert ssq is None and next_gain is None
        body = _mm_body
    else:
        assert ssq is None
        body = functools.partial(_mm_res_body, scale=scale, fuse_norm=next_gain is not None)
        in_specs.append(tile)
        args.append(res)
        blk += _nbytes((tm, tn), F32)
        if next_gain is not None:
            in_specs.append(pl.BlockSpec((1, tn), lambda i, j: (0, j)))
            args.append(next_gain.reshape(1, n))
            out_specs = [tile, tile, pl.BlockSpec((tm, LANES), lambda i, j: (i, 0))]
            out_shape = [out_shape, jax.ShapeDtypeStruct((m, n), BF16), jax.ShapeDtypeStruct((m, LANES), F32)]
            blk += _nbytes((tm, tn), BF16) + _nbytes((tm, LANES), F32)
    return pl.pallas_call(
        body, grid=(m // tm, n // tn), in_specs=in_specs, out_specs=out_specs, out_shape=out_shape,
        compiler_params=_params(("parallel", "arbitrary"), blk), name="matmul")(*args)


CAST_TILE_BYTES = 6 * 2**20


def _matmul_cast(a, b, l, n, res=None, scale=1.0, trans_b=False, src_rows=None):
    m, k = a.shape
    tn = MM_TN
    while _nbytes((k, tn), F32) > CAST_TILE_BYTES:
        tn //= 2
    assert n % tn == 0 and tn % LANES == 0
    blk = _nbytes((m, k), BF16) + _nbytes((k, tn), F32) + _nbytes((k, tn), BF16) + 2 * _nbytes((m, tn), F32)
    if trans_b:
        if src_rows is None:
            wspec = pl.BlockSpec((None, tn, k), lambda j: (l, j, 0))
        else:
            wspec = pl.BlockSpec((pl.Element(1), pl.Element(tn), pl.Element(k)), lambda j: (l, src_rows(j, tn), 0))
        cspec, cshape = pl.BlockSpec((tn, k), lambda j: (j, 0)), (n, k)
    else:
        wspec = pl.BlockSpec((None, k, tn), lambda j: (l, 0, j))
        cspec, cshape = pl.BlockSpec((k, tn), lambda j: (0, j)), (k, n)
    in_specs = [pl.BlockSpec((m, k), lambda j: (0, 0)), wspec]
    args = [a, b]
    if res is not None:
        in_specs.append(pl.BlockSpec((m, tn), lambda j: (0, j)))
        args.append(res)
    return pl.pallas_call(
        functools.partial(_mm_cast_body, scale=scale, has_res=res is not None, trans_b=trans_b), grid=(n // tn,),
        in_specs=in_specs, out_specs=[pl.BlockSpec((m, tn), lambda j: (0, j)), cspec],
        out_shape=[jax.ShapeDtypeStruct((m, n), F32), jax.ShapeDtypeStruct(cshape, BF16)],
        compiler_params=_params(("parallel",), blk), name="matmul_cast")(*args)


def _rope(x, cos, sin_signed):
    return x * cos + pltpu.roll(x, HEAD_DIM_A // 2, 1) * sin_signed


def _branch_multiplicity(delta):
    c = jnp.zeros(delta.shape, F32)
    for window, dilation in DILATED_BRANCHES:
        assert dilation & (dilation - 1) == 0
        hit = jnp.where(delta <= window, 1.0, 0.0)
        if dilation > 1:
            hit = jnp.where((delta & (dilation - 1)) == 0, hit, 0.0)
        c = c + hit
    return jnp.where(delta >= 0, c, 0.0)


def _attn_prompt_body(q_ref, k_ref, v_ref, cos_ref, sin_ref, o_ref, kout_ref, qs, ks, vs, ctab, *, t_len, tq):
    nq = t_len // tq

    @pl.when((pl.program_id(0) == 0) & (pl.program_id(1) == 0))
    def _():
        rel = (lax.broadcasted_iota(jnp.int32, (tq, tq), 0) - lax.broadcasted_iota(jnp.int32, (tq, tq), 1))
        for d in range(nq):
            ctab[:, d * tq:(d + 1) * tq] = _branch_multiplicity(rel + (nq - 1 - d) * tq)

    cos = cos_ref[...]
    sin = sin_ref[...]
    k = _rope(k_ref[0], cos, sin)
    kout_ref[0] = k
    ks[...] = k.astype(BF16)
    qs[...] = (_rope(q_ref[0], cos, sin) * (HEAD_DIM_A ** -0.5)).astype(BF16)
    vs[...] = v_ref[0].astype(BF16)
    for i in range(nq):
        kw = (i + 1) * tq
        s = lax.dot_general(qs[i * tq:(i + 1) * tq, :], ks[0:kw, :], _TRANS_B, preferred_element_type=F32)
        c = ctab[:, (nq - 1 - i) * tq:]
        sm = jnp.where(c > 0.0, s, NEG)
        p = jnp.exp(sm - jnp.max(sm, axis=-1, keepdims=True)) * c
        l = jnp.sum(p, axis=-1, keepdims=True)
        acc = jnp.dot(p.astype(BF16), vs[0:kw, :], preferred_element_type=F32)
        o_ref[0, i * tq:(i + 1) * tq, :] = acc / l


def _attn_sample_body(q_ref, k_ref, v_ref, kp_ref, vp_ref, cos_ref, sin_ref, o_ref, kout_ref, *, t_len, n_prev):
    cos = cos_ref[...]
    sin = sin_ref[...]
    d1 = (n_prev + lax.broadcasted_iota(jnp.int32, (t_len, n_prev), 0)
          - lax.broadcasted_iota(jnp.int32, (t_len, n_prev), 1))
    c1 = _branch_multiplicity(d1)
    d2 = (lax.broadcasted_iota(jnp.int32, (t_len, LANES), 0) - lax.broadcasted_iota(jnp.int32, (t_len, LANES), 1))
    c2 = _branch_multiplicity(d2)
    pad = jnp.zeros((LANES - t_len, HEAD_DIM_A), F32)
    for h in range(N_HEADS_A):
        sl = slice(h * HEAD_DIM_A, (h + 1) * HEAD_DIM_A)
        k = _rope(k_ref[0, :, sl], cos, sin)
        kout_ref[0, :, sl] = k
        q = (_rope(q_ref[0, :, sl], cos, sin) * (HEAD_DIM_A ** -0.5)).astype(BF16)
        kn = jnp.concatenate([k, pad], axis=0).astype(BF16)
        vn = jnp.concatenate([v_ref[0, :, sl], pad], axis=0).astype(BF16)
        s1 = lax.dot_general(q, kp_ref[:, h, :].astype(BF16), _TRANS_B, preferred_element_type=F32)
        s2 = lax.dot_general(q, kn, _TRANS_B, preferred_element_type=F32)
        sm1 = jnp.where(c1 > 0.0, s1, NEG)
        sm2 = jnp.where(c2 > 0.0, s2, NEG)
        m = jnp.maximum(jnp.max(sm1, axis=-1, keepdims=True), jnp.max(sm2, axis=-1, keepdims=True))
        p1 = jnp.exp(sm1 - m) * c1
        p2 = jnp.exp(sm2 - m) * c2
        l = jnp.sum(p1, axis=-1, keepdims=True) + jnp.sum(p2, axis=-1, keepdims=True)
        acc = (jnp.dot(p1.astype(BF16), vp_ref[:, h, :].astype(BF16), preferred_element_type=F32)
               + jnp.dot(p2.astype(BF16), vn, preferred_element_type=F32))
        o_ref[0, :, sl] = acc / l


def _rope_tables(pos0, t_len):
    half = HEAD_DIM_A // 2
    inv = ROPE_THETA ** (-jnp.arange(half, dtype=F32) / half)
    ang = (pos0 + jnp.arange(t_len)).astype(F32)[:, None] * inv[None, :]
    cos = jnp.cos(ang)
    sin = jnp.sin(ang)
    return jnp.concatenate([cos, cos], axis=-1), jnp.concatenate([-sin, sin], axis=-1)


def _attention(z3, pos0, kv_prefix, l):
    b, t, _ = z3.shape
    cos, sin = _rope_tables(pos0, t)
    hd = HEAD_DIM_A
    out_shape = [jax.ShapeDtypeStruct((b, t, A_W), F32), jax.ShapeDtypeStruct((b, t, A_W), F32)]
    if kv_prefix is None:
        col = lambda base: (lambda bi, hi: (bi, 0, base // hd + hi))
        tab = pl.BlockSpec((t, hd), lambda bi, hi: (0, 0))
        zspecs = [pl.BlockSpec((1, t, hd), col(ZM_QA)), pl.BlockSpec((1, t, hd), col(ZM_KA)),
                  pl.BlockSpec((1, t, hd), col(ZM_VA))]
        out_specs = [pl.BlockSpec((1, t, hd), col(0)), pl.BlockSpec((1, t, hd), col(0))]
        blk = 7 * _nbytes((t, hd), F32)
        tq = min(t, 256)
        body = functools.partial(_attn_prompt_body, t_len=t, tq=tq)
        scratch = 3 * _nbytes((t, hd), BF16) + 5 * _nbytes((tq, t), F32)
        return pl.pallas_call(
            body, grid=(b, N_HEADS_A), in_specs=zspecs + [tab, tab], out_specs=out_specs, out_shape=out_shape,
            scratch_shapes=[pltpu.VMEM((t, hd), BF16)] * 3 + [pltpu.VMEM((tq, t), F32)],
            compiler_params=_params(("arbitrary", "arbitrary"), blk, scratch),
            name="attn_prompt")(z3, z3, z3, cos, sin)
    k_prev, v_prev = kv_prefix
    n_prev = k_prev.shape[2]
    assert t <= LANES and n_prev % LANES == 0 and k_prev.shape[3:] == (N_HEADS_A, hd)
    col = lambda base: (lambda bi: (bi, 0, base // A_W))
    tab = pl.BlockSpec((t, hd), lambda bi: (0, 0))
    zspecs = [pl.BlockSpec((1, t, A_W), col(ZM_QA)), pl.BlockSpec((1, t, A_W), col(ZM_KA)),
              pl.BlockSpec((1, t, A_W), col(ZM_VA))]
    out_specs = [pl.BlockSpec((1, t, A_W), col(0)), pl.BlockSpec((1, t, A_W), col(0))]
    pspec = pl.BlockSpec((None, None, n_prev, N_HEADS_A, hd), lambda bi: (l, bi, 0, 0, 0))
    blk = 5 * _nbytes((t, A_W), F32) + 2 * _nbytes((n_prev, A_W), F32)
    body = functools.partial(_attn_sample_body, t_len=t, n_prev=n_prev)
    return pl.pallas_call(
        body, grid=(b,), in_specs=zspecs + [pspec, pspec, tab, tab], out_specs=out_specs,
        out_shape=out_shape, compiler_params=_params(("parallel",), blk),
        name="attn_sample")(z3, z3, z3, k_prev, v_prev, cos, sin)


def _group_sum(x, ones_blockdiag):
    outs = []
    for j in range(x.shape[-1] // LANES):
        outs.append(jnp.dot(x[:, j * LANES:(j + 1) * LANES], ones_blockdiag, preferred_element_type=F32,
                            precision=lax.Precision.HIGHEST))
    return jnp.concatenate(outs, axis=-1)


def _softplus(y):
    return jnp.maximum(y, 0.0) + jnp.log1p(jnp.exp(-jnp.abs(y)))


def _rwkv_prep_body(x_ref, lo_ref, sx_ref, slo_ref, mux_ref, mulo_ref, w0_ref, a0_ref, kkp_ref, kap_ref, rk_ref,
                    wup_ref, aup_ref, gup_ref, ones_ref,
                    r_o, w_o, k_o, v_o, kk_o, b_o, g_o, bonus_o, last_x, last_lo, *, tt, channel_major):
    @pl.when(pl.program_id(1) == 0)
    def _():
        last_x[0:1, :] = sx_ref[0]
        last_lo[0:1, :] = slo_ref[0]

    x = x_ref[0]
    lo = lo_ref[0]
    first = lax.broadcasted_iota(jnp.int32, (tt, 1), 0) == 0
    px = jnp.where(first, last_x[0:1, :], pltpu.roll(x, 1, 0))
    plo = jnp.where(first, last_lo[0:1, :], pltpu.roll(lo, 1, 0))
    last_x[0:1, :] = x[tt - 1:tt, :]
    last_lo[0:1, :] = lo[tt - 1:tt, :]
    fx = x + mux_ref[...] * (px - x)
    flo = lo + mulo_ref[...] * (plo - lo)
    r = fx[:, :B_W]
    k = fx[:, B_W:2 * B_W]
    v = fx[:, 2 * B_W:]
    zwa = flo[:, :LANES]
    zg = flo[:, LANES:]
    ones_bd = ones_ref[...]
    wl = w0_ref[...] + jnp.dot(jnp.tanh(zwa).astype(BF16), wup_ref[...], preferred_element_type=F32)
    w_log = -_softplus(-wl) - 0.5
    decay = jnp.exp(-jnp.exp(w_log))
    a = jax.nn.sigmoid(a0_ref[...] + jnp.dot(zwa.astype(BF16), aup_ref[...], preferred_element_type=F32))
    g = jnp.dot(jax.nn.sigmoid(zg).astype(BF16), gup_ref[...], preferred_element_type=F32)
    kk = k * kkp_ref[...]
    kk = kk / jnp.maximum(jnp.sqrt(_group_sum(kk * kk, ones_bd)), 1e-12)
    kmod = k * (1.0 + (a - 1.0) * kap_ref[...])
    lay = (lambda y: y.T) if channel_major else (lambda y: y)
    r_o[0] = lay(r)
    w_o[0] = lay(decay)
    k_o[0] = lay(kmod)
    v_o[0] = lay(v)
    kk_o[0] = lay(kk)
    b_o[0] = lay(kk * a)
    g_o[0] = g
    bonus_o[0] = _group_sum(r * kmod * rk_ref[...], ones_bd) * v


def _rwkv_prep(zm3, zr3, shift_prev, p, channel_major):
    b, t, _ = zm3.shape
    tt = min(t, 256)
    sx = shift_prev[:, None, :3 * B_W]
    slo = jnp.pad(shift_prev[:, None, 3 * B_W:], ((0, 0), (0, 0), (0, LORA_PAD - LORA_ALL)))
    row = lambda w: pl.BlockSpec((1, w), lambda bi, ti: (0, 0))
    full = lambda a: pl.BlockSpec(a.shape, lambda bi, ti: (0,) * a.ndim)
    in_specs = [pl.BlockSpec((1, tt, 3 * B_W), lambda bi, ti: (bi, ti, ZM_RKV // (3 * B_W))),
                pl.BlockSpec((1, tt, LORA_PAD), lambda bi, ti: (bi, ti, ZR_LORA // LORA_PAD)),
                pl.BlockSpec((1, 1, 3 * B_W), lambda bi, ti: (bi, 0, 0)),
                pl.BlockSpec((1, 1, LORA_PAD), lambda bi, ti: (bi, 0, 0)),
                row(3 * B_W), row(LORA_PAD), row(B_W), row(B_W), row(B_W), row(B_W), row(B_W),
                full(p['w_up']), full(p['a_up']), full(p['g_up']), full(p['ones_bd'])]
    ospec = pl.BlockSpec((1, tt, B_W), lambda bi, ti: (bi, ti, 0))
    oshape = jax.ShapeDtypeStruct((b, t, B_W), F32)
    if channel_major:
        rspec, rshape = pl.BlockSpec((1, B_W, tt), lambda bi, ti: (bi, 0, ti)), jax.ShapeDtypeStruct((b, B_W, t), F32)
    else:
        rspec, rshape = ospec, oshape
    blk = _nbytes((tt, 3 * B_W + LORA_PAD), F32) + 8 * _nbytes((tt, B_W), F32) + 2 * _nbytes((512, B_W), F32)
    return pl.pallas_call(
        functools.partial(_rwkv_prep_body, tt=tt, channel_major=channel_major), grid=(b, t // tt),
        in_specs=in_specs, out_specs=[rspec] * 6 + [ospec] * 2, out_shape=[rshape] * 6 + [oshape] * 2,
        scratch_shapes=[pltpu.VMEM((SUBLANES, 3 * B_W), F32), pltpu.VMEM((SUBLANES, LORA_PAD), F32)],
        compiler_params=_params(("parallel", "arbitrary"), blk), name="rwkv_prep")(
            zm3, zr3, sx, slo, p['mu_x'], p['mu_lo'], p['w0'], p['a0'], p['k_k'], p['k_a'], p['r_k'],
            p['w_up'], p['a_up'], p['g_up'], p['ones_bd'])


N_PARTIAL = 4


def _rwkv_steps(r_ref, w_ref, k_ref, kk_ref, b_ref, v_ref, o_ref, state, *, tc, vp, time_major):
    def step(t, row):
        parts = [jnp.zeros((vp, LANES), F32)] * N_PARTIAL
        for k in range(HEAD_DIM_B):
            parts[k % N_PARTIAL] = parts[k % N_PARTIAL] + state[k] * row(kk_ref, k)
        sa = -((parts[0] + parts[1]) + (parts[2] + parts[3]))
        vt = v_ref[t]
        parts = [jnp.zeros((vp, LANES), F32)] * N_PARTIAL
        for k in range(HEAD_DIM_B):
            s = state[k] * row(w_ref, k) + sa * row(b_ref, k) + vt * row(k_ref, k)
            state[k] = s
            parts[k % N_PARTIAL] = parts[k % N_PARTIAL] + s * row(r_ref, k)
        o_ref[t] = (parts[0] + parts[1]) + (parts[2] + parts[3])

    if time_major:
        def one(t, carry):
            step(t, lambda ref, k: ref[t, pl.ds(k, 1), :])
            return carry

        lax.fori_loop(0, tc, one, 0)
    else:
        def eight(tb, carry):
            for s in range(SUBLANES):
                step(tb * SUBLANES + s, lambda ref, k, s=s: ref[k, tb, pl.ds(s, 1), :])
            return carry

        lax.fori_loop(0, tc // SUBLANES, eight, 0)


def _rwkv_scan_body(r_ref, w_ref, k_ref, kk_ref, b_ref, v_ref, s0_ref, o_ref, sout_ref, state, *, tc, vp):
    @pl.when(pl.program_id(0) == 0)
    def _():
        state[...] = s0_ref[...]

    _rwkv_steps(r_ref, w_ref, k_ref, kk_ref, b_ref, v_ref, o_ref, state, tc=tc, vp=vp, time_major=True)

    @pl.when(pl.program_id(0) == pl.num_programs(0) - 1)
    def _():
        sout_ref[...] = state[...]


def _rwkv_scan_rows_body(r_ref, w_ref, k_ref, kk_ref, b_ref, v_ref, s0_ref, o_ref, sout_ref,
                         state, yt, rk, wk, kkey, kkk, bk, vk, ok, *, tc, nb, dup):
    vp = HEAD_DIM_B // dup
    nh = N_HEADS_B

    @pl.when(pl.program_id(0) == 0)
    def _():
        state[...] = s0_ref[...]

    def to_lanes(x_ref, store, n_rows, row_of):
        for j in range(n_rows):
            pieces = [x_ref[b, pl.ds(row_of(j, vh), nh, stride=HEAD_DIM_B), :] for b in range(nb) for vh in range(dup)]
            store(j, jnp.concatenate(pieces, axis=0).T)

    def key_store(dst):
        def store(j, x):
            dst[j] = x.reshape(tc // SUBLANES, SUBLANES, LANES)
        return store

    def val_store(j, x):
        vk[:, j, :] = x

    for x_ref, dst in ((r_ref, rk), (w_ref, wk), (k_ref, kkey), (kk_ref, kkk), (b_ref, bk)):
        to_lanes(x_ref, key_store(dst), HEAD_DIM_B, lambda j, vh: j)
    to_lanes(v_ref, val_store, vp, lambda j, vh: vh * vp + j)

    _rwkv_steps(rk, wk, kkey, kkk, bk, vk, ok, state, tc=tc, vp=vp, time_major=False)

    for j in range(vp):
        m = ok[:, j, :].T
        for b in range(nb):
            for vh in range(dup):
                lane0 = (b * dup + vh) * nh
                yt[b, pl.ds(vh * vp + j, nh, stride=HEAD_DIM_B), :] = m[lane0:lane0 + nh, :]
    for b in range(nb):
        o_ref[b] = yt[b].T

    @pl.when(pl.program_id(0) == pl.num_programs(0) - 1)
    def _():
        sout_ref[...] = state[...]


def _rwkv_scan_rows(r, w, k, kk, bv, v, s0, dup):
    nb, _, t = r.shape
    vp = HEAD_DIM_B // dup
    tc = LANES
    assert t % tc == 0 and nb * dup * N_HEADS_B == LANES
    cspec = pl.BlockSpec((nb, B_W, tc), lambda i: (0, 0, i))
    xspec = pl.BlockSpec((nb, tc, B_W), lambda i: (0, i, 0))
    sspec = pl.BlockSpec((HEAD_DIM_B, vp, LANES), lambda i: (0, 0, 0))
    key_tile = pltpu.VMEM((HEAD_DIM_B, tc // SUBLANES, SUBLANES, LANES), F32)
    val_tile = pltpu.VMEM((tc, vp, LANES), F32)
    blk = 7 * _nbytes((nb, tc, B_W), F32) + 2 * _nbytes((HEAD_DIM_B, vp, LANES), F32)
    scr = (_nbytes((HEAD_DIM_B, vp, LANES), F32) + _nbytes((nb, B_W, tc), F32)
           + 5 * _nbytes((tc, HEAD_DIM_B, LANES), F32) + 2 * _nbytes((tc, vp, LANES), F32))
    return pl.pallas_call(
        functools.partial(_rwkv_scan_rows_body, tc=tc, nb=nb, dup=dup), grid=(t // tc,),
        in_specs=[cspec] * 6 + [sspec], out_specs=[xspec, sspec],
        out_shape=[jax.ShapeDtypeStruct((nb, t, B_W), F32), jax.ShapeDtypeStruct((HEAD_DIM_B, vp, LANES), F32)],
        scratch_shapes=[pltpu.VMEM((HEAD_DIM_B, vp, LANES), F32), pltpu.VMEM((nb, B_W, tc), F32)]
        + [key_tile] * 5 + [val_tile] * 2,
        compiler_params=_params(("arbitrary",), blk, scr), name="rwkv_scan_rows")(r, w, k, kk, bv, v, s0)


def _rwkv_scan(r, w, k, kk, bv, v, s0):
    t, vp, _ = v.shape
    tc = min(t, 64)
    kspec = pl.BlockSpec((tc, HEAD_DIM_B, LANES), lambda i: (i, 0, 0))
    vspec = pl.BlockSpec((tc, vp, LANES), lambda i: (i, 0, 0))
    sspec = pl.BlockSpec((HEAD_DIM_B, vp, LANES), lambda i: (0, 0, 0))
    blk = 5 * _nbytes((tc, HEAD_DIM_B, LANES), F32) + 2 * _nbytes((tc, vp, LANES), F32) \
        + 2 * _nbytes((HEAD_DIM_B, vp, LANES), F32)
    return pl.pallas_call(
        functools.partial(_rwkv_scan_body, tc=tc, vp=vp), grid=(t // tc,),
        in_specs=[kspec] * 5 + [vspec, sspec], out_specs=[vspec, sspec],
        out_shape=[jax.ShapeDtypeStruct((t, vp, LANES), F32), jax.ShapeDtypeStruct((HEAD_DIM_B, vp, LANES), F32)],
        scratch_shapes=[pltpu.VMEM((HEAD_DIM_B, vp, LANES), F32)],
        compiler_params=_params(("arbitrary",), blk, _nbytes((HEAD_DIM_B, vp, LANES), F32)),
        name="rwkv_scan")(r, w, k, kk, bv, v, s0)


def _rwkv_mix(zm3, zr3, shift_prev, wkv0, p):
    b, t, _ = zm3.shape
    nh, hd = N_HEADS_B, HEAD_DIM_B
    in_vmem_relayout = t % LANES == 0
    r, w, kmod, v, kk, bv, g, bonus = _rwkv_prep(zm3, zr3, shift_prev, p, channel_major=in_vmem_relayout)
    dup = LANES // (b * nh)
    assert dup * b * nh == LANES and hd % dup == 0
    vp = hd // dup

    def key_layout(x):
        y = x.reshape(b, t, nh, hd).transpose(1, 3, 0, 2)[:, :, :, None]
        return jnp.broadcast_to(y, (t, hd, b, dup, nh)).reshape(t, hd, LANES)

    def val_layout(x):
        return x.reshape(b, t, nh, dup, vp).transpose(1, 4, 0, 3, 2).reshape(t, vp, LANES)

    s0 = wkv0.reshape(b, nh, dup, vp, hd).transpose(4, 3, 0, 2, 1).reshape(hd, vp, LANES)
    if in_vmem_relayout:
        o, s = _rwkv_scan_rows(r, w, kmod, kk, bv, v, s0, dup)
    else:
        o, s = _rwkv_scan(key_layout(r), key_layout(w), key_layout(kmod), key_layout(kk), key_layout(bv),
                          val_layout(v), s0)
        o = o.reshape(t, vp, b, dup, nh).transpose(2, 0, 4, 3, 1).reshape(b, t, B_W)
    s = s.reshape(hd, vp, b, dup, nh).transpose(2, 4, 3, 1, 0).reshape(b, nh, hd, hd)
    return o, bonus, g, s


def _gmlp_body(u_ref, v_ref, ws_ref, b_ref, o_ref, *, tc):
    keep = (lax.broadcasted_iota(jnp.int32, (CHUNK, CHUNK), 1) <= lax.broadcasted_iota(jnp.int32, (CHUNK, CHUNK), 0))
    for g in range(N_GROUPS_C):
        sl = slice(g * CHUNK, (g + 1) * CHUNK)
        w = jnp.where(keep, ws_ref[g], 0.0).astype(BF16)
        v = v_ref[0, :, sl]
        if tc < CHUNK:
            v = jnp.concatenate([v, jnp.zeros((CHUNK - tc, CHUNK), F32)], axis=0)
        s = jnp.dot(w, v.astype(BF16), preferred_element_type=F32) + b_ref[g]
        o_ref[0, :, sl] = u_ref[0, :, sl] * s[:tc]


def _gmlp(z3, w_s, b_s):
    b, t, _ = z3.shape
    tc = min(t, CHUNK)
    assert t % tc == 0
    blk = 3 * _nbytes((tc, C_W), F32) + 2 * _nbytes((N_GROUPS_C, CHUNK, CHUNK), F32)
    return pl.pallas_call(
        functools.partial(_gmlp_body, tc=tc), grid=(b, t // tc),
        in_specs=[pl.BlockSpec((1, tc, C_W), lambda bi, ci: (bi, ci, ZR_CU // C_W)),
                  pl.BlockSpec((1, tc, C_W), lambda bi, ci: (bi, ci, ZR_CV // C_W)),
                  pl.BlockSpec((N_GROUPS_C, CHUNK, CHUNK), lambda bi, ci: (0, 0, 0)),
                  pl.BlockSpec((N_GROUPS_C, CHUNK, 1), lambda bi, ci: (0, 0, 0))],
        out_specs=pl.BlockSpec((1, tc, C_W), lambda bi, ci: (bi, ci, 0)),
        out_shape=jax.ShapeDtypeStruct((b, t, C_W), F32),
        compiler_params=_params(("parallel", "parallel"), blk), name="gmlp")(z3, z3, w_s, b_s[:, :, None])


POOL_HALO = 16


def _pool_body(p_ref, pre_ref, w_ref, sc_ref, o_ref, ext, *, t_len, pos0, tc):
    ext[0:POOL_HALO, :] = pre_ref[0]
    ext[POOL_HALO:POOL_HALO + t_len, :] = p_ref[0]
    for c0 in range(0, t_len, tc):
        pos = pos0 + c0 + lax.broadcasted_iota(jnp.int32, (tc, 1), 0)
        for g, win in enumerate(POOL_WINDOWS):
            sl = slice(g * POOL_GROUP, (g + 1) * POOL_GROUP)
            base = POOL_HALO + c0
            acc = ext[base:base + tc, sl]
            for i in range(1, win):
                acc = acc + ext[base - i:base - i + tc, sl]
            cnt = jnp.minimum(win, pos + 1).astype(F32)
            pooled = acc / cnt - p_ref[0, c0:c0 + tc, sl]
            y = jnp.dot(pooled.astype(BF16), w_ref[g], preferred_element_type=F32)
            o_ref[0, c0:c0 + tc, sl] = y * sc_ref[:, sl]


def _pool(z3, prefix, pos0, w_pool, scale):
    b, t, _ = z3.shape
    assert POOL_PREV < POOL_HALO
    pre = jnp.pad(prefix, ((0, 0), (POOL_HALO - POOL_PREV, 0), (0, 0)))
    tc = min(t, 256)
    blk = 2 * _nbytes((t, D_W), F32) + _nbytes((POOL_HALO, D_W), F32) + _nbytes(w_pool.shape, BF16)
    scr = _nbytes((t + POOL_HALO, D_W), F32)
    return pl.pallas_call(
        functools.partial(_pool_body, t_len=t, pos0=pos0, tc=tc), grid=(b,),
        in_specs=[pl.BlockSpec((1, t, D_W), lambda bi: (bi, 0, ZR_D // D_W)),
                  pl.BlockSpec((1, POOL_HALO, D_W), lambda bi: (bi, 0, 0)),
                  pl.BlockSpec(w_pool.shape, lambda bi: (0, 0, 0)),
                  pl.BlockSpec((1, D_W), lambda bi: (0, 0))],
        out_specs=pl.BlockSpec((1, t, D_W), lambda bi: (bi, 0, 0)),
        out_shape=jax.ShapeDtypeStruct((b, t, D_W), F32),
        scratch_shapes=[pltpu.VMEM((t + POOL_HALO, D_W), F32)],
        compiler_params=_params(("parallel",), blk, scr), name="pool")(z3, pre, w_pool, scale.reshape(1, D_W))


def _mix_body(oa_ref, ob_ref, bonus_ref, g_ref, oc_ref, od_ref, ga_ref, lnw_ref, lnb_ref, gc_ref, gd_ref, ones_ref,
              o_ref):
    def rms(x, gain):
        ms = jnp.mean(x * x, axis=-1, keepdims=True)
        return x * lax.rsqrt(ms + RMS_EPS) * gain

    ones_bd = ones_ref[...]
    o = ob_ref[...]
    mean = _group_sum(o, ones_bd) * (1.0 / HEAD_DIM_B)
    d = o - mean
    var = _group_sum(d * d, ones_bd) * (1.0 / HEAD_DIM_B)
    ob = d * lax.rsqrt(var + GN_EPS) * lnw_ref[...] + lnb_ref[...]
    ob = (ob + bonus_ref[...]) * g_ref[...]
    o_ref[:, 0:A_W] = rms(oa_ref[...], ga_ref[...]).astype(BF16)
    o_ref[:, A_W:A_W + B_W] = ob.astype(BF16)
    o_ref[:, A_W + B_W:A_W + B_W + C_W] = rms(oc_ref[...], gc_ref[...]).astype(BF16)
    o_ref[:, A_W + B_W + C_W:] = rms(od_ref[...], gd_ref[...]).astype(BF16)


def _mix(oa, ob, bonus, g, oc, od, ga, lnw, lnb, gc, gd, ones_bd):
    m = oa.shape[0]
    tm = min(m, 256)
    act = pl.BlockSpec((tm, A_W), lambda i: (i, 0))
    row = pl.BlockSpec((1, A_W), lambda i: (0, 0))
    blk = 6 * _nbytes((tm, A_W), F32) + _nbytes((tm, D_MODEL), BF16) + _nbytes((512, A_W), F32)
    r1 = lambda a: a.reshape(1, -1)
    return pl.pallas_call(
        _mix_body, grid=(m // tm,),
        in_specs=[act] * 6 + [row] * 5 + [pl.BlockSpec((LANES, LANES), lambda i: (0, 0))],
        out_specs=pl.BlockSpec((tm, D_MODEL), lambda i: (i, 0)),
        out_shape=jax.ShapeDtypeStruct((m, D_MODEL), BF16),
        compiler_params=_params(("parallel",), blk), name="mix")(
            oa, ob, bonus, g, oc, od, r1(ga), r1(lnw), r1(lnb), r1(gc), r1(gd), ones_bd)


def _zr_src_rows(j, tn):
    assert ZM_W == OFF_B + 3 * B_W and (ZR_LORA - ZR_CU) % tn == 0 and OFF_C % 32 == 0 and ZM_W % 32 == 0
    n_cd = (ZR_LORA - ZR_CU) // tn
    return pl.multiple_of(jnp.where(j < n_cd, OFF_C + j * tn, ZM_W + (j - n_cd) * tn), 32)


def _prep_layer_weights(l, w):
    mu = w['mu_b'][l]
    pad_rows = lambda a, before, total: jnp.pad(a, ((before, total - before - a.shape[0]), (0, 0))).astype(BF16)
    ones_bd = jnp.kron(jnp.eye(LANES // HEAD_DIM_B, dtype=F32), jnp.ones((HEAD_DIM_B, HEAD_DIM_B), F32))
    rw = dict(
        mu_x=mu[None, :3 * B_W], mu_lo=jnp.pad(mu[None, 3 * B_W:], ((0, 0), (0, LORA_PAD - LORA_ALL))),
        w0=w['w0'][l][None], a0=w['a0'][l][None], k_k=w['k_k'][l][None], k_a=w['k_a'][l][None],
        r_k=w['r_k'][l].reshape(1, B_W),
        w_up=pad_rows(w['w_up'][l], 0, LANES), a_up=pad_rows(w['a_up'][l], LORA_W, LANES),
        g_up=pad_rows(w['g_up'][l], 0, LORA_PAD - LANES), ones_bd=ones_bd)
    return dict(w_pool=w['w_pool'][l].astype(BF16), rwkv=rw)


def _layer(x, l, w, wl, wb, pos0, kv_prefix, shift_prev, wkv0, pool_prefix, x_norm=None, next_gain=None):
    cast = wb is None
    wb = {} if cast else wb
    b, t, _ = x.shape
    m = b * t

    def proj(an, name, n, copy, src_rows=None):
        if not cast:
            return _matmul(an[0], wb[copy], trans_b=True, ssq=an[1])
        out, wb[copy] = _matmul_cast(an[0], w[name], l, n, trans_b=True, src_rows=src_rows)
        return out

    def res_mm(a, name, tm, res, scale, gain):
        if cast:
            out, wb[name] = _matmul_cast(a, w[name], l, D_MODEL, res=res, scale=scale)
            return out, (None if gain is None else (_rmsnorm(out, gain, BF16), None))
        if gain is None:
            return _matmul(a, wb[name], res=res, scale=scale, tm=tm), None
        out, ob, ssq = _matmul(a, wb[name], res=res, scale=scale, tm=tm, next_gain=gain)
        return out, (ob, ssq)

    x2 = x.reshape(m, D_MODEL)
    if x_norm is None:
        x_norm = (_rmsnorm(x2, w['ln_ffn1'][l], BF16), None)
    hid = _ffn_up(x_norm[0], w['w1_gate'], w['w1_up'], l, ssq=x_norm[1])
    h, hn = res_mm(hid, 'w1_down', 512, x2, 0.5, w['ln_mix'][l])
    zm3 = proj(hn, 'w_in_t', ZM_W, 'w_in_t').reshape(b, t, ZM_W)
    zr3 = proj(hn, 'w_in_t', ZR_W, 'w_rest_t', _zr_src_rows).reshape(b, t, ZR_W)
    out_a, ka = _attention(zm3, pos0, kv_prefix, l)
    o_b, bonus, gate, wkv_new = _rwkv_mix(zm3, zr3, shift_prev, wkv0, wl['rwkv'])
    out_c = _gmlp(zr3, w['w_s'][l], w['b_s'][l])
    out_d = _pool(zr3, pool_prefix, pos0, wl['w_pool'], w['pool_scale'][l])
    r2 = lambda a: a.reshape(m, -1)
    mix = _mix(r2(out_a), r2(o_b), r2(bonus), r2(gate), r2(out_c), r2(out_d), w['g_out_a'][l], w['ln_x_w'][l],
               w['ln_x_b'][l], w['g_out_c'][l], w['g_out_d'][l], wl['rwkv']['ones_bd'])
    h, hn = res_mm(mix, 'w_out', 1024, h, 1.0, w['ln_ffn2'][l])
    hid = _ffn_up(hn[0], w['w2_gate'], w['w2_up'], l, ssq=hn[1])
    y, y_norm = res_mm(hid, 'w2_down', 512, h, 0.5, next_gain)
    hs = (b, t, N_HEADS_A, HEAD_DIM_A)
    va = zm3[..., ZM_VA:ZM_VA + A_W]
    shift_new = jnp.concatenate([zm3[:, -1, ZM_RKV:], zr3[:, -1, ZR_LORA:ZR_LORA + LORA_ALL]], axis=-1)
    p = zr3[..., ZR_D:ZR_D + D_W]
    pool_new = jnp.concatenate([pool_prefix, p], axis=1)[:, -POOL_PREV:]
    vc = zr3[..., ZR_CV:ZR_CV + C_W]
    states = (ka.reshape(hs), va.reshape(hs), wkv_new, shift_new, pool_new, vc)
    return y.reshape(b, t, D_MODEL), y_norm, states, wb


def kernel(x_prompt, x_sample, cache_k_swa, cache_v_swa, state_rwkv_wkv, state_rwkv_shift, state_pool, ln_ffn1, w1_gate, w1_up, w1_down, ln_mix, w_in, g_out_a, mu_b, w0, w_up, a0, a_up, g_up, k_k, k_a, r_k, ln_x_w, ln_x_b, w_s, b_s, g_out_c, w_pool, pool_scale, g_out_d, w_out, ln_ffn2, w2_gate, w2_up, w2_down, ln_final):
    w = dict(ln_ffn1=ln_ffn1, w1_gate=w1_gate, w1_up=w1_up, w1_down=w1_down, ln_mix=ln_mix, w_in=w_in,
             g_out_a=g_out_a, mu_b=mu_b, w0=w0, w_up=w_up, a0=a0, a_up=a_up, g_up=g_up, k_k=k_k, k_a=k_a,
             r_k=r_k, ln_x_w=ln_x_w, ln_x_b=ln_x_b, w_s=w_s, b_s=b_s, g_out_c=g_out_c, w_pool=w_pool,
             pool_scale=pool_scale, g_out_d=g_out_d, w_out=w_out, ln_ffn2=ln_ffn2, w2_gate=w2_gate,
             w2_up=w2_up, w2_down=w2_down)
    nbp, t_prompt, _ = x_prompt.shape
    yp, ys = x_prompt, x_sample
    p_states, s_states = [], []
    w['w_in_t'] = jnp.swapaxes(w_in, 1, 2)
    yp_norm = ys_norm = None
    for l in range(DEPTH):
        wl = _prep_layer_weights(l, w)
        next_gain = ln_ffn1[l + 1] if l + 1 < DEPTH else None
        ys, ys_norm, ss, wb = _layer(ys, l, w, wl, None, PAST_LEN, (cache_k_swa, cache_v_swa),
                                     state_rwkv_shift[l], state_rwkv_wkv[l], state_pool[l],
                                     x_norm=ys_norm, next_gain=next_gain)
        yp, yp_norm, sp, _ = _layer(yp, l, w, wl, wb, 0, None,
                                    jnp.zeros((nbp, B_FEAT), F32),
                                    jnp.zeros((nbp, N_HEADS_B, HEAD_DIM_B, HEAD_DIM_B), F32),
                                    jnp.zeros((nbp, POOL_PREV, D_W), F32),
                                    x_norm=yp_norm, next_gain=next_gain)
        p_states.append(sp)
        s_states.append(ss)
    keep = min(WIN_MAX, t_prompt)
    stack = lambda states, i: jnp.stack([s[i] for s in states])
    y_prompt = _rmsnorm(yp.reshape(-1, D_MODEL), ln_final, F32).reshape(yp.shape)
    y_sample = _rmsnorm(ys.reshape(-1, D_MODEL), ln_final, F32).reshape(ys.shape)
    return (y_prompt, y_sample,
            jnp.stack([s[0][:, -keep:] for s in p_states]), jnp.stack([s[1][:, -keep:] for s in p_states]),
            stack(p_states, 2), stack(p_states, 3), stack(p_states, 4),
            stack(s_states, 0), stack(s_states, 1), stack(s_states, 2), stack(s_states, 3), stack(s_states, 4),
            stack(s_states, 5))
```

```python
import functools

import jax
import jax.numpy as jnp
from jax import lax
from jax.experimental import pallas as pl
from jax.experimental.pallas import tpu as pltpu

F32 = jnp.float32
BF16 = jnp.bfloat16

D_MODEL = 4096
DEPTH = 2
PAST_LEN = 16384
A_W = B_W = C_W = D_W = D_MODEL // 4
RMS_EPS = 1e-6
HEAD_DIM_A = 128
N_HEADS_A = A_W // HEAD_DIM_A
DILATED_BRANCHES = ((128, 1), (512, 4), (2048, 16))
WIN_MAX = 2048
ROPE_THETA = 10000.0
HEAD_DIM_B = 64
N_HEADS_B = B_W // HEAD_DIM_B
LORA_W, LORA_A, LORA_G = 64, 64, 160
LORA_ALL = LORA_W + LORA_A + LORA_G
GN_EPS = 64e-5
B_FEAT = 3 * B_W + LORA_ALL
CHUNK = 128
N_GROUPS_C = 8
POOL_WINDOWS = (2, 4, 8, 16)
POOL_PREV = max(POOL_WINDOWS) - 1
POOL_GROUP = D_W // len(POOL_WINDOWS)
OFF_B = 3 * A_W
OFF_C = OFF_B + B_FEAT
OFF_D = OFF_C + 2 * C_W

LANES = 128
SUBLANES = 8
VMEM_BYTES_V7X = 64 * 2**20
VMEM_INTERNAL_RESERVE = 12 * 2**20

MXU_COLS_V7X = 256
MM_TN = 2 * MXU_COLS_V7X

ZM_QA, ZM_KA, ZM_VA = 0, A_W, 2 * A_W
ZM_RKV = 3 * A_W
ZM_W = ZM_RKV + 3 * B_W
ZR_CU = 0
ZR_CV = ZR_CU + C_W
ZR_D = ZR_CV + C_W
ZR_LORA = ZR_D + D_W
LORA_PAD = 3 * LANES
ZR_W = -(-(ZR_LORA + LORA_PAD) // MM_TN) * MM_TN

NEG = -1e30


def _vmem_limit(block_bytes, scratch_bytes=0):
    need = 2 * block_bytes + scratch_bytes + VMEM_INTERNAL_RESERVE
    return int(min(max(need, 16 * 2**20), VMEM_BYTES_V7X - 4 * 2**20))


def _params(sem, block_bytes, scratch_bytes=0):
    return pltpu.CompilerParams(dimension_semantics=sem,
                                vmem_limit_bytes=_vmem_limit(block_bytes, scratch_bytes))


def _nbytes(shape, dtype):
    n = 1
    for s in shape:
        n *= s
    return n * jnp.dtype(dtype).itemsize


def _rmsnorm_body(x_ref, g_ref, o_ref):
    x = x_ref[...]
    ms = jnp.mean(x * x, axis=-1, keepdims=True)
    o_ref[...] = (x * lax.rsqrt(ms + RMS_EPS) * g_ref[...]).astype(o_ref.dtype)


def _rmsnorm(x, g, out_dtype):
    m, d = x.shape
    tm = min(m, 256)
    blk = _nbytes((tm, d), F32) + _nbytes((tm, d), out_dtype)
    return pl.pallas_call(
        _rmsnorm_body, grid=(m // tm,),
        in_specs=[pl.BlockSpec((tm, d), lambda i: (i, 0)), pl.BlockSpec((1, d), lambda i: (0, 0))],
        out_specs=pl.BlockSpec((tm, d), lambda i: (i, 0)),
        out_shape=jax.ShapeDtypeStruct((m, d), out_dtype),
        compiler_params=_params(("parallel",), blk), name="rmsnorm")(x, g.reshape(1, d))


def _row_scale(scale_ref, width):
    s = scale_ref[...]
    return jnp.concatenate([s] * (width // LANES), axis=1)


def _ffn_up_body(x_ref, wg_ref, wu_ref, *rest, norm_width):
    o_ref = rest[-1]
    x = x_ref[...]
    g = jnp.dot(x, wg_ref[...].astype(BF16), preferred_element_type=F32)
    u = jnp.dot(x, wu_ref[...].astype(BF16), preferred_element_type=F32)
    if norm_width:
        s = _row_scale(rest[0], g.shape[1])
        g = g * s
        u = u * s
    o_ref[...] = (g * jax.nn.sigmoid(g) * u).astype(o_ref.dtype)


def _ffn_up(x, wg, wu, l, ssq=None):
    m, k = x.shape
    n = wg.shape[2]
    tm = min(m, 1024)
    tn = MXU_COLS_V7X
    blk = _nbytes((tm, k), BF16) + 2 * _nbytes((k, tn), F32) + _nbytes((k, tn), BF16) + _nbytes((tm, tn), BF16)
    wspec = pl.BlockSpec((None, k, tn), lambda i, j: (l, 0, j))
    in_specs = [pl.BlockSpec((tm, k), lambda i, j: (i, 0)), wspec, wspec]
    args = [x, wg, wu]
    if ssq is not None:
        in_specs.append(pl.BlockSpec((tm, LANES), lambda i, j: (i, 0)))
        args.append(ssq)
    return pl.pallas_call(
        functools.partial(_ffn_up_body, norm_width=k if ssq is not None else 0), grid=(m // tm, n // tn),
        in_specs=in_specs, out_specs=pl.BlockSpec((tm, tn), lambda i, j: (i, j)),
        out_shape=jax.ShapeDtypeStruct((m, n), BF16),
        compiler_params=_params(("parallel", "arbitrary"), blk), name="ffn_up")(*args)


def _mm_body(a_ref, b_ref, o_ref):
    o_ref[...] = jnp.dot(a_ref[...], b_ref[...], preferred_element_type=F32)


def _mm_res_body(a_ref, b_ref, r_ref, *rest, scale, norm_width):
    acc = jnp.dot(a_ref[...], b_ref[...], preferred_element_type=F32)
    h = r_ref[...] + scale * acc
    if not norm_width:
        rest[0][...] = h
        return
    gain_ref, o_ref, hb_ref, ssq_ref = rest
    o_ref[...] = h
    hb_ref[...] = (h * gain_ref[...]).astype(BF16)
    hh = h * h
    part = hh[:, :LANES]
    for c in range(1, hh.shape[1] // LANES):
        part = part + hh[:, c * LANES:(c + 1) * LANES]

    @pl.when(pl.program_id(1) == 0)
    def _():
        ssq_ref[...] = part

    @pl.when(pl.program_id(1) > 0)
    def _():
        ssq_ref[...] += part

    @pl.when(pl.program_id(1) == pl.num_programs(1) - 1)
    def _():
        mean_sq = jnp.sum(ssq_ref[...], axis=-1, keepdims=True) * (1.0 / norm_width)
        ssq_ref[...] = jnp.broadcast_to(lax.rsqrt(mean_sq + RMS_EPS), ssq_ref.shape)


_TRANS_B = (((1,), (1,)), ((), ()))


def _mm_t_body(a_ref, bt_ref, *rest, norm_width):
    acc = lax.dot_general(a_ref[...], bt_ref[...], _TRANS_B, preferred_element_type=F32)
    if norm_width:
        acc = acc * _row_scale(rest[0], acc.shape[1])
    rest[-1][...] = acc


def _mm_cast_body(a_ref, b_ref, *rest, scale, has_res, trans_b):
    r_ref = rest[0] if has_res else None
    o_ref, bo_ref = rest[-2:]
    b = (b_ref[0] if len(b_ref.shape) == 3 else b_ref[...]).astype(BF16)
    bo_ref[...] = b
    if trans_b:
        acc = lax.dot_general(a_ref[...], b, _TRANS_B, preferred_element_type=F32)
    else:
        acc = jnp.dot(a_ref[...], b, preferred_element_type=F32)
    o_ref[...] = r_ref[...] + scale * acc if has_res else acc


def _matmul(a, b, l=None, res=None, scale=1.0, tm=1024, trans_b=False, ssq=None, next_gain=None):
    m, k = a.shape
    n = b.shape[-2] if trans_b else b.shape[-1]
    tm = min(m, tm)
    tn = MM_TN
    assert m % tm == 0 and n % tn == 0
    blk = _nbytes((tm, k), BF16) + _nbytes((k, tn), BF16) + _nbytes((tm, tn), F32)
    bshape, bidx = ((tn, k), lambda i, j: (j, 0)) if trans_b else ((k, tn), lambda i, j: (0, j))
    bspec = (pl.BlockSpec(bshape, bidx) if b.ndim == 2
             else pl.BlockSpec((None,) + bshape, lambda i, j: (l,) + bidx(i, j)))
    in_specs = [pl.BlockSpec((tm, k), lambda i, j: (i, 0)), bspec]
    args = [a, b]
    tile = pl.BlockSpec((tm, tn), lambda i, j: (i, j))
    out_specs, out_shape = tile, jax.ShapeDtypeStruct((m, n), F32)
    if trans_b:
        assert res is None and next_gain is None
        body = functools.partial(_mm_t_body, norm_width=k if ssq is not None else 0)
        if ssq is not None:
            in_specs.append(pl.BlockSpec((tm, LANES), lambda i, j: (i, 0)))
            args.append(ssq)
    elif res is None:
        assert ssq is None and next_gain is None
        body = _mm_body
    else:
        assert ssq is None
        body = functools.partial(_mm_res_body, scale=scale, norm_width=n if next_gain is not None else 0)
        in_specs.append(tile)
        args.append(res)
        blk += _nbytes((tm, tn), F32)
        if next_gain is not None:
            in_specs.append(pl.BlockSpec((1, tn), lambda i, j: (0, j)))
            args.append(next_gain.reshape(1, n))
            out_specs = [tile, tile, pl.BlockSpec((tm, LANES), lambda i, j: (i, 0))]
            out_shape = [out_shape, jax.ShapeDtypeStruct((m, n), BF16), jax.ShapeDtypeStruct((m, LANES), F32)]
            blk += _nbytes((tm, tn), BF16) + _nbytes((tm, LANES), F32)
    return pl.pallas_call(
        body, grid=(m // tm, n // tn), in_specs=in_specs, out_specs=out_specs, out_shape=out_shape,
        compiler_params=_params(("parallel", "arbitrary"), blk), name="matmul")(*args)


CAST_TILE_BYTES = 6 * 2**20


def _matmul_cast(a, b, l, n, res=None, scale=1.0, trans_b=False, src_rows=None):
    m, k = a.shape
    tn = MM_TN
    while _nbytes((k, tn), F32) > CAST_TILE_BYTES:
        tn //= 2
    assert n % tn == 0 and tn % LANES == 0
    blk = _nbytes((m, k), BF16) + _nbytes((k, tn), F32) + _nbytes((k, tn), BF16) + 2 * _nbytes((m, tn), F32)
    if trans_b:
        if src_rows is None:
            wspec = pl.BlockSpec((None, tn, k), lambda j: (l, j, 0))
        else:
            wspec = pl.BlockSpec((pl.Element(1), pl.Element(tn), pl.Element(k)), lambda j: (l, src_rows(j, tn), 0))
        cspec, cshape = pl.BlockSpec((tn, k), lambda j: (j, 0)), (n, k)
    else:
        wspec = pl.BlockSpec((None, k, tn), lambda j: (l, 0, j))
        cspec, cshape = pl.BlockSpec((k, tn), lambda j: (0, j)), (k, n)
    in_specs = [pl.BlockSpec((m, k), lambda j: (0, 0)), wspec]
    args = [a, b]
    if res is not None:
        in_specs.append(pl.BlockSpec((m, tn), lambda j: (0, j)))
        args.append(res)
    return pl.pallas_call(
        functools.partial(_mm_cast_body, scale=scale, has_res=res is not None, trans_b=trans_b), grid=(n // tn,),
        in_specs=in_specs, out_specs=[pl.BlockSpec((m, tn), lambda j: (0, j)), cspec],
        out_shape=[jax.ShapeDtypeStruct((m, n), F32), jax.ShapeDtypeStruct(cshape, BF16)],
        compiler_params=_params(("parallel",), blk), name="matmul_cast")(*args)


def _rope(x, cos, sin_signed):
    return x * cos + pltpu.roll(x, HEAD_DIM_A // 2, 1) * sin_signed


def _branch_multiplicity(delta):
    c = jnp.zeros(delta.shape, F32)
    for window, dilation in DILATED_BRANCHES:
        assert dilation & (dilation - 1) == 0
        hit = jnp.where(delta <= window, 1.0, 0.0)
        if dilation > 1:
            hit = jnp.where((delta & (dilation - 1)) == 0, hit, 0.0)
        c = c + hit
    return jnp.where(delta >= 0, c, 0.0)


def _attn_prompt_body(q_ref, k_ref, v_ref, cos_ref, sin_ref, o_ref, kout_ref, qs, ks, vs, ctab, *, t_len, tq):
    nq = t_len // tq

    @pl.when((pl.program_id(0) == 0) & (pl.program_id(1) == 0))
    def _():
        rel = (lax.broadcasted_iota(jnp.int32, (tq, tq), 0) - lax.broadcasted_iota(jnp.int32, (tq, tq), 1))
        for d in range(nq):
            ctab[:, d * tq:(d + 1) * tq] = _branch_multiplicity(rel + (nq - 1 - d) * tq)

    cos = cos_ref[...]
    sin = sin_ref[...]
    k = _rope(k_ref[0], cos, sin)
    kout_ref[0] = k
    ks[...] = k.astype(BF16)
    qs[...] = (_rope(q_ref[0], cos, sin) * (HEAD_DIM_A ** -0.5)).astype(BF16)
    vs[...] = v_ref[0].astype(BF16)
    for i in range(nq):
        kw = (i + 1) * tq
        s = lax.dot_general(qs[i * tq:(i + 1) * tq, :], ks[0:kw, :], _TRANS_B, preferred_element_type=F32)
        c = ctab[:, (nq - 1 - i) * tq:]
        sm = jnp.where(c > 0.0, s, NEG)
        p = jnp.exp(sm - jnp.max(sm, axis=-1, keepdims=True)) * c
        l = jnp.sum(p, axis=-1, keepdims=True)
        acc = jnp.dot(p.astype(BF16), vs[0:kw, :], preferred_element_type=F32)
        o_ref[0, i * tq:(i + 1) * tq, :] = acc / l


def _attn_sample_body(q_ref, k_ref, v_ref, kp_hbm, vp_hbm, cos_ref, sin_ref, o_ref, kout_ref, kp_ref, vp_ref, sems,
                      *, t_len, n_prev, layer):
    bi = pl.program_id(0)

    def head_copy(src, dst, which, h):
        return pltpu.make_async_copy(src.at[layer, bi, :, h, :], dst.at[h], sems.at[which, h])

    for h in range(N_HEADS_A):
        head_copy(kp_hbm, kp_ref, 0, h).start()
        head_copy(vp_hbm, vp_ref, 1, h).start()
    cos = cos_ref[...]
    sin = sin_ref[...]
    d1 = (n_prev + lax.broadcasted_iota(jnp.int32, (t_len, n_prev), 0)
          - lax.broadcasted_iota(jnp.int32, (t_len, n_prev), 1))
    c1 = _branch_multiplicity(d1)
    d2 = (lax.broadcasted_iota(jnp.int32, (t_len, LANES), 0) - lax.broadcasted_iota(jnp.int32, (t_len, LANES), 1))
    c2 = _branch_multiplicity(d2)
    pad = jnp.zeros((LANES - t_len, HEAD_DIM_A), F32)
    for h in range(N_HEADS_A):
        sl = slice(h * HEAD_DIM_A, (h + 1) * HEAD_DIM_A)
        k = _rope(k_ref[0, :, sl], cos, sin)
        kout_ref[0, :, sl] = k
        q = (_rope(q_ref[0, :, sl], cos, sin) * (HEAD_DIM_A ** -0.5)).astype(BF16)
        kn = jnp.concatenate([k, pad], axis=0).astype(BF16)
        vn = jnp.concatenate([v_ref[0, :, sl], pad], axis=0).astype(BF16)
        head_copy(kp_hbm, kp_ref, 0, h).wait()
        head_copy(vp_hbm, vp_ref, 1, h).wait()
        s1 = lax.dot_general(q, kp_ref[h].astype(BF16), _TRANS_B, preferred_element_type=F32)
        s2 = lax.dot_general(q, kn, _TRANS_B, preferred_element_type=F32)
        sm1 = jnp.where(c1 > 0.0, s1, NEG)
        sm2 = jnp.where(c2 > 0.0, s2, NEG)
        m = jnp.maximum(jnp.max(sm1, axis=-1, keepdims=True), jnp.max(sm2, axis=-1, keepdims=True))
        p1 = jnp.exp(sm1 - m) * c1
        p2 = jnp.exp(sm2 - m) * c2
        l = jnp.sum(p1, axis=-1, keepdims=True) + jnp.sum(p2, axis=-1, keepdims=True)
        acc = (jnp.dot(p1.astype(BF16), vp_ref[h].astype(BF16), preferred_element_type=F32)
               + jnp.dot(p2.astype(BF16), vn, preferred_element_type=F32))
        o_ref[0, :, sl] = acc / l


def _rope_tables(pos0, t_len):
    half = HEAD_DIM_A // 2
    inv = ROPE_THETA ** (-jnp.arange(half, dtype=F32) / half)
    ang = (pos0 + jnp.arange(t_len)).astype(F32)[:, None] * inv[None, :]
    cos = jnp.cos(ang)
    sin = jnp.sin(ang)
    return jnp.concatenate([cos, cos], axis=-1), jnp.concatenate([-sin, sin], axis=-1)


def _attention(z3, pos0, kv_prefix, l):
    b, t, _ = z3.shape
    cos, sin = _rope_tables(pos0, t)
    hd = HEAD_DIM_A
    out_shape = [jax.ShapeDtypeStruct((b, t, A_W), F32), jax.ShapeDtypeStruct((b, t, A_W), F32)]
    if kv_prefix is None:
        col = lambda base: (lambda bi, hi: (bi, 0, base // hd + hi))
        tab = pl.BlockSpec((t, hd), lambda bi, hi: (0, 0))
        zspecs = [pl.BlockSpec((1, t, hd), col(ZM_QA)), pl.BlockSpec((1, t, hd), col(ZM_KA)),
                  pl.BlockSpec((1, t, hd), col(ZM_VA))]
        out_specs = [pl.BlockSpec((1, t, hd), col(0)), pl.BlockSpec((1, t, hd), col(0))]
        blk = 7 * _nbytes((t, hd), F32)
        tq = min(t, 256)
        body = functools.partial(_attn_prompt_body, t_len=t, tq=tq)
        scratch = 3 * _nbytes((t, hd), BF16) + 5 * _nbytes((tq, t), F32)
        return pl.pallas_call(
            body, grid=(b, N_HEADS_A), in_specs=zspecs + [tab, tab], out_specs=out_specs, out_shape=out_shape,
            scratch_shapes=[pltpu.VMEM((t, hd), BF16)] * 3 + [pltpu.VMEM((tq, t), F32)],
            compiler_params=_params(("arbitrary", "arbitrary"), blk, scratch),
            name="attn_prompt")(z3, z3, z3, cos, sin)
    k_prev, v_prev = kv_prefix
    n_prev = k_prev.shape[2]
    assert t <= LANES and n_prev % LANES == 0 and k_prev.shape[3:] == (N_HEADS_A, hd)
    col = lambda base: (lambda bi: (bi, 0, base // A_W))
    tab = pl.BlockSpec((t, hd), lambda bi: (0, 0))
    zspecs = [pl.BlockSpec((1, t, A_W), col(ZM_QA)), pl.BlockSpec((1, t, A_W), col(ZM_KA)),
              pl.BlockSpec((1, t, A_W), col(ZM_VA))]
    out_specs = [pl.BlockSpec((1, t, A_W), col(0)), pl.BlockSpec((1, t, A_W), col(0))]
    pspec = pl.BlockSpec(memory_space=pl.ANY)
    blk = 5 * _nbytes((t, A_W), F32)
    head_buf = pltpu.VMEM((N_HEADS_A, n_prev, hd), F32)
    body = functools.partial(_attn_sample_body, t_len=t, n_prev=n_prev, layer=l)
    return pl.pallas_call(
        body, grid=(b,), in_specs=zspecs + [pspec, pspec, tab, tab], out_specs=out_specs,
        out_shape=out_shape, scratch_shapes=[head_buf, head_buf, pltpu.SemaphoreType.DMA((2, N_HEADS_A))],
        compiler_params=_params(("arbitrary",), blk, 2 * _nbytes((N_HEADS_A, n_prev, hd), F32)),
        name="attn_sample")(z3, z3, z3, k_prev, v_prev, cos, sin)


def _group_sum(x, ones_blockdiag):
    outs = []
    for j in range(x.shape[-1] // LANES):
        outs.append(jnp.dot(x[:, j * LANES:(j + 1) * LANES], ones_blockdiag, preferred_element_type=F32,
                            precision=lax.Precision.HIGHEST))
    return jnp.concatenate(outs, axis=-1)


def _softplus(y):
    return jnp.maximum(y, 0.0) + jnp.log1p(jnp.exp(-jnp.abs(y)))


def _rwkv_prep_body(x_ref, lo_ref, sx_ref, slo_ref, mux_ref, mulo_ref, w0_ref, a0_ref, kkp_ref, kap_ref, rk_ref,
                    wup_ref, aup_ref, gup_ref, ones_ref,
                    r_o, w_o, k_o, v_o, kk_o, b_o, g_o, bonus_o, last_x, last_lo, *, tt, channel_major):
    @pl.when(pl.program_id(1) == 0)
    def _():
        last_x[0:1, :] = sx_ref[0]
        last_lo[0:1, :] = slo_ref[0]

    x = x_ref[0]
    lo = lo_ref[0]
    first = lax.broadcasted_iota(jnp.int32, (tt, 1), 0) == 0
    px = jnp.where(first, last_x[0:1, :], pltpu.roll(x, 1, 0))
    plo = jnp.where(first, last_lo[0:1, :], pltpu.roll(lo, 1, 0))
    last_x[0:1, :] = x[tt - 1:tt, :]
    last_lo[0:1, :] = lo[tt - 1:tt, :]
    fx = x + mux_ref[...] * (px - x)
    flo = lo + mulo_ref[...] * (plo - lo)
    r = fx[:, :B_W]
    k = fx[:, B_W:2 * B_W]
    v = fx[:, 2 * B_W:]
    zwa = flo[:, :LANES]
    zg = flo[:, LANES:]
    ones_bd = ones_ref[...]
    wl = w0_ref[...] + jnp.dot(jnp.tanh(zwa).astype(BF16), wup_ref[...], preferred_element_type=F32)
    w_log = -_softplus(-wl) - 0.5
    decay = jnp.exp(-jnp.exp(w_log))
    a = jax.nn.sigmoid(a0_ref[...] + jnp.dot(zwa.astype(BF16), aup_ref[...], preferred_element_type=F32))
    g = jnp.dot(jax.nn.sigmoid(zg).astype(BF16), gup_ref[...], preferred_element_type=F32)
    kk = k * kkp_ref[...]
    kk = kk / jnp.maximum(jnp.sqrt(_group_sum(kk * kk, ones_bd)), 1e-12)
    kmod = k * (1.0 + (a - 1.0) * kap_ref[...])
    lay = (lambda y: y.T) if channel_major else (lambda y: y)
    r_o[0] = lay(r)
    w_o[0] = lay(decay)
    k_o[0] = lay(kmod)
    v_o[0] = lay(v)
    kk_o[0] = lay(kk)
    b_o[0] = lay(kk * a)
    g_o[0] = g
    bonus_o[0] = _group_sum(r * kmod * rk_ref[...], ones_bd) * v


def _rwkv_prep(zm3, zr3, shift_prev, p, channel_major):
    b, t, _ = zm3.shape
    tt = min(t, 256)
    sx = shift_prev[:, None, :3 * B_W]
    slo = jnp.pad(shift_prev[:, None, 3 * B_W:], ((0, 0), (0, 0), (0, LORA_PAD - LORA_ALL)))
    row = lambda w: pl.BlockSpec((1, w), lambda bi, ti: (0, 0))
    full = lambda a: pl.BlockSpec(a.shape, lambda bi, ti: (0,) * a.ndim)
    in_specs = [pl.BlockSpec((1, tt, 3 * B_W), lambda bi, ti: (bi, ti, ZM_RKV // (3 * B_W))),
                pl.BlockSpec((1, tt, LORA_PAD), lambda bi, ti: (bi, ti, ZR_LORA // LORA_PAD)),
                pl.BlockSpec((1, 1, 3 * B_W), lambda bi, ti: (bi, 0, 0)),
                pl.BlockSpec((1, 1, LORA_PAD), lambda bi, ti: (bi, 0, 0)),
                row(3 * B_W), row(LORA_PAD), row(B_W), row(B_W), row(B_W), row(B_W), row(B_W),
                full(p['w_up']), full(p['a_up']), full(p['g_up']), full(p['ones_bd'])]
    ospec = pl.BlockSpec((1, tt, B_W), lambda bi, ti: (bi, ti, 0))
    oshape = jax.ShapeDtypeStruct((b, t, B_W), F32)
    if channel_major:
        rspec, rshape = pl.BlockSpec((1, B_W, tt), lambda bi, ti: (bi, 0, ti)), jax.ShapeDtypeStruct((b, B_W, t), F32)
    else:
        rspec, rshape = ospec, oshape
    blk = _nbytes((tt, 3 * B_W + LORA_PAD), F32) + 8 * _nbytes((tt, B_W), F32) + 2 * _nbytes((512, B_W), F32)
    return pl.pallas_call(
        functools.partial(_rwkv_prep_body, tt=tt, channel_major=channel_major), grid=(b, t // tt),
        in_specs=in_specs, out_specs=[rspec] * 6 + [ospec] * 2, out_shape=[rshape] * 6 + [oshape] * 2,
        scratch_shapes=[pltpu.VMEM((SUBLANES, 3 * B_W), F32), pltpu.VMEM((SUBLANES, LORA_PAD), F32)],
        compiler_params=_params(("parallel", "arbitrary"), blk), name="rwkv_prep")(
            zm3, zr3, sx, slo, p['mu_x'], p['mu_lo'], p['w0'], p['a0'], p['k_k'], p['k_a'], p['r_k'],
            p['w_up'], p['a_up'], p['g_up'], p['ones_bd'])


N_PARTIAL = 4


def _rwkv_steps(r_ref, w_ref, k_ref, kk_ref, b_ref, v_ref, o_ref, state, *, tc, vp, time_major):
    def step(t, row):
        parts = [jnp.zeros((vp, LANES), F32)] * N_PARTIAL
        for k in range(HEAD_DIM_B):
            parts[k % N_PARTIAL] = parts[k % N_PARTIAL] + state[k] * row(kk_ref, k)
        sa = -((parts[0] + parts[1]) + (parts[2] + parts[3]))
        vt = v_ref[t]
        parts = [jnp.zeros((vp, LANES), F32)] * N_PARTIAL
        for k in range(HEAD_DIM_B):
            s = state[k] * row(w_ref, k) + sa * row(b_ref, k) + vt * row(k_ref, k)
            state[k] = s
            parts[k % N_PARTIAL] = parts[k % N_PARTIAL] + s * row(r_ref, k)
        o_ref[t] = (parts[0] + parts[1]) + (parts[2] + parts[3])

    if time_major:
        def one(t, carry):
            step(t, lambda ref, k: ref[t, pl.ds(k, 1), :])
            return carry

        lax.fori_loop(0, tc, one, 0)
    else:
        def eight(tb, carry):
            for s in range(SUBLANES):
                step(tb * SUBLANES + s, lambda ref, k, s=s: ref[k, tb, pl.ds(s, 1), :])
            return carry

        lax.fori_loop(0, tc // SUBLANES, eight, 0)


def _rwkv_scan_body(r_ref, w_ref, k_ref, kk_ref, b_ref, v_ref, s0_ref, o_ref, sout_ref, state, *, tc, vp):
    @pl.when(pl.program_id(0) == 0)
    def _():
        state[...] = s0_ref[...]

    _rwkv_steps(r_ref, w_ref, k_ref, kk_ref, b_ref, v_ref, o_ref, state, tc=tc, vp=vp, time_major=True)

    @pl.when(pl.program_id(0) == pl.num_programs(0) - 1)
    def _():
        sout_ref[...] = state[...]


def _rwkv_scan_rows_body(r_ref, w_ref, k_ref, kk_ref, b_ref, v_ref, s0_ref, o_ref, sout_ref,
                         state, yt, rk, wk, kkey, kkk, bk, vk, ok, *, tc, nb, dup):
    vp = HEAD_DIM_B // dup
    nh = N_HEADS_B

    @pl.when(pl.program_id(0) == 0)
    def _():
        state[...] = s0_ref[...]

    def to_lanes(x_ref, store, n_rows, row_of):
        for j in range(n_rows):
            pieces = [x_ref[b, pl.ds(row_of(j, vh), nh, stride=HEAD_DIM_B), :] for b in range(nb) for vh in range(dup)]
            store(j, jnp.concatenate(pieces, axis=0).T)

    def key_store(dst):
        def store(j, x):
            dst[j] = x.reshape(tc // SUBLANES, SUBLANES, LANES)
        return store

    def val_store(j, x):
        vk[:, j, :] = x

    for x_ref, dst in ((r_ref, rk), (w_ref, wk), (k_ref, kkey), (kk_ref, kkk), (b_ref, bk)):
        to_lanes(x_ref, key_store(dst), HEAD_DIM_B, lambda j, vh: j)
    to_lanes(v_ref, val_store, vp, lambda j, vh: vh * vp + j)

    _rwkv_steps(rk, wk, kkey, kkk, bk, vk, ok, state, tc=tc, vp=vp, time_major=False)

    for j in range(vp):
        m = ok[:, j, :].T
        for b in range(nb):
            for vh in range(dup):
                lane0 = (b * dup + vh) * nh
                yt[b, pl.ds(vh * vp + j, nh, stride=HEAD_DIM_B), :] = m[lane0:lane0 + nh, :]
    for b in range(nb):
        o_ref[b] = yt[b].T

    @pl.when(pl.program_id(0) == pl.num_programs(0) - 1)
    def _():
        sout_ref[...] = state[...]


def _rwkv_scan_rows(r, w, k, kk, bv, v, s0, dup):
    nb, _, t = r.shape
    vp = HEAD_DIM_B // dup
    tc = LANES
    assert t % tc == 0 and nb * dup * N_HEADS_B == LANES
    cspec = pl.BlockSpec((nb, B_W, tc), lambda i: (0, 0, i))
    xspec = pl.BlockSpec((nb, tc, B_W), lambda i: (0, i, 0))
    sspec = pl.BlockSpec((HEAD_DIM_B, vp, LANES), lambda i: (0, 0, 0))
    key_tile = pltpu.VMEM((HEAD_DIM_B, tc // SUBLANES, SUBLANES, LANES), F32)
    val_tile = pltpu.VMEM((tc, vp, LANES), F32)
    blk = 7 * _nbytes((nb, tc, B_W), F32) + 2 * _nbytes((HEAD_DIM_B, vp, LANES), F32)
    scr = (_nbytes((HEAD_DIM_B, vp, LANES), F32) + _nbytes((nb, B_W, tc), F32)
           + 5 * _nbytes((tc, HEAD_DIM_B, LANES), F32) + 2 * _nbytes((tc, vp, LANES), F32))
    return pl.pallas_call(
        functools.partial(_rwkv_scan_rows_body, tc=tc, nb=nb, dup=dup), grid=(t // tc,),
        in_specs=[cspec] * 6 + [sspec], out_specs=[xspec, sspec],
        out_shape=[jax.ShapeDtypeStruct((nb, t, B_W), F32), jax.ShapeDtypeStruct((HEAD_DIM_B, vp, LANES), F32)],
        scratch_shapes=[pltpu.VMEM((HEAD_DIM_B, vp, LANES), F32), pltpu.VMEM((nb, B_W, tc), F32)]
        + [key_tile] * 5 + [val_tile] * 2,
        compiler_params=_params(("arbitrary",), blk, scr), name="rwkv_scan_rows")(r, w, k, kk, bv, v, s0)


def _rwkv_scan(r, w, k, kk, bv, v, s0):
    t, vp, _ = v.shape
    tc = min(t, 64)
    kspec = pl.BlockSpec((tc, HEAD_DIM_B, LANES), lambda i: (i, 0, 0))
    vspec = pl.BlockSpec((tc, vp, LANES), lambda i: (i, 0, 0))
    sspec = pl.BlockSpec((HEAD_DIM_B, vp, LANES), lambda i: (0, 0, 0))
    blk = 5 * _nbytes((tc, HEAD_DIM_B, LANES), F32) + 2 * _nbytes((tc, vp, LANES), F32) \
        + 2 * _nbytes((HEAD_DIM_B, vp, LANES), F32)
    return pl.pallas_call(
        functools.partial(_rwkv_scan_body, tc=tc, vp=vp), grid=(t // tc,),
        in_specs=[kspec] * 5 + [vspec, sspec], out_specs=[vspec, sspec],
        out_shape=[jax.ShapeDtypeStruct((t, vp, LANES), F32), jax.ShapeDtypeStruct((HEAD_DIM_B, vp, LANES), F32)],
        scratch_shapes=[pltpu.VMEM((HEAD_DIM_B, vp, LANES), F32)],
        compiler_params=_params(("arbitrary",), blk, _nbytes((HEAD_DIM_B, vp, LANES), F32)),
        name="rwkv_scan")(r, w, k, kk, bv, v, s0)


def _rwkv_mix(zm3, zr3, shift_prev, wkv0, p):
    b, t, _ = zm3.shape
    nh, hd = N_HEADS_B, HEAD_DIM_B
    in_vmem_relayout = t % LANES == 0
    r, w, kmod, v, kk, bv, g, bonus = _rwkv_prep(zm3, zr3, shift_prev, p, channel_major=in_vmem_relayout)
    dup = LANES // (b * nh)
    assert dup * b * nh == LANES and hd % dup == 0
    vp = hd // dup

    def key_layout(x):
        y = x.reshape(b, t, nh, hd).transpose(1, 3, 0, 2)[:, :, :, None]
        return jnp.broadcast_to(y, (t, hd, b, dup, nh)).reshape(t, hd, LANES)

    def val_layout(x):
        return x.reshape(b, t, nh, dup, vp).transpose(1, 4, 0, 3, 2).reshape(t, vp, LANES)

    s0 = wkv0.reshape(b, nh, dup, vp, hd).transpose(4, 3, 0, 2, 1).reshape(hd, vp, LANES)
    if in_vmem_relayout:
        o, s = _rwkv_scan_rows(r, w, kmod, kk, bv, v, s0, dup)
    else:
        o, s = _rwkv_scan(key_layout(r), key_layout(w), key_layout(kmod), key_layout(kk), key_layout(bv),
                          val_layout(v), s0)
        o = o.reshape(t, vp, b, dup, nh).transpose(2, 0, 4, 3, 1).reshape(b, t, B_W)
    s = s.reshape(hd, vp, b, dup, nh).transpose(2, 4, 3, 1, 0).reshape(b, nh, hd, hd)
    return o, bonus, g, s


def _gmlp_body(u_ref, v_ref, ws_ref, b_ref, o_ref, *, tc):
    keep = (lax.broadcasted_iota(jnp.int32, (CHUNK, CHUNK), 1) <= lax.broadcasted_iota(jnp.int32, (CHUNK, CHUNK), 0))
    for g in range(N_GROUPS_C):
        sl = slice(g * CHUNK, (g + 1) * CHUNK)
        w = jnp.where(keep, ws_ref[g], 0.0).astype(BF16)
        v = v_ref[0, :, sl]
        if tc < CHUNK:
            v = jnp.concatenate([v, jnp.zeros((CHUNK - tc, CHUNK), F32)], axis=0)
        s = jnp.dot(w, v.astype(BF16), preferred_element_type=F32) + b_ref[g]
        o_ref[0, :, sl] = u_ref[0, :, sl] * s[:tc]


def _gmlp(z3, w_s, b_s):
    b, t, _ = z3.shape
    tc = min(t, CHUNK)
    assert t % tc == 0
    blk = 3 * _nbytes((tc, C_W), F32) + 2 * _nbytes((N_GROUPS_C, CHUNK, CHUNK), F32)
    return pl.pallas_call(
        functools.partial(_gmlp_body, tc=tc), grid=(b, t // tc),
        in_specs=[pl.BlockSpec((1, tc, C_W), lambda bi, ci: (bi, ci, ZR_CU // C_W)),
                  pl.BlockSpec((1, tc, C_W), lambda bi, ci: (bi, ci, ZR_CV // C_W)),
                  pl.BlockSpec((N_GROUPS_C, CHUNK, CHUNK), lambda bi, ci: (0, 0, 0)),
                  pl.BlockSpec((N_GROUPS_C, CHUNK, 1), lambda bi, ci: (0, 0, 0))],
        out_specs=pl.BlockSpec((1, tc, C_W), lambda bi, ci: (bi, ci, 0)),
        out_shape=jax.ShapeDtypeStruct((b, t, C_W), F32),
        compiler_params=_params(("parallel", "parallel"), blk), name="gmlp")(z3, z3, w_s, b_s[:, :, None])


POOL_HALO = 16


def _pool_body(p_ref, pre_ref, w_ref, sc_ref, o_ref, ext, *, t_len, pos0, tc):
    ext[0:POOL_HALO, :] = pre_ref[0]
    ext[POOL_HALO:POOL_HALO + t_len, :] = p_ref[0]
    for c0 in range(0, t_len, tc):
        pos = pos0 + c0 + lax.broadcasted_iota(jnp.int32, (tc, 1), 0)
        for g, win in enumerate(POOL_WINDOWS):
            sl = slice(g * POOL_GROUP, (g + 1) * POOL_GROUP)
            base = POOL_HALO + c0
            acc = ext[base:base + tc, sl]
            for i in range(1, win):
                acc = acc + ext[base - i:base - i + tc, sl]
            cnt = jnp.minimum(win, pos + 1).astype(F32)
            pooled = acc / cnt - p_ref[0, c0:c0 + tc, sl]
            y = jnp.dot(pooled.astype(BF16), w_ref[g], preferred_element_type=F32)
            o_ref[0, c0:c0 + tc, sl] = y * sc_ref[:, sl]


def _pool(z3, prefix, pos0, w_pool, scale):
    b, t, _ = z3.shape
    assert POOL_PREV < POOL_HALO
    pre = jnp.pad(prefix, ((0, 0), (POOL_HALO - POOL_PREV, 0), (0, 0)))
    tc = min(t, 256)
    blk = 2 * _nbytes((t, D_W), F32) + _nbytes((POOL_HALO, D_W), F32) + _nbytes(w_pool.shape, BF16)
    scr = _nbytes((t + POOL_HALO, D_W), F32)
    return pl.pallas_call(
        functools.partial(_pool_body, t_len=t, pos0=pos0, tc=tc), grid=(b,),
        in_specs=[pl.BlockSpec((1, t, D_W), lambda bi: (bi, 0, ZR_D // D_W)),
                  pl.BlockSpec((1, POOL_HALO, D_W), lambda bi: (bi, 0, 0)),
                  pl.BlockSpec(w_pool.shape, lambda bi: (0, 0, 0)),
                  pl.BlockSpec((1, D_W), lambda bi: (0, 0))],
        out_specs=pl.BlockSpec((1, t, D_W), lambda bi: (bi, 0, 0)),
        out_shape=jax.ShapeDtypeStruct((b, t, D_W), F32),
        scratch_shapes=[pltpu.VMEM((t + POOL_HALO, D_W), F32)],
        compiler_params=_params(("parallel",), blk, scr), name="pool")(z3, pre, w_pool, scale.reshape(1, D_W))


def _mix_body(oa_ref, ob_ref, bonus_ref, g_ref, oc_ref, od_ref, ga_ref, lnw_ref, lnb_ref, gc_ref, gd_ref, ones_ref,
              o_ref):
    def rms(x, gain):
        ms = jnp.mean(x * x, axis=-1, keepdims=True)
        return x * lax.rsqrt(ms + RMS_EPS) * gain

    ones_bd = ones_ref[...]
    o = ob_ref[...]
    mean = _group_sum(o, ones_bd) * (1.0 / HEAD_DIM_B)
    d = o - mean
    var = _group_sum(d * d, ones_bd) * (1.0 / HEAD_DIM_B)
    ob = d * lax.rsqrt(var + GN_EPS) * lnw_ref[...] + lnb_ref[...]
    ob = (ob + bonus_ref[...]) * g_ref[...]
    o_ref[:, 0:A_W] = rms(oa_ref[...], ga_ref[...]).astype(BF16)
    o_ref[:, A_W:A_W + B_W] = ob.astype(BF16)
    o_ref[:, A_W + B_W:A_W + B_W + C_W] = rms(oc_ref[...], gc_ref[...]).astype(BF16)
    o_ref[:, A_W + B_W + C_W:] = rms(od_ref[...], gd_ref[...]).astype(BF16)


def _mix(oa, ob, bonus, g, oc, od, ga, lnw, lnb, gc, gd, ones_bd):
    m = oa.shape[0]
    tm = min(m, 256)
    act = pl.BlockSpec((tm, A_W), lambda i: (i, 0))
    row = pl.BlockSpec((1, A_W), lambda i: (0, 0))
    blk = 6 * _nbytes((tm, A_W), F32) + _nbytes((tm, D_MODEL), BF16) + _nbytes((512, A_W), F32)
    r1 = lambda a: a.reshape(1, -1)
    return pl.pallas_call(
        _mix_body, grid=(m // tm,),
        in_specs=[act] * 6 + [row] * 5 + [pl.BlockSpec((LANES, LANES), lambda i: (0, 0))],
        out_specs=pl.BlockSpec((tm, D_MODEL), lambda i: (i, 0)),
        out_shape=jax.ShapeDtypeStruct((m, D_MODEL), BF16),
        compiler_params=_params(("parallel",), blk), name="mix")(
            oa, ob, bonus, g, oc, od, r1(ga), r1(lnw), r1(lnb), r1(gc), r1(gd), ones_bd)


def _zr_src_rows(j, tn):
    assert ZM_W == OFF_B + 3 * B_W and (ZR_LORA - ZR_CU) % tn == 0 and OFF_C % 32 == 0 and ZM_W % 32 == 0
    n_cd = (ZR_LORA - ZR_CU) // tn
    return pl.multiple_of(jnp.where(j < n_cd, OFF_C + j * tn, ZM_W + (j - n_cd) * tn), 32)


def _prep_layer_weights(l, w):
    mu = w['mu_b'][l]
    pad_rows = lambda a, before, total: jnp.pad(a, ((before, total - before - a.shape[0]), (0, 0))).astype(BF16)
    ones_bd = jnp.kron(jnp.eye(LANES // HEAD_DIM_B, dtype=F32), jnp.ones((HEAD_DIM_B, HEAD_DIM_B), F32))
    rw = dict(
        mu_x=mu[None, :3 * B_W], mu_lo=jnp.pad(mu[None, 3 * B_W:], ((0, 0), (0, LORA_PAD - LORA_ALL))),
        w0=w['w0'][l][None], a0=w['a0'][l][None], k_k=w['k_k'][l][None], k_a=w['k_a'][l][None],
        r_k=w['r_k'][l].reshape(1, B_W),
        w_up=pad_rows(w['w_up'][l], 0, LANES), a_up=pad_rows(w['a_up'][l], LORA_W, LANES),
        g_up=pad_rows(w['g_up'][l], 0, LORA_PAD - LANES), ones_bd=ones_bd)
    return dict(w_pool=w['w_pool'][l].astype(BF16), rwkv=rw)


def _layer(x, l, w, wl, wb, pos0, kv_prefix, shift_prev, wkv0, pool_prefix, x_norm=None, next_gain=None):
    cast = wb is None
    wb = {} if cast else wb
    b, t, _ = x.shape
    m = b * t

    def proj(an, name, n, copy, src_rows=None):
        if not cast:
            return _matmul(an[0], wb[copy], trans_b=True, ssq=an[1])
        out, wb[copy] = _matmul_cast(an[0], w[name], l, n, trans_b=True, src_rows=src_rows)
        return out

    def res_mm(a, name, tm, res, scale, gain):
        if cast:
            out, wb[name] = _matmul_cast(a, w[name], l, D_MODEL, res=res, scale=scale)
            return out, (None if gain is None else (_rmsnorm(out, gain, BF16), None))
        if gain is None:
            return _matmul(a, wb[name], res=res, scale=scale, tm=tm), None
        out, ob, ssq = _matmul(a, wb[name], res=res, scale=scale, tm=tm, next_gain=gain)
        return out, (ob, ssq)

    x2 = x.reshape(m, D_MODEL)
    if x_norm is None:
        x_norm = (_rmsnorm(x2, w['ln_ffn1'][l], BF16), None)
    hid = _ffn_up(x_norm[0], w['w1_gate'], w['w1_up'], l, ssq=x_norm[1])
    h, hn = res_mm(hid, 'w1_down', 512, x2, 0.5, w['ln_mix'][l])
    zm3 = proj(hn, 'w_in_t', ZM_W, 'w_in_t').reshape(b, t, ZM_W)
    zr3 = proj(hn, 'w_in_t', ZR_W, 'w_rest_t', _zr_src_rows).reshape(b, t, ZR_W)
    out_a, ka = _attention(zm3, pos0, kv_prefix, l)
    o_b, bonus, gate, wkv_new = _rwkv_mix(zm3, zr3, shift_prev, wkv0, wl['rwkv'])
    out_c = _gmlp(zr3, w['w_s'][l], w['b_s'][l])
    out_d = _pool(zr3, pool_prefix, pos0, wl['w_pool'], w['pool_scale'][l])
    r2 = lambda a: a.reshape(m, -1)
    mix = _mix(r2(out_a), r2(o_b), r2(bonus), r2(gate), r2(out_c), r2(out_d), w['g_out_a'][l], w['ln_x_w'][l],
               w['ln_x_b'][l], w['g_out_c'][l], w['g_out_d'][l], wl['rwkv']['ones_bd'])
    h, hn = res_mm(mix, 'w_out', 1024, h, 1.0, w['ln_ffn2'][l])
    hid = _ffn_up(hn[0], w['w2_gate'], w['w2_up'], l, ssq=hn[1])
    y, y_norm = res_mm(hid, 'w2_down', 512, h, 0.5, next_gain)
    hs = (b, t, N_HEADS_A, HEAD_DIM_A)
    va = zm3[..., ZM_VA:ZM_VA + A_W]
    shift_new = jnp.concatenate([zm3[:, -1, ZM_RKV:], zr3[:, -1, ZR_LORA:ZR_LORA + LORA_ALL]], axis=-1)
    p = zr3[..., ZR_D:ZR_D + D_W]
    pool_new = jnp.concatenate([pool_prefix, p], axis=1)[:, -POOL_PREV:]
    vc = zr3[..., ZR_CV:ZR_CV + C_W]
    states = (ka.reshape(hs), va.reshape(hs), wkv_new, shift_new, pool_new, vc)
    return y.reshape(b, t, D_MODEL), y_norm, states, wb


def kernel(x_prompt, x_sample, cache_k_swa, cache_v_swa, state_rwkv_wkv, state_rwkv_shift, state_pool, ln_ffn1, w1_gate, w1_up, w1_down, ln_mix, w_in, g_out_a, mu_b, w0, w_up, a0, a_up, g_up, k_k, k_a, r_k, ln_x_w, ln_x_b, w_s, b_s, g_out_c, w_pool, pool_scale, g_out_d, w_out, ln_ffn2, w2_gate, w2_up, w2_down, ln_final):
    w = dict(ln_ffn1=ln_ffn1, w1_gate=w1_gate, w1_up=w1_up, w1_down=w1_down, ln_mix=ln_mix, w_in=w_in,
             g_out_a=g_out_a, mu_b=mu_b, w0=w0, w_up=w_up, a0=a0, a_up=a_up, g_up=g_up, k_k=k_k, k_a=k_a,
             r_k=r_k, ln_x_w=ln_x_w, ln_x_b=ln_x_b, w_s=w_s, b_s=b_s, g_out_c=g_out_c, w_pool=w_pool,
             pool_scale=pool_scale, g_out_d=g_out_d, w_out=w_out, ln_ffn2=ln_ffn2, w2_gate=w2_gate,
             w2_up=w2_up, w2_down=w2_down)
    nbp, t_prompt, _ = x_prompt.shape
    yp, ys = x_prompt, x_sample
    p_states, s_states = [], []
    w['w_in_t'] = jnp.swapaxes(w_in, 1, 2)
    yp_norm = ys_norm = None
    for l in range(DEPTH):
        wl = _prep_layer_weights(l, w)
        next_gain = ln_ffn1[l + 1] if l + 1 < DEPTH else None
        ys, ys_norm, ss, wb = _layer(ys, l, w, wl, None, PAST_LEN, (cache_k_swa, cache_v_swa),
                                     state_rwkv_shift[l], state_rwkv_wkv[l], state_pool[l],
                                     x_norm=ys_norm, next_gain=next_gain)
        yp, yp_norm, sp, _ = _layer(yp, l, w, wl, wb, 0, None,
                                    jnp.zeros((nbp, B_FEAT), F32),
                                    jnp.zeros((nbp, N_HEADS_B, HEAD_DIM_B, HEAD_DIM_B), F32),
                                    jnp.zeros((nbp, POOL_PREV, D_W), F32),
                                    x_norm=yp_norm, next_gain=next_gain)
        p_states.append(sp)
        s_states.append(ss)
    keep = min(WIN_MAX, t_prompt)
    stack = lambda states, i: jnp.stack([s[i] for s in states])
    y_prompt = _rmsnorm(yp.reshape(-1, D_MODEL), ln_final, F32).reshape(yp.shape)
    y_sample = _rmsnorm(ys.reshape(-1, D_MODEL), ln_final, F32).reshape(ys.shape)
    return (y_prompt, y_sample,
            jnp.stack([s[0][:, -keep:] for s in p_states]), jnp.stack([s[1][:, -keep:] for s in p_states]),
            stack(p_states, 2), stack(p_states, 3), stack(p_states, 4),
            stack(s_states, 0), stack(s_states, 1), stack(s_states, 2), stack(s_states, 3), stack(s_states, 4),
            stack(s_states, 5))
```
